```python
import math
import jax, jax.numpy as jnp
from jax import lax
import numpy as np


D_MODEL = 2048
BATCH = 4
SEQ = 4096
DEPTH = 1

MIX_WIDTH = D_MODEL
HG_WIDTH = MIX_WIDTH // 2
DA_WIDTH = MIX_WIDTH - HG_WIDTH
HG_EXPAND = 128
HG_HEADS = HG_WIDTH // HG_EXPAND
HG_DK = HG_EXPAND
HG_DV = HG_WIDTH // HG_HEADS
HG_CHUNK = 32
DA_STD_HEAD = 128
DA_HEADS = DA_WIDTH // DA_STD_HEAD // 2
DA_HEAD_DIM = DA_WIDTH // DA_HEADS // 2
DA_BLOCK = 128
ROPE_THETA = 10000.0
D_FF = ((8 * D_MODEL // 3 + 255) // 256) * 256
DEEPNORM_ALPHA = (2.0 * DEPTH) ** 0.25
DEEPNORM_BETA = (8.0 * DEPTH) ** -0.25
LN_EPS = 1e-5
RMS_EPS = 1e-6
IN_COLS = 4 * HG_WIDTH + 3 * DA_WIDTH

kernel_name = "hymba_hgrn2_diffattn_deepnorm_adaln"


def layer_norm_plain(x):
    xf = x.astype(jnp.float32)
    mu = jnp.mean(xf, -1, keepdims=True)
    var = jnp.mean(jnp.square(xf - mu), -1, keepdims=True)
    return ((xf - mu) * lax.rsqrt(var + LN_EPS)).astype(x.dtype)


def layer_norm_affine(x, g, b):
    xf = x.astype(jnp.float32)
    mu = jnp.mean(xf, -1, keepdims=True)
    var = jnp.mean(jnp.square(xf - mu), -1, keepdims=True)
    y = (xf - mu) * lax.rsqrt(var + LN_EPS) * g.astype(jnp.float32) + b.astype(jnp.float32)
    return y.astype(x.dtype)


def rms_norm(x, w):
    xf = x.astype(jnp.float32)
    return xf * lax.rsqrt(jnp.mean(xf * xf, -1, keepdims=True) + RMS_EPS) * w.astype(jnp.float32)


def rope_tables(positions, dim):
    inv = 1.0 / (ROPE_THETA ** (jnp.arange(0, dim, 2, dtype=jnp.float32) / dim))
    ang = positions.astype(jnp.float32)[..., None] * inv
    return jnp.cos(ang), jnp.sin(ang)


def apply_rope(x, cos, sin):
    x1, x2 = jnp.split(x.astype(jnp.float32), 2, axis=-1)
    c = cos[:, :, None, :]
    s = sin[:, :, None, :]
    return jnp.concatenate([x1 * c - x2 * s, x2 * c + x1 * s], axis=-1)


def hgrn2_mixer(q, f_logit, i, g, lb, norm_w):
    B, S, _ = q.shape
    C = HG_CHUNK
    N = S // C
    lbf = lb.astype(jnp.float32)
    fl = f_logit.astype(jnp.float32)
    f = lbf + (1.0 - lbf) * jax.nn.sigmoid(fl)
    log_f = jnp.log(f)
    k = (1.0 - lbf) * jax.nn.sigmoid(-fl)

    def to_chunks(t, d):
        return t.reshape(B, N, C, HG_HEADS, d).transpose(0, 3, 1, 2, 4)

    qc = to_chunks(jax.nn.silu(q.astype(jnp.float32)) * (HG_DK ** -0.5), HG_DK)
    kc = to_chunks(k, HG_DK)
    vc = to_chunks(i.astype(jnp.float32), HG_DV)
    G = jnp.cumsum(to_chunks(log_f, HG_DK), axis=3)
    G_ref = G[:, :, :, C // 2 - 1:C // 2, :]
    G_last = G[:, :, :, C - 1:C, :]

    A = jnp.einsum('bhncd,bhnjd->bhncj', qc * jnp.exp(G - G_ref), kc * jnp.exp(G_ref - G))
    causal = jnp.tril(jnp.ones((C, C), dtype=bool))
    A = jnp.where(causal, A, 0.0)
    o_intra = jnp.einsum('bhncj,bhnjv->bhncv', A, vc)

    q_dec = jnp.moveaxis(qc * jnp.exp(G), 2, 0)
    k_dec = jnp.moveaxis(kc * jnp.exp(G_last - G), 2, 0)
    v_ch = jnp.moveaxis(vc, 2, 0)
    d_last = jnp.moveaxis(jnp.exp(G_last[:, :, :, 0, :]), 2, 0)

    def step(state, xs):
        qd, kd, v, dl = xs
        o = jnp.einsum('bhcd,bhdv->bhcv', qd, state)
        state = dl[..., None] * state + jnp.einsum('bhcd,bhcv->bhdv', kd, v)
        return state, o

    s0 = jnp.zeros((B, HG_HEADS, HG_DK, HG_DV), jnp.float32)
    _, o_inter = lax.scan(step, s0, (q_dec, k_dec, v_ch, d_last))
    o = o_intra + jnp.moveaxis(o_inter, 0, 2)
    o = o.transpose(0, 2, 3, 1, 4).reshape(B, S, HG_HEADS, HG_DV)
    o = rms_norm(o, norm_w) * jax.nn.sigmoid(g.astype(jnp.float32)).reshape(B, S, HG_HEADS, HG_DV)
    return o.reshape(B, S, HG_WIDTH)


def diff_attention(q, k, v, cos, sin, lam, subln_w, lambda_init):
    B, S, _ = q.shape
    nb = S // DA_BLOCK
    qh = apply_rope(q.reshape(B, S, 2 * DA_HEADS, DA_HEAD_DIM), cos, sin) * (DA_HEAD_DIM ** -0.5)
    kh = apply_rope(k.reshape(B, S, 2 * DA_HEADS, DA_HEAD_DIM), cos, sin)
    vh = v.astype(jnp.float32).reshape(B, S, DA_HEADS, 2 * DA_HEAD_DIM)
    qb = qh.reshape(B, nb, DA_BLOCK, 2 * DA_HEADS, DA_HEAD_DIM).transpose(1, 0, 3, 2, 4)
    kpos = jnp.arange(S)
    neg = jnp.finfo(jnp.float32).min

    def block(args):
        qi, idx = args
        s = jnp.einsum('bhqd,bkhd->bhqk', qi, kh)
        qpos = idx * DA_BLOCK + jnp.arange(DA_BLOCK)
        s = jnp.where(kpos[None, :] <= qpos[:, None], s, neg)
        p = jax.nn.softmax(s, axis=-1).reshape(B, DA_HEADS, 2, DA_BLOCK, S)
        a = p[:, :, 0] - lam * p[:, :, 1]
        return jnp.einsum('bhqk,bkhv->bhqv', a, vh)

    o = lax.map(block, (qb, jnp.arange(nb)))
    o = o.transpose(1, 0, 3, 2, 4).reshape(B, S, DA_HEADS, 2 * DA_HEAD_DIM)
    o = rms_norm(o, subln_w) * (1.0 - lambda_init)
    return o.reshape(B, S, DA_WIDTH)


def setup_inputs(seed: int = 0) -> dict:
    key = jax.random.key(seed)
    ks = jax.random.split(key, 24)
    nrm = jax.random.normal
    L = DEPTH
    x = nrm(ks[0], (BATCH, SEQ, D_MODEL), jnp.float32)
    c = nrm(ks[1], (BATCH, D_MODEL), jnp.float32)
    positions = jnp.broadcast_to(jnp.arange(SEQ, dtype=jnp.int32), (BATCH, SEQ))
    w_ada = nrm(ks[2], (L, D_MODEL, 6 * D_MODEL), jnp.float32) * D_MODEL ** -0.5
    b_ada = nrm(ks[3], (L, 6 * D_MODEL), jnp.float32) * 0.02
    col_scale = jnp.concatenate([
        jnp.ones((2 * HG_WIDTH,), jnp.float32),
        jnp.full((HG_WIDTH,), DEEPNORM_BETA, jnp.float32),
        jnp.ones((HG_WIDTH + 2 * DA_WIDTH,), jnp.float32),
        jnp.full((DA_WIDTH,), DEEPNORM_BETA, jnp.float32)])
    w_in = nrm(ks[4], (L, D_MODEL, IN_COLS), jnp.float32) * D_MODEL ** -0.5 * col_scale
    lb_logits = nrm(ks[5], (L + 1, HG_WIDTH), jnp.float32)
    hg_norm_w = 1.0 + 0.02 * nrm(ks[6], (L, HG_DV), jnp.float32)
    lam_q1 = 0.1 * nrm(ks[7], (L, DA_HEAD_DIM), jnp.float32)
    lam_k1 = 0.1 * nrm(ks[8], (L, DA_HEAD_DIM), jnp.float32)
    lam_q2 = 0.1 * nrm(ks[9], (L, DA_HEAD_DIM), jnp.float32)
    lam_k2 = 0.1 * nrm(ks[10], (L, DA_HEAD_DIM), jnp.float32)
    subln_w = 1.0 + 0.02 * nrm(ks[11], (L, 2 * DA_HEAD_DIM), jnp.float32)
    w_out = nrm(ks[12], (L, MIX_WIDTH, D_MODEL), jnp.float32) * MIX_WIDTH ** -0.5 * DEEPNORM_BETA
    ln1_g = 1.0 + 0.02 * nrm(ks[13], (L, D_MODEL), jnp.float32)
    ln1_b = 0.02 * nrm(ks[14], (L, D_MODEL), jnp.float32)
    w_gate = nrm(ks[15], (L, D_MODEL, D_FF), jnp.float32) * D_MODEL ** -0.5
    w_up = nrm(ks[16], (L, D_MODEL, D_FF), jnp.float32) * D_MODEL ** -0.5
    w_down = nrm(ks[17], (L, D_FF, D_MODEL), jnp.float32) * D_FF ** -0.5 * DEEPNORM_BETA
    ln2_g = 1.0 + 0.02 * nrm(ks[18], (L, D_MODEL), jnp.float32)
    ln2_b = 0.02 * nrm(ks[19], (L, D_MODEL), jnp.float32)
    return {"x": x, "c": c, "positions": positions, "w_ada": w_ada, "b_ada": b_ada,
            "w_in": w_in, "lb_logits": lb_logits, "hg_norm_w": hg_norm_w,
            "lam_q1": lam_q1, "lam_k1": lam_k1, "lam_q2": lam_q2, "lam_k2": lam_k2,
            "subln_w": subln_w, "w_out": w_out, "ln1_g": ln1_g, "ln1_b": ln1_b,
            "w_gate": w_gate, "w_up": w_up, "w_down": w_down, "ln2_g": ln2_g, "ln2_b": ln2_b}


def reference(x, c, positions, w_ada, b_ada, w_in, lb_logits, hg_norm_w, lam_q1, lam_k1,
              lam_q2, lam_k2, subln_w, w_out, ln1_g, ln1_b, w_gate, w_up, w_down, ln2_g, ln2_b):
    cos, sin = rope_tables(positions, DA_HEAD_DIM)
    lb_all = jnp.cumsum(jax.nn.softmax(lb_logits.astype(jnp.float32), axis=0), axis=0)
    cond = jax.nn.silu(c)
    h = x
    split_pts = [HG_WIDTH, 2 * HG_WIDTH, 3 * HG_WIDTH, 4 * HG_WIDTH,
                 4 * HG_WIDTH + DA_WIDTH, 4 * HG_WIDTH + 2 * DA_WIDTH]
    for l in range(DEPTH):
        mod = jnp.einsum('bd,de->be', cond, w_ada[l]) + b_ada[l]
        sh1, sc1, g1, sh2, sc2, g2 = jnp.split(mod[:, None, :], 6, axis=-1)

        u = layer_norm_plain(h) * (1.0 + sc1) + sh1
        proj = jnp.einsum('bsd,de->bse', u, w_in[l])
        hq, hf, hi, hg, dq, dk, dv = jnp.split(proj, split_pts, axis=-1)
        y_hg = hgrn2_mixer(hq, hf, hi, hg, lb_all[l], hg_norm_w[l])
        lambda_init = 0.8 - 0.6 * math.exp(-0.3 * l)
        lam = (jnp.exp(jnp.sum(lam_q1[l].astype(jnp.float32) * lam_k1[l].astype(jnp.float32)))
               - jnp.exp(jnp.sum(lam_q2[l].astype(jnp.float32) * lam_k2[l].astype(jnp.float32)))
               + lambda_init)
        y_da = diff_attention(dq, dk, dv, cos, sin, lam, subln_w[l], lambda_init)
        mix = jnp.concatenate([y_hg, y_da], axis=-1).astype(h.dtype)
        y = jnp.einsum('bsm,md->bsd', mix, w_out[l])
        h = layer_norm_affine(DEEPNORM_ALPHA * h + g1 * y, ln1_g[l], ln1_b[l])

        u = layer_norm_plain(h) * (1.0 + sc2) + sh2
        a = jnp.einsum('bsd,df->bsf', u, w_gate[l])
        b = jnp.einsum('bsd,df->bsf', u, w_up[l])
        y = jnp.einsum('bsf,fd->bsd', jax.nn.silu(a) * b, w_down[l])
        h = layer_norm_affine(DEEPNORM_ALPHA * h + g2 * y, ln2_g[l], ln2_b[l])
    return h
```

```python
import functools
import math

import numpy as np
import jax
import jax.numpy as jnp
from jax import lax
from jax.experimental import pallas as pl
from jax.experimental.pallas import tpu as pltpu

D_MODEL = 2048
HG_WIDTH = 1024
DA_WIDTH = 1024
HG_HEADS = 8
HG_DK = 128
HG_CHUNK = 32
DA_HEADS = 4
DA_HEAD_DIM = 128
ROPE_THETA = 10000.0
D_FF = 5632
DEPTH = 1
DEEPNORM_ALPHA = (2.0 * DEPTH) ** 0.25
LN_EPS = 1e-5
RMS_EPS = 1e-6
LAMBDA_INIT = 0.8 - 0.6 * math.exp(-0.3 * 0)
SEG = 1024
N_SEG = 7

VMEM_LIMIT = 56 * 1024 * 1024

F32 = jnp.float32
BF16 = jnp.bfloat16


def _dot(a, b):
    return jnp.dot(a, b, preferred_element_type=F32)


def _dot_nt(a, b):
    return lax.dot_general(a, b, (((1,), (1,)), ((), ())), preferred_element_type=F32)


def _dot_tn(a, b):
    return lax.dot_general(a, b, (((0,), (0,)), ((), ())), preferred_element_type=F32)


def _ln_plain(x):
    mu = jnp.mean(x, -1, keepdims=True)
    xc = x - mu
    var = jnp.mean(xc * xc, -1, keepdims=True)
    return xc * lax.rsqrt(var + LN_EPS)


def _adaln_kernel(c_ref, w_ref, b_ref, o_ref):
    cond = c_ref[...]
    cond = cond * jax.nn.sigmoid(cond)
    o_ref[...] = _dot(cond, w_ref[...]) + b_ref[...]


def _adaln(c, w, b):
    bsz, d = c.shape
    n = w.shape[1]
    tn = 1024
    return pl.pallas_call(
        _adaln_kernel,
        grid=(n // tn,),
        in_specs=[pl.BlockSpec((bsz, d), lambda j: (0, 0)),
                  pl.BlockSpec((d, tn), lambda j: (0, j)),
                  pl.BlockSpec((1, tn), lambda j: (0, j))],
        out_specs=pl.BlockSpec((bsz, tn), lambda j: (0, j)),
        out_shape=jax.ShapeDtypeStruct((bsz, n), F32),
        compiler_params=pltpu.CompilerParams(dimension_semantics=("arbitrary",),
                                             vmem_limit_bytes=VMEM_LIMIT),
        name="adaln",
    )(c, w, b.reshape(1, n))


def _rope_kernel(pos_ref, inv_ref, sign_ref, cos_ref, sin_ref):
    ang = pos_ref[0] * inv_ref[...]
    cos_ref[0] = jnp.cos(ang)
    sin_ref[0] = jnp.sin(ang) * sign_ref[...]


def _rope_tables(positions):
    bsz, s = positions.shape
    ts = 512
    half = DA_HEAD_DIM // 2
    inv = 1.0 / (ROPE_THETA ** (jnp.arange(0, DA_HEAD_DIM, 2, dtype=F32) / DA_HEAD_DIM))
    inv2 = jnp.concatenate([inv, inv]).reshape(1, DA_HEAD_DIM)
    sign = jnp.concatenate([-jnp.ones((half,), F32), jnp.ones((half,), F32)]).reshape(1, DA_HEAD_DIM)
    pos = positions.astype(F32).reshape(bsz, s, 1)
    tab = jax.ShapeDtypeStruct((bsz, s, DA_HEAD_DIM), F32)
    return pl.pallas_call(
        _rope_kernel,
        grid=(bsz, s // ts),
        in_specs=[pl.BlockSpec((1, ts, 1), lambda b, i: (b, i, 0)),
                  pl.BlockSpec((1, DA_HEAD_DIM), lambda b, i: (0, 0)),
                  pl.BlockSpec((1, DA_HEAD_DIM), lambda b, i: (0, 0))],
        out_specs=[pl.BlockSpec((1, ts, DA_HEAD_DIM), lambda b, i: (b, i, 0)),
                   pl.BlockSpec((1, ts, DA_HEAD_DIM), lambda b, i: (b, i, 0))],
        out_shape=[tab, tab],
        compiler_params=pltpu.CompilerParams(dimension_semantics=("arbitrary", "arbitrary")),
        name="rope_tables",
    )(pos, inv2, sign)


def _rope_apply(acc, cos, sin, scale):
    outs = []
    for h in range(SEG // DA_HEAD_DIM):
        xh = acc[:, h * DA_HEAD_DIM:(h + 1) * DA_HEAD_DIM]
        rot = pltpu.roll(xh, DA_HEAD_DIM // 2, 1)
        y = xh * cos + rot * sin
        if scale is not None:
            y = y * scale
        outs.append(y.astype(BF16))
    return jnp.concatenate(outs, axis=-1)


def _inproj_kernel(x_ref, mod_ref, w_ref, lbl_ref, cos_ref, sin_ref,
                   oq_ref, olf_ref, ok_ref, ov_ref, og_ref, odq_ref, odk_ref, odv_ref, u_ref):
    j = pl.program_id(2)

    @pl.when(j == 0)
    def _():
        m = mod_ref[0]
        u = _ln_plain(x_ref[0]) * (1.0 + m[1:2, :]) + m[0:1, :]
        u_ref[...] = u.astype(BF16)

    acc = _dot(u_ref[...], w_ref[...])

    @pl.when(j == 0)
    def _():
        oq_ref[0] = (acc * jax.nn.sigmoid(acc) * (HG_DK ** -0.5)).astype(BF16)

    @pl.when(j == 1)
    def _():
        lbl = lbl_ref[...]
        e = jnp.exp(lbl - jnp.max(lbl, axis=0, keepdims=True))
        lb = e[0:1, :] / jnp.sum(e, axis=0, keepdims=True)
        f = lb + (1.0 - lb) * jax.nn.sigmoid(acc)
        olf_ref[0] = jnp.log(f)
        ok_ref[0] = ((1.0 - lb) * jax.nn.sigmoid(-acc)).astype(BF16)

    @pl.when(j == 2)
    def _():
        ov_ref[0] = acc.astype(BF16)

    @pl.when(j == 3)
    def _():
        og_ref[0] = jax.nn.sigmoid(acc).astype(BF16)

    @pl.when(j == 4)
    def _():
        odq_ref[0] = _rope_apply(acc, cos_ref[0], sin_ref[0], DA_HEAD_DIM ** -0.5)

    @pl.when(j == 5)
    def _():
        odk_ref[0] = _rope_apply(acc, cos_ref[0], sin_ref[0], None)

    @pl.when(j == 6)
    def _():
        odv_ref[0] = acc.astype(BF16)


def _inproj(x, mod, w_in, lb_logits, cos, sin):
    bsz, s, d = x.shape
    tm = 512
    row = lambda b, i, j: (b, i, 0)
    seg_bf = jax.ShapeDtypeStruct((bsz, s, SEG), BF16)
    seg_f32 = jax.ShapeDtypeStruct((bsz, s, SEG), F32)
    out_spec = pl.BlockSpec((1, tm, SEG), row)
    return pl.pallas_call(
        _inproj_kernel,
        grid=(bsz, s // tm, N_SEG),
        in_specs=[pl.BlockSpec((1, tm, d), row),
                  pl.BlockSpec((1, 6, d), lambda b, i, j: (b, 0, 0)),
                  pl.BlockSpec((d, SEG), lambda b, i, j: (0, j)),
                  pl.BlockSpec(lb_logits.shape, lambda b, i, j: (0, 0)),
                  pl.BlockSpec((1, tm, DA_HEAD_DIM), row),
                  pl.BlockSpec((1, tm, DA_HEAD_DIM), row)],
        out_specs=[out_spec] * 8,
        out_shape=[seg_bf, seg_f32, seg_bf, seg_bf, seg_bf, seg_bf, seg_bf, seg_bf],
        scratch_shapes=[pltpu.VMEM((tm, d), BF16)],
        compiler_params=pltpu.CompilerParams(
            dimension_semantics=("arbitrary", "arbitrary", "arbitrary"),
            vmem_limit_bytes=VMEM_LIMIT),
        name="inproj",
    )(x, mod, w_in, lb_logits, cos, sin)


HG_BLOCK = 256


def _hgrn_consts():
    L, C = HG_BLOCK, HG_CHUNK
    r = np.arange(L)[:, None]
    c = np.arange(L)[None, :]
    same = (r // C) == (c // C)
    t_cum = same & (c <= r)
    t_ref = same & ((c % C) <= C // 2 - 1)
    t_last = same
    tmat = np.concatenate([t_cum, t_ref, t_last], axis=0).astype(np.float32)
    return jnp.asarray(tmat, BF16), jnp.asarray(t_cum.astype(np.float32))


def _hgrn_kernel(q_ref, lf_ref, k_ref, v_ref, g_ref, nw_ref, tmat_ref, mask_ref, o_ref, st_ref):
    L, C = HG_BLOCK, HG_CHUNK

    @pl.when(pl.program_id(2) == 0)
    def _():
        st_ref[...] = jnp.zeros_like(st_ref)

    q = q_ref[0].astype(F32)
    k = k_ref[0].astype(F32)
    v = v_ref[0]
    lf = lf_ref[0]

    hi = lf.astype(BF16)
    r1 = lf - hi.astype(F32)
    mid = r1.astype(BF16)
    lo = (r1 - mid.astype(F32)).astype(BF16)
    tmat = tmat_ref[...]
    gg = _dot(tmat, hi) + _dot(tmat, mid) + _dot(tmat, lo)
    g_cum = gg[:L]
    g_mid = gg[L:2 * L]
    g_last = gg[2 * L:]

    qa = (q * jnp.exp(g_cum - g_mid)).astype(BF16)
    ka = (k * jnp.exp(g_mid - g_cum)).astype(BF16)
    qd = (q * jnp.exp(g_cum)).astype(BF16)
    kd = (k * jnp.exp(g_last - g_cum)).astype(BF16)
    dl = jnp.exp(g_last)

    a = _dot_nt(qa, ka)
    a = jnp.where(mask_ref[...] > 0.0, a, 0.0).astype(BF16)
    o_intra = _dot(a, v)

    st = st_ref[...]
    outs = []
    for c in range(L // C):
        rows = slice(c * C, (c + 1) * C)
        o_inter = _dot_nt(qd[rows], st.astype(BF16))
        outs.append(o_intra[rows] + o_inter)
        st = st * dl[c * C:c * C + 1, :] + _dot_tn(v[rows], kd[rows])
    st_ref[...] = st

    o = jnp.concatenate(outs, axis=0)
    ms = jnp.mean(o * o, -1, keepdims=True)
    y = o * lax.rsqrt(ms + RMS_EPS) * nw_ref[...] * g_ref[0].astype(F32)
    o_ref[0] = y.astype(BF16)


def _hgrn(qs, lf, ks, vs, gs, norm_w):
    bsz, s, _ = qs.shape
    L = HG_BLOCK
    tmat, mask = _hgrn_consts()
    blk = pl.BlockSpec((1, L, HG_DK), lambda b, h, n: (b, n, h))
    const = lambda shape: pl.BlockSpec(shape, lambda b, h, n: (0, 0))
    return pl.pallas_call(
        _hgrn_kernel,
        grid=(bsz, HG_HEADS, s // L),
        in_specs=[blk, blk, blk, blk, blk, const((1, HG_DK)), const((3 * L, L)), const((L, L))],
        out_specs=blk,
        out_shape=jax.ShapeDtypeStruct((bsz, s, HG_WIDTH), BF16),
        scratch_shapes=[pltpu.VMEM((HG_DK, HG_DK), F32)],
        compiler_params=pltpu.CompilerParams(
            dimension_semantics=("arbitrary", "arbitrary", "arbitrary"),
            vmem_limit_bytes=VMEM_LIMIT),
        name="hgrn",
    )(qs, lf, ks, vs, gs, norm_w.reshape(1, HG_DK), tmat, mask)


ATT_BLOCK = 512


def _attn_kernel(q_ref, k_ref, v_ref, lq1_ref, lk1_ref, lq2_ref, lk2_ref, sw_ref, o_ref,
                 acc1_ref, acc2_ref, m1_ref, l1_ref, m2_ref, l2_ref):
    T = ATT_BLOCK
    D = DA_HEAD_DIM
    i = pl.program_id(2)
    q = q_ref[0]
    q1 = q[:, :D]
    q2 = q[:, D:]

    for ref in (acc1_ref, acc2_ref, l1_ref, l2_ref):
        ref[...] = jnp.zeros_like(ref)
    m1_ref[...] = jnp.full_like(m1_ref, -jnp.inf)
    m2_ref[...] = jnp.full_like(m2_ref, -jnp.inf)

    def step(j, masked):
        start = pl.multiple_of(j * T, T)
        kb = k_ref[0, pl.ds(start, T), :]
        vb = v_ref[0, pl.ds(start, T), :]
        if masked:
            row = lax.broadcasted_iota(jnp.int32, (T, T), 0)
            col = lax.broadcasted_iota(jnp.int32, (T, T), 1)
            keep = col <= row
        for qq, kk, acc_ref, m_ref, l_ref in ((q1, kb[:, :D], acc1_ref, m1_ref, l1_ref),
                                              (q2, kb[:, D:], acc2_ref, m2_ref, l2_ref)):
            s = _dot_nt(qq, kk)
            if masked:
                s = jnp.where(keep, s, jnp.finfo(F32).min)
            m_prev = m_ref[...]
            m_new = jnp.maximum(m_prev, jnp.max(s, axis=-1, keepdims=True))
            alpha = jnp.exp(m_prev - m_new)
            p = jnp.exp(s - m_new)
            l_ref[...] = alpha * l_ref[...] + jnp.sum(p, axis=-1, keepdims=True)
            acc_ref[...] = alpha * acc_ref[...] + _dot(p.astype(BF16), vb)
            m_ref[...] = m_new

    def body(j, carry):
        step(j, False)
        return carry

    lax.fori_loop(0, i, body, 0)
    step(i, True)

    lam = (jnp.exp(jnp.sum(lq1_ref[...] * lk1_ref[...], axis=-1, keepdims=True))
           - jnp.exp(jnp.sum(lq2_ref[...] * lk2_ref[...], axis=-1, keepdims=True))
           + LAMBDA_INIT)
    o = acc1_ref[...] / l1_ref[...] - lam * (acc2_ref[...] / l2_ref[...])
    ms = jnp.mean(o * o, -1, keepdims=True)
    y = o * lax.rsqrt(ms + RMS_EPS) * sw_ref[...] * (1.0 - LAMBDA_INIT)
    o_ref[0] = y.astype(BF16)


def _attn(q, k, v, lq1, lk1, lq2, lk2, subln_w):
    bsz, s, _ = q.shape
    T = ATT_BLOCK
    hw = 2 * DA_HEAD_DIM
    qblk = pl.BlockSpec((1, T, hw), lambda b, h, i: (b, i, h))
    kvblk = pl.BlockSpec((1, s, hw), lambda b, h, i: (b, 0, h))
    vec = lambda n: pl.BlockSpec((1, n), lambda b, h, i: (0, 0))
    return pl.pallas_call(
        _attn_kernel,
        grid=(bsz, DA_HEADS, s // T),
        in_specs=[qblk, kvblk, kvblk, vec(DA_HEAD_DIM), vec(DA_HEAD_DIM), vec(DA_HEAD_DIM),
                  vec(DA_HEAD_DIM), vec(hw)],
        out_specs=qblk,
        out_shape=jax.ShapeDtypeStruct((bsz, s, DA_WIDTH), BF16),
        scratch_shapes=[pltpu.VMEM((T, hw), F32), pltpu.VMEM((T, hw), F32),
                        pltpu.VMEM((T, 1), F32), pltpu.VMEM((T, 1), F32),
                        pltpu.VMEM((T, 1), F32), pltpu.VMEM((T, 1), F32)],
        compiler_params=pltpu.CompilerParams(
            dimension_semantics=("arbitrary", "arbitrary", "arbitrary"),
            vmem_limit_bytes=VMEM_LIMIT),
        name="diff_attn",
    )(q, k, v, lq1.reshape(1, -1), lk1.reshape(1, -1), lq2.reshape(1, -1), lk2.reshape(1, -1),
      subln_w.reshape(1, -1))


def _outproj_kernel(yh_ref, ya_ref, w_ref, x_ref, mod_ref, g_ref, b_ref, h_ref, u_ref):
    m = mod_ref[0]
    y = _dot(yh_ref[0], w_ref[:HG_WIDTH, :]) + _dot(ya_ref[0], w_ref[HG_WIDTH:, :])
    r = DEEPNORM_ALPHA * x_ref[0] + m[2:3, :] * y
    h = _ln_plain(r) * g_ref[...] + b_ref[...]
    h_ref[0] = h
    u = _ln_plain(h) * (1.0 + m[4:5, :]) + m[3:4, :]
    u_ref[0] = u.astype(BF16)


def _outproj(y_hg, y_da, w_out, x, mod, ln_g, ln_b):
    bsz, s, d = x.shape
    tm = 256
    row = lambda b, i: (b, i, 0)
    const2 = lambda shape: pl.BlockSpec(shape, lambda b, i: (0, 0))
    return pl.pallas_call(
        _outproj_kernel,
        grid=(bsz, s // tm),
        in_specs=[pl.BlockSpec((1, tm, HG_WIDTH), row),
                  pl.BlockSpec((1, tm, DA_WIDTH), row),
                  const2(w_out.shape),
                  pl.BlockSpec((1, tm, d), row),
                  pl.BlockSpec((1, 6, d), lambda b, i: (b, 0, 0)),
                  const2((1, d)), const2((1, d))],
        out_specs=[pl.BlockSpec((1, tm, d), row), pl.BlockSpec((1, tm, d), row)],
        out_shape=[jax.ShapeDtypeStruct((bsz, s, d), F32), jax.ShapeDtypeStruct((bsz, s, d), BF16)],
        compiler_params=pltpu.CompilerParams(dimension_semantics=("arbitrary", "arbitrary"),
                                             vmem_limit_bytes=VMEM_LIMIT),
        name="outproj_ln1",
    )(y_hg, y_da, w_out, x, mod, ln_g.reshape(1, d), ln_b.reshape(1, d))


def _ffn_kernel(u_ref, wg_ref, wu_ref, wd_ref, h_ref, mod_ref, g_ref, b_ref, o_ref, acc_ref):
    j = pl.program_id(2)

    @pl.when(j == 0)
    def _():
        acc_ref[...] = jnp.zeros_like(acc_ref)

    u = u_ref[0]
    a = _dot(u, wg_ref[...])
    b = _dot(u, wu_ref[...])
    z = (a * jax.nn.sigmoid(a) * b).astype(BF16)
    acc_ref[...] += _dot(z, wd_ref[...])

    @pl.when(j == pl.num_programs(2) - 1)
    def _():
        m = mod_ref[0]
        r = DEEPNORM_ALPHA * h_ref[0] + m[5:6, :] * acc_ref[...]
        o_ref[0] = _ln_plain(r) * g_ref[...] + b_ref[...]


def _ffn(u, w_gate, w_up, w_down, h, mod, ln_g, ln_b):
    bsz, s, d = h.shape
    f = w_gate.shape[1]
    tm, tf = 512, 512
    row = lambda b, i, j: (b, i, 0)
    const3 = lambda shape: pl.BlockSpec(shape, lambda b, i, j: (0, 0))
    return pl.pallas_call(
        _ffn_kernel,
        grid=(bsz, s // tm, f // tf),
        in_specs=[pl.BlockSpec((1, tm, d), row),
                  pl.BlockSpec((d, tf), lambda b, i, j: (0, j)),
                  pl.BlockSpec((d, tf), lambda b, i, j: (0, j)),
                  pl.BlockSpec((tf, d), lambda b, i, j: (j, 0)),
                  pl.BlockSpec((1, tm, d), row),
                  pl.BlockSpec((1, 6, d), lambda b, i, j: (b, 0, 0)),
                  const3((1, d)), const3((1, d))],
        out_specs=pl.BlockSpec((1, tm, d), row),
        out_shape=jax.ShapeDtypeStruct((bsz, s, d), F32),
        scratch_shapes=[pltpu.VMEM((tm, d), F32)],
        compiler_params=pltpu.CompilerParams(
            dimension_semantics=("arbitrary", "arbitrary", "arbitrary"),
            vmem_limit_bytes=VMEM_LIMIT),
        name="ffn_ln2",
    )(u, w_gate, w_up, w_down, h, mod, ln_g.reshape(1, d), ln_b.reshape(1, d))


def kernel(x, c, positions, w_ada, b_ada, w_in, lb_logits, hg_norm_w, lam_q1, lam_k1, lam_q2, lam_k2,
           subln_w, w_out, ln1_g, ln1_b, w_gate, w_up, w_down, ln2_g, ln2_b):
    assert w_ada.shape[0] == DEPTH == 1
    bsz, s, d = x.shape

    mod = _adaln(c, w_ada[0], b_ada[0]).reshape(bsz, 6, d)
    cos, sin = _rope_tables(positions)

    qs, lf, ks, vs, gs, dq, dk, dv = _inproj(x, mod, w_in[0].astype(BF16), lb_logits, cos, sin)
    y_hg = _hgrn(qs, lf, ks, vs, gs, hg_norm_w[0])
    y_da = _attn(dq, dk, dv, lam_q1[0], lam_k1[0], lam_q2[0], lam_k2[0], subln_w[0])
    h1, u2 = _outproj(y_hg, y_da, w_out[0].astype(BF16), x, mod, ln1_g[0], ln1_b[0])
    return _ffn(u2, w_gate[0].astype(BF16), w_up[0].astype(BF16), w_down[0].astype(BF16),
                h1, mod, ln2_g[0], ln2_b[0])
```

```python
import functools
import math

import numpy as np
import jax
import jax.numpy as jnp
from jax import lax
from jax.experimental import pallas as pl
from jax.experimental.pallas import tpu as pltpu

D_MODEL = 2048
HG_WIDTH = 1024
DA_WIDTH = 1024
HG_HEADS = 8
HG_DK = 128
HG_CHUNK = 32
DA_HEADS = 4
DA_HEAD_DIM = 128
ROPE_THETA = 10000.0
D_FF = 5632
DEPTH = 1
DEEPNORM_ALPHA = (2.0 * DEPTH) ** 0.25
LN_EPS = 1e-5
RMS_EPS = 1e-6
LAMBDA_INIT = 0.8 - 0.6 * math.exp(-0.3 * 0)
SEG = 1024
N_SEG = 7

VMEM_LIMIT = 56 * 1024 * 1024

F32 = jnp.float32
BF16 = jnp.bfloat16


def _dot(a, b):
    return jnp.dot(a, b, preferred_element_type=F32)


def _dot_nt(a, b):
    return lax.dot_general(a, b, (((1,), (1,)), ((), ())), preferred_element_type=F32)


def _dot_tn(a, b):
    return lax.dot_general(a, b, (((0,), (0,)), ((), ())), preferred_element_type=F32)


def _ln_plain(x):
    mu = jnp.mean(x, -1, keepdims=True)
    xc = x - mu
    var = jnp.mean(xc * xc, -1, keepdims=True)
    return xc * lax.rsqrt(var + LN_EPS)


def _adaln_kernel(c_ref, w_ref, b_ref, o_ref):
    cond = c_ref[...]
    cond = cond * jax.nn.sigmoid(cond)
    o_ref[...] = _dot(cond, w_ref[...]) + b_ref[...]


def _adaln(c, w, b):
    bsz, d = c.shape
    n = w.shape[1]
    tn = 1024
    return pl.pallas_call(
        _adaln_kernel,
        grid=(n // tn,),
        in_specs=[pl.BlockSpec((bsz, d), lambda j: (0, 0)),
                  pl.BlockSpec((d, tn), lambda j: (0, j)),
                  pl.BlockSpec((1, tn), lambda j: (0, j))],
        out_specs=pl.BlockSpec((bsz, tn), lambda j: (0, j)),
        out_shape=jax.ShapeDtypeStruct((bsz, n), F32),
        compiler_params=pltpu.CompilerParams(dimension_semantics=("arbitrary",),
                                             vmem_limit_bytes=VMEM_LIMIT),
        name="adaln",
    )(c, w, b.reshape(1, n))


def _rope_kernel(pos_ref, inv_ref, sign_ref, cos_ref, sin_ref):
    ang = pos_ref[0] * inv_ref[...]
    cos_ref[0] = jnp.cos(ang)
    sin_ref[0] = jnp.sin(ang) * sign_ref[...]


def _rope_tables(positions):
    bsz, s = positions.shape
    ts = 512
    half = DA_HEAD_DIM // 2
    inv = 1.0 / (ROPE_THETA ** (jnp.arange(0, DA_HEAD_DIM, 2, dtype=F32) / DA_HEAD_DIM))
    inv2 = jnp.concatenate([inv, inv]).reshape(1, DA_HEAD_DIM)
    sign = jnp.concatenate([-jnp.ones((half,), F32), jnp.ones((half,), F32)]).reshape(1, DA_HEAD_DIM)
    pos = positions.astype(F32).reshape(bsz, s, 1)
    tab = jax.ShapeDtypeStruct((bsz, s, DA_HEAD_DIM), F32)
    return pl.pallas_call(
        _rope_kernel,
        grid=(bsz, s // ts),
        in_specs=[pl.BlockSpec((1, ts, 1), lambda b, i: (b, i, 0)),
                  pl.BlockSpec((1, DA_HEAD_DIM), lambda b, i: (0, 0)),
                  pl.BlockSpec((1, DA_HEAD_DIM), lambda b, i: (0, 0))],
        out_specs=[pl.BlockSpec((1, ts, DA_HEAD_DIM), lambda b, i: (b, i, 0)),
                   pl.BlockSpec((1, ts, DA_HEAD_DIM), lambda b, i: (b, i, 0))],
        out_shape=[tab, tab],
        compiler_params=pltpu.CompilerParams(dimension_semantics=("arbitrary", "arbitrary")),
        name="rope_tables",
    )(pos, inv2, sign)


def _rope_apply(acc, cos, sin, scale):
    outs = []
    for h in range(SEG // DA_HEAD_DIM):
        xh = acc[:, h * DA_HEAD_DIM:(h + 1) * DA_HEAD_DIM]
        rot = pltpu.roll(xh, DA_HEAD_DIM // 2, 1)
        y = xh * cos + rot * sin
        if scale is not None:
            y = y * scale
        outs.append(y.astype(BF16))
    return jnp.concatenate(outs, axis=-1)


def _inproj_kernel(x_ref, mod_ref, w_ref, lbl_ref, cos_ref, sin_ref,
                   oq_ref, olf_ref, ok_ref, ov_ref, og_ref, odq_ref, odk_ref, odv_ref, u_ref):
    j = pl.program_id(2)

    @pl.when(j == 0)
    def _():
        m = mod_ref[0]
        u = _ln_plain(x_ref[0]) * (1.0 + m[1:2, :]) + m[0:1, :]
        u_ref[...] = u.astype(BF16)

    acc = _dot(u_ref[...], w_ref[...])

    @pl.when(j == 0)
    def _():
        oq_ref[0] = (acc * jax.nn.sigmoid(acc) * (HG_DK ** -0.5)).astype(BF16)

    @pl.when(j == 1)
    def _():
        lbl = lbl_ref[...]
        e = jnp.exp(lbl - jnp.max(lbl, axis=0, keepdims=True))
        lb = e[0:1, :] / jnp.sum(e, axis=0, keepdims=True)
        f = lb + (1.0 - lb) * jax.nn.sigmoid(acc)
        olf_ref[0] = jnp.log(f)
        ok_ref[0] = ((1.0 - lb) * jax.nn.sigmoid(-acc)).astype(BF16)

    @pl.when(j == 2)
    def _():
        ov_ref[0] = acc.astype(BF16)

    @pl.when(j == 3)
    def _():
        og_ref[0] = jax.nn.sigmoid(acc).astype(BF16)

    @pl.when(j == 4)
    def _():
        odq_ref[0] = _rope_apply(acc, cos_ref[0], sin_ref[0], DA_HEAD_DIM ** -0.5 * math.log2(math.e))

    @pl.when(j == 5)
    def _():
        odk_ref[0] = _rope_apply(acc, cos_ref[0], sin_ref[0], None)

    @pl.when(j == 6)
    def _():
        odv_ref[0] = acc.astype(BF16)


def _inproj(x, mod, w_in, lb_logits, cos, sin):
    bsz, s, d = x.shape
    tm = 512
    row = lambda b, i, j: (b, i, 0)
    seg_bf = jax.ShapeDtypeStruct((bsz, s, SEG), BF16)
    seg_f32 = jax.ShapeDtypeStruct((bsz, s, SEG), F32)
    out_spec = pl.BlockSpec((1, tm, SEG), row)
    return pl.pallas_call(
        _inproj_kernel,
        grid=(bsz, s // tm, N_SEG),
        in_specs=[pl.BlockSpec((1, tm, d), row),
                  pl.BlockSpec((1, 6, d), lambda b, i, j: (b, 0, 0)),
                  pl.BlockSpec((d, SEG), lambda b, i, j: (0, j)),
                  pl.BlockSpec(lb_logits.shape, lambda b, i, j: (0, 0)),
                  pl.BlockSpec((1, tm, DA_HEAD_DIM), row),
                  pl.BlockSpec((1, tm, DA_HEAD_DIM), row)],
        out_specs=[out_spec] * 8,
        out_shape=[seg_bf, seg_f32, seg_bf, seg_bf, seg_bf, seg_bf, seg_bf, seg_bf],
        scratch_shapes=[pltpu.VMEM((tm, d), BF16)],
        compiler_params=pltpu.CompilerParams(
            dimension_semantics=("arbitrary", "arbitrary", "arbitrary"),
            vmem_limit_bytes=VMEM_LIMIT),
        name="inproj",
    )(x, mod, w_in, lb_logits, cos, sin)


HG_BLOCK = 256
HG_HEADS_PER_STEP = 8


def _hgrn_consts():
    L, C = HG_BLOCK, HG_CHUNK
    r = np.arange(L)[:, None]
    c = np.arange(L)[None, :]
    t_cum = (((r // C) == (c // C)) & (c <= r)).astype(np.float32)
    return jnp.asarray(t_cum, BF16), jnp.asarray(t_cum)


def _bcast_chunk_rows(x, row_in_chunk):
    L, C = HG_BLOCK, HG_CHUNK
    parts = [jnp.broadcast_to(x[c * C + row_in_chunk:c * C + row_in_chunk + 1, :], (C, x.shape[1]))
             for c in range(L // C)]
    return jnp.concatenate(parts, axis=0)


def _hgrn_head(q, lf, k, v, g, nw, tmat, maskf, st):
    L, C = HG_BLOCK, HG_CHUNK

    hi = lf.astype(BF16)
    r1 = lf - hi.astype(F32)
    mid = r1.astype(BF16)
    lo = (r1 - mid.astype(F32)).astype(BF16)
    gg = _dot(tmat, jnp.concatenate([hi, mid, lo], axis=1))
    yield
    g_cum = gg[:, :HG_DK] + gg[:, HG_DK:2 * HG_DK] + gg[:, 2 * HG_DK:]
    g_mid = _bcast_chunk_rows(g_cum, C // 2 - 1)
    g_last = _bcast_chunk_rows(g_cum, C - 1)

    qa = (q * jnp.exp(g_cum - g_mid)).astype(BF16)
    ka = (k * jnp.exp(g_mid - g_cum)).astype(BF16)
    qd = (q * jnp.exp(g_cum)).astype(BF16)
    kd = (k * jnp.exp(g_last - g_cum)).astype(BF16)
    dl = jnp.exp(g_last)

    a = _dot_nt(qa, ka)
    yield
    a = jnp.where(maskf > 0.0, a, 0.0).astype(BF16)
    o_intra = _dot(a, v)
    yield

    zeros = jnp.zeros((C, HG_DK), BF16)
    upds = []
    for p in range(L // (2 * C)):
        r0 = 2 * p * C
        rhs = jnp.concatenate(
            [jnp.concatenate([kd[r0:r0 + C], zeros], axis=1),
             jnp.concatenate([zeros, kd[r0 + C:r0 + 2 * C]], axis=1)], axis=0)
        u2 = _dot_tn(v[r0:r0 + 2 * C], rhs)
        upds += [u2[:, :HG_DK], u2[:, HG_DK:]]
    yield

    starts = []
    for c in range(L // C):
        starts.append(st.astype(BF16))
        st = st * dl[c * C:c * C + 1, :] + upds[c]

    outs = [o_intra[c * C:(c + 1) * C] + _dot_nt(qd[c * C:(c + 1) * C], starts[c])
            for c in range(L // C)]
    yield
    o = jnp.concatenate(outs, axis=0)
    ms = jnp.mean(o * o, -1, keepdims=True)
    y = o * lax.rsqrt(ms + RMS_EPS) * nw * g
    return y.astype(BF16), st


def _hgrn_kernel(q_ref, lf_ref, k_ref, v_ref, g_ref, nw_ref, tmat_ref, mask_ref, o_ref, st_ref):
    @pl.when(pl.program_id(2) == 0)
    def _():
        st_ref[...] = jnp.zeros_like(st_ref)

    lanes = [slice(h * HG_DK, (h + 1) * HG_DK) for h in range(HG_HEADS_PER_STEP)]
    heads = [_hgrn_head(q_ref[0, :, ln].astype(F32), lf_ref[0, :, ln], k_ref[0, :, ln].astype(F32),
                        v_ref[0, :, ln], g_ref[0, :, ln].astype(F32), nw_ref[...], tmat_ref[...],
                        mask_ref[...], st_ref[h]) for h, ln in enumerate(lanes)]
    pending = dict(enumerate(heads))
    while pending:
        for h, head in list(pending.items()):
            try:
                next(head)
            except StopIteration as done:
                y, st = done.value
                o_ref[0, :, lanes[h]] = y
                st_ref[h] = st
                del pending[h]


def _hgrn(qs, lf, ks, vs, gs, norm_w):
    bsz, s, _ = qs.shape
    L = HG_BLOCK
    hp = HG_HEADS_PER_STEP
    tmat, mask = _hgrn_consts()
    blk = pl.BlockSpec((1, L, hp * HG_DK), lambda b, h, n: (b, n, h))
    const = lambda shape: pl.BlockSpec(shape, lambda b, h, n: (0, 0))
    return pl.pallas_call(
        _hgrn_kernel,
        grid=(bsz, HG_HEADS // hp, s // L),
        in_specs=[blk, blk, blk, blk, blk, const((1, HG_DK)), const((L, L)), const((L, L))],
        out_specs=blk,
        out_shape=jax.ShapeDtypeStruct((bsz, s, HG_WIDTH), BF16),
        scratch_shapes=[pltpu.VMEM((hp, HG_DK, HG_DK), F32)],
        compiler_params=pltpu.CompilerParams(
            dimension_semantics=("arbitrary", "arbitrary", "arbitrary"),
            vmem_limit_bytes=VMEM_LIMIT),
        name="hgrn",
    )(qs, lf, ks, vs, gs, norm_w.reshape(1, HG_DK), tmat, mask)


ATT_BLOCK = 512
ATT_COLS = 256


def _attn_chain(k, q_t, v_t, acc_ref, m_ref, l_ref, cols, masked):
    s = _dot(k, q_t)
    yield
    if masked:
        key = lax.broadcasted_iota(jnp.int32, s.shape, 0)
        qry = lax.broadcasted_iota(jnp.int32, s.shape, 1) + cols.start
        s = jnp.where(key <= qry, s, jnp.finfo(F32).min)
    m_prev = m_ref[:, cols]
    m_new = jnp.maximum(m_prev, jnp.max(s, axis=0, keepdims=True))
    alpha = jnp.exp2(m_prev - m_new)
    p = jnp.exp2(s - m_new)
    l_ref[:, cols] = alpha * l_ref[:, cols] + jnp.sum(p, axis=0, keepdims=True)
    m_ref[:, cols] = m_new
    pv = _dot(v_t, p.astype(BF16))
    yield
    acc_ref[:, cols] = alpha * acc_ref[:, cols] + pv


def _run_interleaved(chains):
    pending = list(chains)
    while pending:
        for chain in list(pending):
            try:
                next(chain)
            except StopIteration:
                pending.remove(chain)


def _attn_kernel(q_ref, k_ref, v_ref, lq1_ref, lk1_ref, lq2_ref, lk2_ref, sw_ref, o_ref,
                 vt_ref, acc1_ref, acc2_ref, m1_ref, l1_ref, m2_ref, l2_ref):
    T = ATT_BLOCK
    C = ATT_COLS
    D = DA_HEAD_DIM
    i = pl.program_id(2)

    @pl.when(i == 0)
    def _():
        for j in range(vt_ref.shape[0]):
            vt_ref[j] = v_ref[0, j * T:(j + 1) * T, :].T

    q_t = q_ref[0].T
    for ref in (acc1_ref, acc2_ref, l1_ref, l2_ref):
        ref[...] = jnp.zeros_like(ref)
    m1_ref[...] = jnp.full_like(m1_ref, -jnp.inf)
    m2_ref[...] = jnp.full_like(m2_ref, -jnp.inf)

    def step(j, masked):
        start = pl.multiple_of(j * T, T)
        kb = k_ref[0, pl.ds(start, T), :]
        vt = vt_ref[j]
        chains = []
        for c in range(T // C):
            cols = slice(c * C, (c + 1) * C)
            nk = (c + 1) * C if masked else T
            chains.append(_attn_chain(kb[:nk, :D], q_t[:D, cols], vt[:, :nk],
                                      acc1_ref, m1_ref, l1_ref, cols, masked))
            chains.append(_attn_chain(kb[:nk, D:], q_t[D:, cols], vt[:, :nk],
                                      acc2_ref, m2_ref, l2_ref, cols, masked))
        _run_interleaved(chains)

    def body(j, carry):
        step(j, False)
        return carry

    lax.fori_loop(0, i, body, 0)
    step(i, True)

    lam = (jnp.exp(jnp.sum(lq1_ref[...] * lk1_ref[...], axis=-1, keepdims=True))
           - jnp.exp(jnp.sum(lq2_ref[...] * lk2_ref[...], axis=-1, keepdims=True))
           + LAMBDA_INIT)
    o_t = acc1_ref[...] / l1_ref[...] - lam * (acc2_ref[...] / l2_ref[...])
    o = o_t.T
    ms = jnp.mean(o * o, -1, keepdims=True)
    y = o * lax.rsqrt(ms + RMS_EPS) * sw_ref[...] * (1.0 - LAMBDA_INIT)
    o_ref[0] = y.astype(BF16)


def _attn(q, k, v, lq1, lk1, lq2, lk2, subln_w):
    bsz, s, _ = q.shape
    T = ATT_BLOCK
    hw = 2 * DA_HEAD_DIM
    qblk = pl.BlockSpec((1, T, hw), lambda b, h, i: (b, i, h))
    kvblk = pl.BlockSpec((1, s, hw), lambda b, h, i: (b, 0, h))
    vec = lambda n: pl.BlockSpec((1, n), lambda b, h, i: (0, 0))
    return pl.pallas_call(
        _attn_kernel,
        grid=(bsz, DA_HEADS, s // T),
        in_specs=[qblk, kvblk, kvblk, vec(DA_HEAD_DIM), vec(DA_HEAD_DIM), vec(DA_HEAD_DIM),
                  vec(DA_HEAD_DIM), vec(hw)],
        out_specs=qblk,
        out_shape=jax.ShapeDtypeStruct((bsz, s, DA_WIDTH), BF16),
        scratch_shapes=[pltpu.VMEM((s // T, hw, T), BF16),
                        pltpu.VMEM((hw, T), F32), pltpu.VMEM((hw, T), F32),
                        pltpu.VMEM((1, T), F32), pltpu.VMEM((1, T), F32),
                        pltpu.VMEM((1, T), F32), pltpu.VMEM((1, T), F32)],
        compiler_params=pltpu.CompilerParams(
            dimension_semantics=("arbitrary", "arbitrary", "arbitrary"),
            vmem_limit_bytes=VMEM_LIMIT),
        name="diff_attn",
    )(q, k, v, lq1.reshape(1, -1), lk1.reshape(1, -1), lq2.reshape(1, -1), lk2.reshape(1, -1),
      subln_w.reshape(1, -1))


def _outproj_kernel(yh_ref, ya_ref, w_ref, x_ref, mod_ref, g_ref, b_ref, h_ref, u_ref):
    m = mod_ref[0]
    y = _dot(yh_ref[0], w_ref[:HG_WIDTH, :]) + _dot(ya_ref[0], w_ref[HG_WIDTH:, :])
    r = DEEPNORM_ALPHA * x_ref[0] + m[2:3, :] * y
    h = _ln_plain(r) * g_ref[...] + b_ref[...]
    h_ref[0] = h
    u = _ln_plain(h) * (1.0 + m[4:5, :]) + m[3:4, :]
    u_ref[0] = u.astype(BF16)


def _outproj(y_hg, y_da, w_out, x, mod, ln_g, ln_b):
    bsz, s, d = x.shape
    tm = 256
    row = lambda b, i: (b, i, 0)
    const2 = lambda shape: pl.BlockSpec(shape, lambda b, i: (0, 0))
    return pl.pallas_call(
        _outproj_kernel,
        grid=(bsz, s // tm),
        in_specs=[pl.BlockSpec((1, tm, HG_WIDTH), row),
                  pl.BlockSpec((1, tm, DA_WIDTH), row),
                  const2(w_out.shape),
                  pl.BlockSpec((1, tm, d), row),
                  pl.BlockSpec((1, 6, d), lambda b, i: (b, 0, 0)),
                  const2((1, d)), const2((1, d))],
        out_specs=[pl.BlockSpec((1, tm, d), row), pl.BlockSpec((1, tm, d), row)],
        out_shape=[jax.ShapeDtypeStruct((bsz, s, d), F32), jax.ShapeDtypeStruct((bsz, s, d), BF16)],
        compiler_params=pltpu.CompilerParams(dimension_semantics=("arbitrary", "arbitrary"),
                                             vmem_limit_bytes=VMEM_LIMIT),
        name="outproj_ln1",
    )(y_hg, y_da, w_out, x, mod, ln_g.reshape(1, d), ln_b.reshape(1, d))


def _ffn_kernel(u_ref, wg_ref, wu_ref, wd_ref, h_ref, mod_ref, g_ref, b_ref, o_ref, acc_ref):
    j = pl.program_id(2)

    @pl.when(j == 0)
    def _():
        acc_ref[...] = jnp.zeros_like(acc_ref)

    u = u_ref[0]
    a = _dot(u, wg_ref[...])
    b = _dot(u, wu_ref[...])
    z = (a * jax.nn.sigmoid(a) * b).astype(BF16)
    acc_ref[...] += _dot(z, wd_ref[...])

    @pl.when(j == pl.num_programs(2) - 1)
    def _():
        m = mod_ref[0]
        r = DEEPNORM_ALPHA * h_ref[0] + m[5:6, :] * acc_ref[...]
        o_ref[0] = _ln_plain(r) * g_ref[...] + b_ref[...]


def _ffn(u, w_gate, w_up, w_down, h, mod, ln_g, ln_b):
    bsz, s, d = h.shape
    f = w_gate.shape[1]
    tm, tf = 512, 512
    row = lambda b, i, j: (b, i, 0)
    const3 = lambda shape: pl.BlockSpec(shape, lambda b, i, j: (0, 0))
    return pl.pallas_call(
        _ffn_kernel,
        grid=(bsz, s // tm, f // tf),
        in_specs=[pl.BlockSpec((1, tm, d), row),
                  pl.BlockSpec((d, tf), lambda b, i, j: (0, j)),
                  pl.BlockSpec((d, tf), lambda b, i, j: (0, j)),
                  pl.BlockSpec((tf, d), lambda b, i, j: (j, 0)),
                  pl.BlockSpec((1, tm, d), row),
                  pl.BlockSpec((1, 6, d), lambda b, i, j: (b, 0, 0)),
                  const3((1, d)), const3((1, d))],
        out_specs=pl.BlockSpec((1, tm, d), row),
        out_shape=jax.ShapeDtypeStruct((bsz, s, d), F32),
        scratch_shapes=[pltpu.VMEM((tm, d), F32)],
        compiler_params=pltpu.CompilerParams(
            dimension_semantics=("arbitrary", "arbitrary", "arbitrary"),
            vmem_limit_bytes=VMEM_LIMIT),
        name="ffn_ln2",
    )(u, w_gate, w_up, w_down, h, mod, ln_g.reshape(1, d), ln_b.reshape(1, d))


def kernel(x, c, positions, w_ada, b_ada, w_in, lb_logits, hg_norm_w, lam_q1, lam_k1, lam_q2, lam_k2,
           subln_w, w_out, ln1_g, ln1_b, w_gate, w_up, w_down, ln2_g, ln2_b):
    assert w_ada.shape[0] == DEPTH == 1
    bsz, s, d = x.shape

    mod = _adaln(c, w_ada[0], b_ada[0]).reshape(bsz, 6, d)
    cos, sin = _rope_tables(positions)

    qs, lf, ks, vs, gs, dq, dk, dv = _inproj(x, mod, w_in[0].astype(BF16), lb_logits, cos, sin)
    y_hg = _hgrn(qs, lf, ks, vs, gs, hg_norm_w[0])
    y_da = _attn(dq, dk, dv, lam_q1[0], lam_k1[0], lam_q2[0], lam_k2[0], subln_w[0])
    h1, u2 = _outproj(y_hg, y_da, w_out[0].astype(BF16), x, mod, ln1_g[0], ln1_b[0])
    return _ffn(u2, w_gate[0].astype(BF16), w_up[0].astype(BF16), w_down[0].astype(BF16),
                h1, mod, ln2_g[0], ln2_b[0])
```

```python
import math

import numpy as np
import jax
import jax.numpy as jnp
from jax import lax
from jax.experimental import pallas as pl
from jax.experimental.pallas import tpu as pltpu

D_MODEL = 2048
HG_WIDTH = 1024
DA_WIDTH = 1024
HG_HEADS = 8
HG_DK = 128
HG_CHUNK = 32
DA_HEADS = 4
DA_HEAD_DIM = 128
ROPE_THETA = 10000.0
D_FF = 5632
DEPTH = 1
DEEPNORM_ALPHA = (2.0 * DEPTH) ** 0.25
LN_EPS = 1e-5
RMS_EPS = 1e-6
LAMBDA_INIT = 0.8 - 0.6 * math.exp(-0.3 * 0)
SEG = 1024
N_SEG = 7

VMEM_LIMIT = 56 * 1024 * 1024

F32 = jnp.float32
BF16 = jnp.bfloat16


def _dot(a, b):
    return jnp.dot(a, b, preferred_element_type=F32)


def _dot_nt(a, b):
    return lax.dot_general(a, b, (((1,), (1,)), ((), ())), preferred_element_type=F32)


def _dot_tn(a, b):
    return lax.dot_general(a, b, (((0,), (0,)), ((), ())), preferred_element_type=F32)


def _ln_plain(x):
    mu = jnp.mean(x, -1, keepdims=True)
    xc = x - mu
    var = jnp.mean(xc * xc, -1, keepdims=True)
    return xc * lax.rsqrt(var + LN_EPS)


def _run_interleaved(chains):
    pending = list(chains)
    results = {}
    while pending:
        for chain in list(pending):
            try:
                next(chain)
            except StopIteration as done:
                results[id(chain)] = done.value
                pending.remove(chain)
    return [results[id(chain)] for chain in chains]


def _adaln_kernel(c_ref, w_ref, b_ref, o_ref):
    cond = c_ref[...]
    cond = cond * jax.nn.sigmoid(cond)
    o_ref[...] = _dot(cond, w_ref[...]) + b_ref[...]


def _adaln(c, w, b):
    bsz, d = c.shape
    n = w.shape[1]
    tn = 1024
    return pl.pallas_call(
        _adaln_kernel,
        grid=(n // tn,),
        in_specs=[pl.BlockSpec((bsz, d), lambda j: (0, 0)),
                  pl.BlockSpec((d, tn), lambda j: (0, j)),
                  pl.BlockSpec((1, tn), lambda j: (0, j))],
        out_specs=pl.BlockSpec((bsz, tn), lambda j: (0, j)),
        out_shape=jax.ShapeDtypeStruct((bsz, n), F32),
        compiler_params=pltpu.CompilerParams(dimension_semantics=("arbitrary",),
                                             vmem_limit_bytes=VMEM_LIMIT),
        name="adaln",
    )(c, w, b.reshape(1, n))


def _rope_kernel(pos_ref, inv_ref, sign_ref, cos_ref, sin_ref):
    ang = pos_ref[0] * inv_ref[...]
    cos_ref[0] = jnp.cos(ang)
    sin_ref[0] = jnp.sin(ang) * sign_ref[...]


def _rope_tables(positions):
    bsz, s = positions.shape
    ts = 512
    half = DA_HEAD_DIM // 2
    inv = 1.0 / (ROPE_THETA ** (jnp.arange(0, DA_HEAD_DIM, 2, dtype=F32) / DA_HEAD_DIM))
    inv2 = jnp.concatenate([inv, inv]).reshape(1, DA_HEAD_DIM)
    sign = jnp.concatenate([-jnp.ones((half,), F32), jnp.ones((half,), F32)]).reshape(1, DA_HEAD_DIM)
    pos = positions.astype(F32).reshape(bsz, s, 1)
    tab = jax.ShapeDtypeStruct((bsz, s, DA_HEAD_DIM), F32)
    return pl.pallas_call(
        _rope_kernel,
        grid=(bsz, s // ts),
        in_specs=[pl.BlockSpec((1, ts, 1), lambda b, i: (b, i, 0)),
                  pl.BlockSpec((1, DA_HEAD_DIM), lambda b, i: (0, 0)),
                  pl.BlockSpec((1, DA_HEAD_DIM), lambda b, i: (0, 0))],
        out_specs=[pl.BlockSpec((1, ts, DA_HEAD_DIM), lambda b, i: (b, i, 0)),
                   pl.BlockSpec((1, ts, DA_HEAD_DIM), lambda b, i: (b, i, 0))],
        out_shape=[tab, tab],
        compiler_params=pltpu.CompilerParams(dimension_semantics=("arbitrary", "arbitrary")),
        name="rope_tables",
    )(pos, inv2, sign)


def _rope_apply(acc, cos, sin, scale):
    outs = []
    for h in range(SEG // DA_HEAD_DIM):
        xh = acc[:, h * DA_HEAD_DIM:(h + 1) * DA_HEAD_DIM]
        rot = pltpu.roll(xh, DA_HEAD_DIM // 2, 1)
        y = xh * cos + rot * sin
        if scale is not None:
            y = y * scale
        outs.append(y.astype(BF16))
    return jnp.concatenate(outs, axis=-1)


def _inproj_kernel(x_ref, mod_ref, w_ref, lbl_ref, cos_ref, sin_ref,
                   oq_ref, olf_ref, ok_ref, ov_ref, og_ref, odq_ref, odk_ref, odv_ref, u_ref):
    j = pl.program_id(2)

    @pl.when(j == 0)
    def _():
        m = mod_ref[0]
        u = _ln_plain(x_ref[0]) * (1.0 + m[1:2, :]) + m[0:1, :]
        u_ref[...] = u.astype(BF16)

    acc = _dot(u_ref[...], w_ref[...])

    @pl.when(j == 0)
    def _():
        oq_ref[0] = (acc * jax.nn.sigmoid(acc) * (HG_DK ** -0.5)).astype(BF16)

    @pl.when(j == 1)
    def _():
        lbl = lbl_ref[...]
        e = jnp.exp(lbl - jnp.max(lbl, axis=0, keepdims=True))
        lb = e[0:1, :] / jnp.sum(e, axis=0, keepdims=True)
        f = lb + (1.0 - lb) * jax.nn.sigmoid(acc)
        olf_ref[0] = jnp.log(f)
        ok_ref[0] = ((1.0 - lb) * jax.nn.sigmoid(-acc)).astype(BF16)

    @pl.when(j == 2)
    def _():
        ov_ref[0] = acc.astype(BF16)

    @pl.when(j == 3)
    def _():
        og_ref[0] = jax.nn.sigmoid(acc).astype(BF16)

    @pl.when(j == 4)
    def _():
        odq_ref[0] = _rope_apply(acc, cos_ref[0], sin_ref[0], DA_HEAD_DIM ** -0.5 * math.log2(math.e))

    @pl.when(j == 5)
    def _():
        odk_ref[0] = _rope_apply(acc, cos_ref[0], sin_ref[0], None)

    @pl.when(j == 6)
    def _():
        odv_ref[0] = acc.astype(BF16)


def _inproj(x, mod, w_in, lb_logits, cos, sin):
    bsz, s, d = x.shape
    tm = 512
    row = lambda b, i, j: (b, i, 0)
    seg_bf = jax.ShapeDtypeStruct((bsz, s, SEG), BF16)
    seg_f32 = jax.ShapeDtypeStruct((bsz, s, SEG), F32)
    out_spec = pl.BlockSpec((1, tm, SEG), row)
    return pl.pallas_call(
        _inproj_kernel,
        grid=(bsz, s // tm, N_SEG),
        in_specs=[pl.BlockSpec((1, tm, d), row),
                  pl.BlockSpec((1, 6, d), lambda b, i, j: (b, 0, 0)),
                  pl.BlockSpec((d, SEG), lambda b, i, j: (0, j)),
                  pl.BlockSpec(lb_logits.shape, lambda b, i, j: (0, 0)),
                  pl.BlockSpec((1, tm, DA_HEAD_DIM), row),
                  pl.BlockSpec((1, tm, DA_HEAD_DIM), row)],
        out_specs=[out_spec] * 8,
        out_shape=[seg_bf, seg_f32, seg_bf, seg_bf, seg_bf, seg_bf, seg_bf, seg_bf],
        scratch_shapes=[pltpu.VMEM((tm, d), BF16)],
        compiler_params=pltpu.CompilerParams(
            dimension_semantics=("arbitrary", "arbitrary", "arbitrary"),
            vmem_limit_bytes=VMEM_LIMIT),
        name="inproj",
    )(x, mod, w_in, lb_logits, cos, sin)


HG_BLOCK = 256
HG_HEADS_PER_STEP = 8


def _hgrn_consts():
    L, C = HG_BLOCK, HG_CHUNK
    r = np.arange(L)[:, None]
    c = np.arange(L)[None, :]
    t_cum = (((r // C) == (c // C)) & (c <= r)).astype(np.float32)
    return jnp.asarray(t_cum, BF16), jnp.asarray(t_cum)


def _bcast_chunk_rows(x, row_in_chunk):
    L, C = HG_BLOCK, HG_CHUNK
    parts = [jnp.broadcast_to(x[c * C + row_in_chunk:c * C + row_in_chunk + 1, :], (C, x.shape[1]))
             for c in range(L // C)]
    return jnp.concatenate(parts, axis=0)


def _hgrn_head(q, lf, k, v, g, nw, tmat, maskf, st):
    L, C = HG_BLOCK, HG_CHUNK

    hi = lf.astype(BF16)
    r1 = lf - hi.astype(F32)
    mid = r1.astype(BF16)
    lo = (r1 - mid.astype(F32)).astype(BF16)
    gg = _dot(tmat, jnp.concatenate([hi, mid, lo], axis=1))
    yield
    g_cum = gg[:, :HG_DK] + gg[:, HG_DK:2 * HG_DK] + gg[:, 2 * HG_DK:]
    g_mid = _bcast_chunk_rows(g_cum, C // 2 - 1)
    g_last = _bcast_chunk_rows(g_cum, C - 1)

    qa = (q * jnp.exp(g_cum - g_mid)).astype(BF16)
    ka = (k * jnp.exp(g_mid - g_cum)).astype(BF16)
    qd = (q * jnp.exp(g_cum)).astype(BF16)
    kd = (k * jnp.exp(g_last - g_cum)).astype(BF16)
    dl = jnp.exp(g_last)

    a = _dot_nt(qa, ka)
    yield
    a = jnp.where(maskf > 0.0, a, 0.0).astype(BF16)
    o_intra = _dot(a, v)
    yield

    zeros = jnp.zeros((C, HG_DK), BF16)
    upds = []
    for p in range(L // (2 * C)):
        r0 = 2 * p * C
        rhs = jnp.concatenate(
            [jnp.concatenate([kd[r0:r0 + C], zeros], axis=1),
             jnp.concatenate([zeros, kd[r0 + C:r0 + 2 * C]], axis=1)], axis=0)
        u2 = _dot_tn(v[r0:r0 + 2 * C], rhs)
        upds += [u2[:, :HG_DK], u2[:, HG_DK:]]
    yield

    starts = []
    for c in range(L // C):
        starts.append(st.astype(BF16))
        st = st * dl[c * C:c * C + 1, :] + upds[c]

    outs = [o_intra[c * C:(c + 1) * C] + _dot_nt(qd[c * C:(c + 1) * C], starts[c])
            for c in range(L // C)]
    yield
    o = jnp.concatenate(outs, axis=0)
    ms = jnp.mean(o * o, -1, keepdims=True)
    y = o * lax.rsqrt(ms + RMS_EPS) * nw * g
    return y.astype(BF16), st


def _hgrn_kernel(q_ref, lf_ref, k_ref, v_ref, g_ref, nw_ref, tmat_ref, mask_ref, o_ref, st_ref):
    @pl.when(pl.program_id(2) == 0)
    def _():
        st_ref[...] = jnp.zeros_like(st_ref)

    lanes = [slice(h * HG_DK, (h + 1) * HG_DK) for h in range(HG_HEADS_PER_STEP)]
    heads = [_hgrn_head(q_ref[0, :, ln].astype(F32), lf_ref[0, :, ln], k_ref[0, :, ln].astype(F32),
                        v_ref[0, :, ln], g_ref[0, :, ln].astype(F32), nw_ref[...], tmat_ref[...],
                        mask_ref[...], st_ref[h]) for h, ln in enumerate(lanes)]
    for h, (y, st) in enumerate(_run_interleaved(heads)):
        o_ref[0, :, lanes[h]] = y
        st_ref[h] = st


def _hgrn(qs, lf, ks, vs, gs, norm_w):
    bsz, s, _ = qs.shape
    L = HG_BLOCK
    hp = HG_HEADS_PER_STEP
    tmat, mask = _hgrn_consts()
    blk = pl.BlockSpec((1, L, hp * HG_DK), lambda b, h, n: (b, n, h))
    const = lambda shape: pl.BlockSpec(shape, lambda b, h, n: (0, 0))
    return pl.pallas_call(
        _hgrn_kernel,
        grid=(bsz, HG_HEADS // hp, s // L),
        in_specs=[blk, blk, blk, blk, blk, const((1, HG_DK)), const((L, L)), const((L, L))],
        out_specs=blk,
        out_shape=jax.ShapeDtypeStruct((bsz, s, HG_WIDTH), BF16),
        scratch_shapes=[pltpu.VMEM((hp, HG_DK, HG_DK), F32)],
        compiler_params=pltpu.CompilerParams(
            dimension_semantics=("arbitrary", "arbitrary", "arbitrary"),
            vmem_limit_bytes=VMEM_LIMIT),
        name="hgrn",
    )(qs, lf, ks, vs, gs, norm_w.reshape(1, HG_DK), tmat, mask)


ATT_BLOCK = 512
ATT_COLS = 256
ATT_ONES = 16


def _attn_scores(k_ref, q_t, n, s_ref, slot):
    T, C, D = ATT_BLOCK, ATT_COLS, DA_HEAD_DIM
    kb = k_ref[0, pl.ds(pl.multiple_of(n * T, T), T), :]
    for t in range(2):
        for c in range(T // C):
            s_ref[slot, t * (T // C) + c] = _dot(kb[:, t * D:(t + 1) * D],
                                                 q_t[t * D:(t + 1) * D, c * C:(c + 1) * C])


def _attn_chain(s, v_t, acc_ref, m_ref, cols, masked):
    if masked:
        key = lax.broadcasted_iota(jnp.int32, s.shape, 0)
        qry = lax.broadcasted_iota(jnp.int32, s.shape, 1) + cols.start
        s = jnp.where(key <= qry, s, jnp.finfo(F32).min)
    m_prev = m_ref[:, cols]
    m_new = jnp.maximum(m_prev, jnp.max(s, axis=0, keepdims=True))
    alpha = jnp.exp2(m_prev - m_new)
    p = jnp.exp2(s - m_new)
    m_ref[:, cols] = m_new
    pv = _dot(v_t, p.astype(BF16))
    yield
    acc_ref[:, cols] = alpha * acc_ref[:, cols] + pv


def _attn_kernel(q_ref, k_ref, v_ref, lq1_ref, lk1_ref, lq2_ref, lk2_ref, sw_ref, o_ref,
                 vt_ref, s_ref, acc1_ref, acc2_ref, m1_ref, m2_ref):
    T = ATT_BLOCK
    C = ATT_COLS
    DV = 2 * DA_HEAD_DIM
    i = pl.program_id(2)

    @pl.when(i == 0)
    def _():
        for j in range(vt_ref.shape[0]):
            vt_ref[j, :DV, :] = v_ref[0, j * T:(j + 1) * T, :].T
            vt_ref[j, DV:, :] = jnp.ones((ATT_ONES, T), BF16)

    q_t = q_ref[0].T
    acc1_ref[...] = jnp.zeros_like(acc1_ref)
    acc2_ref[...] = jnp.zeros_like(acc2_ref)
    m1_ref[...] = jnp.full_like(m1_ref, -jnp.inf)
    m2_ref[...] = jnp.full_like(m2_ref, -jnp.inf)

    def step(n, slot, masked, prefetch):
        if prefetch:
            _attn_scores(k_ref, q_t, n + 1, s_ref, 1 - slot)
        vt = vt_ref[n]
        chains = []
        for t, (acc_ref, m_ref) in enumerate(((acc1_ref, m1_ref), (acc2_ref, m2_ref))):
            for c in range(T // C):
                cols = slice(c * C, (c + 1) * C)
                nk = (c + 1) * C if masked else T
                chains.append(_attn_chain(s_ref[slot, t * (T // C) + c, :nk, :], vt[:, :nk],
                                          acc_ref, m_ref, cols, masked))
        _run_interleaved(chains)

    _attn_scores(k_ref, q_t, 0, s_ref, 0)

    def pair(t, carry):
        step(2 * t, 0, False, True)
        step(2 * t + 1, 1, False, True)
        return carry

    lax.fori_loop(0, i // 2, pair, 0)

    @pl.when(i % 2 == 0)
    def _():
        step(i, 0, True, False)

    @pl.when(i % 2 == 1)
    def _():
        step(i - 1, 0, False, True)
        step(i, 1, True, False)

    lam = (jnp.exp(jnp.sum(lq1_ref[...] * lk1_ref[...], axis=-1, keepdims=True))
           - jnp.exp(jnp.sum(lq2_ref[...] * lk2_ref[...], axis=-1, keepdims=True))
           + LAMBDA_INIT)
    o_t = (acc1_ref[:DV, :] / acc1_ref[DV:DV + 1, :]
           - lam * (acc2_ref[:DV, :] / acc2_ref[DV:DV + 1, :]))
    o = o_t.T
    ms = jnp.mean(o * o, -1, keepdims=True)
    y = o * lax.rsqrt(ms + RMS_EPS) * sw_ref[...] * (1.0 - LAMBDA_INIT)
    o_ref[0] = y.astype(BF16)


def _attn(q, k, v, lq1, lk1, lq2, lk2, subln_w):
    bsz, s, _ = q.shape
    T = ATT_BLOCK
    hw = 2 * DA_HEAD_DIM
    qblk = pl.BlockSpec((1, T, hw), lambda b, h, i: (b, i, h))
    kvblk = pl.BlockSpec((1, s, hw), lambda b, h, i: (b, 0, h))
    vec = lambda n: pl.BlockSpec((1, n), lambda b, h, i: (0, 0))
    return pl.pallas_call(
        _attn_kernel,
        grid=(bsz, DA_HEADS, s // T),
        in_specs=[qblk, kvblk, kvblk, vec(DA_HEAD_DIM), vec(DA_HEAD_DIM), vec(DA_HEAD_DIM),
                  vec(DA_HEAD_DIM), vec(hw)],
        out_specs=qblk,
        out_shape=jax.ShapeDtypeStruct((bsz, s, DA_WIDTH), BF16),
        scratch_shapes=[pltpu.VMEM((s // T, hw + ATT_ONES, T), BF16),
                        pltpu.VMEM((2, 2 * (T // ATT_COLS), T, ATT_COLS), F32),
                        pltpu.VMEM((hw + ATT_ONES, T), F32), pltpu.VMEM((hw + ATT_ONES, T), F32),
                        pltpu.VMEM((1, T), F32), pltpu.VMEM((1, T), F32)],
        compiler_params=pltpu.CompilerParams(
            dimension_semantics=("arbitrary", "arbitrary", "arbitrary"),
            vmem_limit_bytes=VMEM_LIMIT),
        name="diff_attn",
    )(q, k, v, lq1.reshape(1, -1), lk1.reshape(1, -1), lq2.reshape(1, -1), lk2.reshape(1, -1),
      subln_w.reshape(1, -1))


OUT_ROWS = 256


def _outproj_chain(yh, ya, w_ref, x, m, g, b):
    y = _dot(yh, w_ref[:HG_WIDTH, :]) + _dot(ya, w_ref[HG_WIDTH:, :])
    yield
    r = DEEPNORM_ALPHA * x + m[2:3, :] * y
    h = _ln_plain(r) * g + b
    u = _ln_plain(h) * (1.0 + m[4:5, :]) + m[3:4, :]
    return h, u.astype(BF16)


def _outproj_kernel(yh_ref, ya_ref, w_ref, x_ref, mod_ref, g_ref, b_ref, h_ref, u_ref):
    m = mod_ref[0]
    rows = [slice(r, r + OUT_ROWS) for r in range(0, x_ref.shape[1], OUT_ROWS)]
    chains = [_outproj_chain(yh_ref[0, rs, :], ya_ref[0, rs, :], w_ref, x_ref[0, rs, :], m,
                             g_ref[...], b_ref[...]) for rs in rows]
    for rs, (h, u) in zip(rows, _run_interleaved(chains)):
        h_ref[0, rs, :] = h
        u_ref[0, rs, :] = u


def _outproj(y_hg, y_da, w_out, x, mod, ln_g, ln_b):
    bsz, s, d = x.shape
    tm = 512
    row = lambda b, i: (b, i, 0)
    const2 = lambda shape: pl.BlockSpec(shape, lambda b, i: (0, 0))
    return pl.pallas_call(
        _outproj_kernel,
        grid=(bsz, s // tm),
        in_specs=[pl.BlockSpec((1, tm, HG_WIDTH), row),
                  pl.BlockSpec((1, tm, DA_WIDTH), row),
                  const2(w_out.shape),
                  pl.BlockSpec((1, tm, d), row),
                  pl.BlockSpec((1, 6, d), lambda b, i: (b, 0, 0)),
                  const2((1, d)), const2((1, d))],
        out_specs=[pl.BlockSpec((1, tm, d), row), pl.BlockSpec((1, tm, d), row)],
        out_shape=[jax.ShapeDtypeStruct((bsz, s, d), F32), jax.ShapeDtypeStruct((bsz, s, d), BF16)],
        compiler_params=pltpu.CompilerParams(dimension_semantics=("arbitrary", "arbitrary"),
                                             vmem_limit_bytes=VMEM_LIMIT),
        name="outproj_ln1",
    )(y_hg, y_da, w_out, x, mod, ln_g.reshape(1, d), ln_b.reshape(1, d))


def _ffn_kernel(u_ref, wg_ref, wu_ref, wd_ref, h_ref, mod_ref, g_ref, b_ref, o_ref, acc_ref):
    j = pl.program_id(2)

    @pl.when(j == 0)
    def _():
        acc_ref[...] = jnp.zeros_like(acc_ref)

    u = u_ref[0]
    a = _dot(u, wg_ref[...])
    b = _dot(u, wu_ref[...])
    z = (a * jax.nn.sigmoid(a) * b).astype(BF16)
    acc_ref[...] += _dot(z, wd_ref[...])

    @pl.when(j == pl.num_programs(2) - 1)
    def _():
        m = mod_ref[0]
        r = DEEPNORM_ALPHA * h_ref[0] + m[5:6, :] * acc_ref[...]
        o_ref[0] = _ln_plain(r) * g_ref[...] + b_ref[...]


def _ffn(u, w_gate, w_up, w_down, h, mod, ln_g, ln_b):
    bsz, s, d = h.shape
    f = w_gate.shape[1]
    tm, tf = 512, 512
    row = lambda b, i, j: (b, i, 0)
    const3 = lambda shape: pl.BlockSpec(shape, lambda b, i, j: (0, 0))
    return pl.pallas_call(
        _ffn_kernel,
        grid=(bsz, s // tm, f // tf),
        in_specs=[pl.BlockSpec((1, tm, d), row),
                  pl.BlockSpec((d, tf), lambda b, i, j: (0, j)),
                  pl.BlockSpec((d, tf), lambda b, i, j: (0, j)),
                  pl.BlockSpec((tf, d), lambda b, i, j: (j, 0)),
                  pl.BlockSpec((1, tm, d), row),
                  pl.BlockSpec((1, 6, d), lambda b, i, j: (b, 0, 0)),
                  const3((1, d)), const3((1, d))],
        out_specs=pl.BlockSpec((1, tm, d), row),
        out_shape=jax.ShapeDtypeStruct((bsz, s, d), F32),
        scratch_shapes=[pltpu.VMEM((tm, d), F32)],
        compiler_params=pltpu.CompilerParams(
            dimension_semantics=("arbitrary", "arbitrary", "arbitrary"),
            vmem_limit_bytes=VMEM_LIMIT),
        name="ffn_ln2",
    )(u, w_gate, w_up, w_down, h, mod, ln_g.reshape(1, d), ln_b.reshape(1, d))


def kernel(x, c, positions, w_ada, b_ada, w_in, lb_logits, hg_norm_w, lam_q1, lam_k1, lam_q2, lam_k2,
           subln_w, w_out, ln1_g, ln1_b, w_gate, w_up, w_down, ln2_g, ln2_b):
    assert w_ada.shape[0] == DEPTH == 1
    bsz, s, d = x.shape

    mod = _adaln(c, w_ada[0], b_ada[0]).reshape(bsz, 6, d)
    cos, sin = _rope_tables(positions)

    qs, lf, ks, vs, gs, dq, dk, dv = _inproj(x, mod, w_in[0].astype(BF16), lb_logits, cos, sin)
    y_hg = _hgrn(qs, lf, ks, vs, gs, hg_norm_w[0])
    y_da = _attn(dq, dk, dv, lam_q1[0], lam_k1[0], lam_q2[0], lam_k2[0], subln_w[0])
    h1, u2 = _outproj(y_hg, y_da, w_out[0].astype(BF16), x, mod, ln1_g[0], ln1_b[0])
    return _ffn(u2, w_gate[0].astype(BF16), w_up[0].astype(BF16), w_down[0].astype(BF16),
                h1, mod, ln2_g[0], ln2_b[0])
```

```python
import math

import numpy as np
import jax
import jax.numpy as jnp
from jax import lax
from jax.experimental import pallas as pl
from jax.experimental.pallas import tpu as pltpu

D_MODEL = 2048
HG_WIDTH = 1024
DA_WIDTH = 1024
HG_HEADS = 8
HG_DK = 128
HG_CHUNK = 32
DA_HEADS = 4
DA_HEAD_DIM = 128
ROPE_THETA = 10000.0
D_FF = 5632
DEPTH = 1
DEEPNORM_ALPHA = (2.0 * DEPTH) ** 0.25
LN_EPS = 1e-5
RMS_EPS = 1e-6
LAMBDA_INIT = 0.8 - 0.6 * math.exp(-0.3 * 0)
SEG = 1024
N_SEG = 7

VMEM_LIMIT = 56 * 1024 * 1024
FFN_VMEM_LIMIT = 62 * 1024 * 1024

F32 = jnp.float32
BF16 = jnp.bfloat16


def _dot(a, b):
    return jnp.dot(a, b, preferred_element_type=F32)


def _dot_nt(a, b):
    return lax.dot_general(a, b, (((1,), (1,)), ((), ())), preferred_element_type=F32)


def _dot_tn(a, b):
    return lax.dot_general(a, b, (((0,), (0,)), ((), ())), preferred_element_type=F32)


def _ln_plain(x):
    mu = jnp.mean(x, -1, keepdims=True)
    xc = x - mu
    var = jnp.mean(xc * xc, -1, keepdims=True)
    return xc * lax.rsqrt(var + LN_EPS)


def _run_interleaved(chains):
    pending = list(chains)
    results = {}
    while pending:
        for chain in list(pending):
            try:
                next(chain)
            except StopIteration as done:
                results[id(chain)] = done.value
                pending.remove(chain)
    return [results[id(chain)] for chain in chains]


def _adaln_kernel(c_ref, w_ref, b_ref, o_ref):
    cond = c_ref[...]
    cond = cond * jax.nn.sigmoid(cond)
    o_ref[...] = _dot(cond, w_ref[...]) + b_ref[...]


def _adaln(c, w, b):
    bsz, d = c.shape
    n = w.shape[1]
    tn = 1024
    return pl.pallas_call(
        _adaln_kernel,
        grid=(n // tn,),
        in_specs=[pl.BlockSpec((bsz, d), lambda j: (0, 0)),
                  pl.BlockSpec((d, tn), lambda j: (0, j)),
                  pl.BlockSpec((1, tn), lambda j: (0, j))],
        out_specs=pl.BlockSpec((bsz, tn), lambda j: (0, j)),
        out_shape=jax.ShapeDtypeStruct((bsz, n), F32),
        compiler_params=pltpu.CompilerParams(dimension_semantics=("arbitrary",),
                                             vmem_limit_bytes=VMEM_LIMIT),
        name="adaln",
    )(c, w, b.reshape(1, n))


def _rope_kernel(pos_ref, inv_ref, sign_ref, cos_ref, sin_ref):
    ang = pos_ref[0] * inv_ref[...]
    cos_ref[0] = jnp.cos(ang)
    sin_ref[0] = jnp.sin(ang) * sign_ref[...]


def _rope_tables(positions):
    bsz, s = positions.shape
    ts = 512
    half = DA_HEAD_DIM // 2
    inv = 1.0 / (ROPE_THETA ** (jnp.arange(0, DA_HEAD_DIM, 2, dtype=F32) / DA_HEAD_DIM))
    inv2 = jnp.concatenate([inv, inv]).reshape(1, DA_HEAD_DIM)
    sign = jnp.concatenate([-jnp.ones((half,), F32), jnp.ones((half,), F32)]).reshape(1, DA_HEAD_DIM)
    pos = positions.astype(F32).reshape(bsz, s, 1)
    tab = jax.ShapeDtypeStruct((bsz, s, DA_HEAD_DIM), F32)
    return pl.pallas_call(
        _rope_kernel,
        grid=(bsz, s // ts),
        in_specs=[pl.BlockSpec((1, ts, 1), lambda b, i: (b, i, 0)),
                  pl.BlockSpec((1, DA_HEAD_DIM), lambda b, i: (0, 0)),
                  pl.BlockSpec((1, DA_HEAD_DIM), lambda b, i: (0, 0))],
        out_specs=[pl.BlockSpec((1, ts, DA_HEAD_DIM), lambda b, i: (b, i, 0)),
                   pl.BlockSpec((1, ts, DA_HEAD_DIM), lambda b, i: (b, i, 0))],
        out_shape=[tab, tab],
        compiler_params=pltpu.CompilerParams(dimension_semantics=("arbitrary", "arbitrary")),
        name="rope_tables",
    )(pos, inv2, sign)


def _rope_apply(acc, cos, sin, scale):
    outs = []
    for h in range(SEG // DA_HEAD_DIM):
        xh = acc[:, h * DA_HEAD_DIM:(h + 1) * DA_HEAD_DIM]
        rot = pltpu.roll(xh, DA_HEAD_DIM // 2, 1)
        y = xh * cos + rot * sin
        if scale is not None:
            y = y * scale
        outs.append(y.astype(BF16))
    return jnp.concatenate(outs, axis=-1)


def _sigmoid_tanh(x):
    return 0.5 * jnp.tanh(0.5 * x) + 0.5


def _inproj_kernel(x_ref, mod_ref, w_ref, lbl_ref, cos_ref, sin_ref,
                   oq_ref, olf_ref, ok_ref, ov_ref, og_ref, odq_ref, odk_ref, odv_ref, u_ref):
    j = pl.program_id(2)

    @pl.when(j == 0)
    def _():
        m = mod_ref[0]
        u = _ln_plain(x_ref[0]) * (1.0 + m[1:2, :]) + m[0:1, :]
        u_ref[...] = u.astype(BF16)

    acc = _dot(u_ref[...], w_ref[...])

    @pl.when(j == 0)
    def _():
        oq_ref[0] = (acc * _sigmoid_tanh(acc) * (HG_DK ** -0.5)).astype(BF16)

    @pl.when(j == 1)
    def _():
        lbl = lbl_ref[...]
        e = jnp.exp(lbl - jnp.max(lbl, axis=0, keepdims=True))
        lb = e[0:1, :] / jnp.sum(e, axis=0, keepdims=True)
        t = jnp.exp(-jnp.abs(acc))
        big = 1.0 / (1.0 + t)
        small = t * big
        pos = acc >= 0.0
        f = lb + (1.0 - lb) * jnp.where(pos, big, small)
        olf_ref[0] = jnp.log(f)
        ok_ref[0] = ((1.0 - lb) * jnp.where(pos, small, big)).astype(BF16)

    @pl.when(j == 2)
    def _():
        ov_ref[0] = acc.astype(BF16)

    @pl.when(j == 3)
    def _():
        og_ref[0] = _sigmoid_tanh(acc).astype(BF16)

    @pl.when(j == 4)
    def _():
        odq_ref[0] = _rope_apply(acc, cos_ref[0], sin_ref[0], DA_HEAD_DIM ** -0.5 * math.log2(math.e))

    @pl.when(j == 5)
    def _():
        odk_ref[0] = _rope_apply(acc, cos_ref[0], sin_ref[0], None)

    @pl.when(j == 6)
    def _():
        odv_ref[0] = acc.astype(BF16)


def _inproj(x, mod, w_in, lb_logits, cos, sin):
    bsz, s, d = x.shape
    tm = 512
    row = lambda b, i, j: (b, i, 0)
    seg_bf = jax.ShapeDtypeStruct((bsz, s, SEG), BF16)
    seg_f32 = jax.ShapeDtypeStruct((bsz, s, SEG), F32)
    out_spec = pl.BlockSpec((1, tm, SEG), row)
    return pl.pallas_call(
        _inproj_kernel,
        grid=(bsz, s // tm, N_SEG),
        in_specs=[pl.BlockSpec((1, tm, d), row),
                  pl.BlockSpec((1, 6, d), lambda b, i, j: (b, 0, 0)),
                  pl.BlockSpec((d, SEG), lambda b, i, j: (0, j)),
                  pl.BlockSpec(lb_logits.shape, lambda b, i, j: (0, 0)),
                  pl.BlockSpec((1, tm, DA_HEAD_DIM), row),
                  pl.BlockSpec((1, tm, DA_HEAD_DIM), row)],
        out_specs=[out_spec] * 8,
        out_shape=[seg_bf, seg_f32, seg_bf, seg_bf, seg_bf, seg_bf, seg_bf, seg_bf],
        scratch_shapes=[pltpu.VMEM((tm, d), BF16)],
        compiler_params=pltpu.CompilerParams(
            dimension_semantics=("arbitrary", "arbitrary", "arbitrary"),
            vmem_limit_bytes=VMEM_LIMIT),
        name="inproj",
    )(x, mod, w_in, lb_logits, cos, sin)


HG_BLOCK = 256
HG_HEADS_PER_STEP = 8


def _hgrn_consts():
    L, C = HG_BLOCK, HG_CHUNK
    r = np.arange(L)[:, None]
    c = np.arange(L)[None, :]
    t_cum = (((r // C) == (c // C)) & (c <= r)).astype(np.float32)
    return jnp.asarray(t_cum, BF16), jnp.asarray(t_cum)


def _bcast_chunk_rows(x, row_in_chunk):
    L, C = HG_BLOCK, HG_CHUNK
    parts = [jnp.broadcast_to(x[c * C + row_in_chunk:c * C + row_in_chunk + 1, :], (C, x.shape[1]))
             for c in range(L // C)]
    return jnp.concatenate(parts, axis=0)


def _hgrn_head(q, lf, k, v, g, nw, tmat, maskf, st):
    L, C = HG_BLOCK, HG_CHUNK

    hi = lf.astype(BF16)
    r1 = lf - hi.astype(F32)
    mid = r1.astype(BF16)
    lo = (r1 - mid.astype(F32)).astype(BF16)
    gg = _dot(tmat, jnp.concatenate([hi, mid, lo], axis=1))
    yield
    g_cum = gg[:, :HG_DK] + gg[:, HG_DK:2 * HG_DK] + gg[:, 2 * HG_DK:]
    g_mid = _bcast_chunk_rows(g_cum, C // 2 - 1)
    g_last = _bcast_chunk_rows(g_cum, C - 1)

    qa = (q * jnp.exp(g_cum - g_mid)).astype(BF16)
    ka = (k * jnp.exp(g_mid - g_cum)).astype(BF16)
    qd = (q * jnp.exp(g_cum)).astype(BF16)
    kd = (k * jnp.exp(g_last - g_cum)).astype(BF16)
    dl = jnp.exp(g_last)

    a = _dot_nt(qa, ka)
    yield
    a = jnp.where(maskf > 0.0, a, 0.0).astype(BF16)
    o_intra = _dot(a, v)
    yield

    zeros = jnp.zeros((C, HG_DK), BF16)
    upds = []
    for p in range(L // (2 * C)):
        r0 = 2 * p * C
        rhs = jnp.concatenate(
            [jnp.concatenate([kd[r0:r0 + C], zeros], axis=1),
             jnp.concatenate([zeros, kd[r0 + C:r0 + 2 * C]], axis=1)], axis=0)
        u2 = _dot_tn(v[r0:r0 + 2 * C], rhs)
        upds += [u2[:, :HG_DK], u2[:, HG_DK:]]
    yield

    starts = []
    for c in range(L // C):
        starts.append(st.astype(BF16))
        st = st * dl[c * C:c * C + 1, :] + upds[c]

    outs = [o_intra[c * C:(c + 1) * C] + _dot_nt(qd[c * C:(c + 1) * C], starts[c])
            for c in range(L // C)]
    yield
    o = jnp.concatenate(outs, axis=0)
    ms = jnp.mean(o * o, -1, keepdims=True)
    y = o * lax.rsqrt(ms + RMS_EPS) * nw * g
    return y.astype(BF16), st


def _hgrn_kernel(q_ref, lf_ref, k_ref, v_ref, g_ref, nw_ref, tmat_ref, mask_ref, o_ref, st_ref):
    @pl.when(pl.program_id(2) == 0)
    def _():
        st_ref[...] = jnp.zeros_like(st_ref)

    lanes = [slice(h * HG_DK, (h + 1) * HG_DK) for h in range(HG_HEADS_PER_STEP)]
    heads = [_hgrn_head(q_ref[0, :, ln].astype(F32), lf_ref[0, :, ln], k_ref[0, :, ln].astype(F32),
                        v_ref[0, :, ln], g_ref[0, :, ln].astype(F32), nw_ref[...], tmat_ref[...],
                        mask_ref[...], st_ref[h]) for h, ln in enumerate(lanes)]
    for h, (y, st) in enumerate(_run_interleaved(heads)):
        o_ref[0, :, lanes[h]] = y
        st_ref[h] = st


def _hgrn(qs, lf, ks, vs, gs, norm_w):
    bsz, s, _ = qs.shape
    L = HG_BLOCK
    hp = HG_HEADS_PER_STEP
    tmat, mask = _hgrn_consts()
    blk = pl.BlockSpec((1, L, hp * HG_DK), lambda b, h, n: (b, n, h))
    const = lambda shape: pl.BlockSpec(shape, lambda b, h, n: (0, 0))
    return pl.pallas_call(
        _hgrn_kernel,
        grid=(bsz, HG_HEADS // hp, s // L),
        in_specs=[blk, blk, blk, blk, blk, const((1, HG_DK)), const((L, L)), const((L, L))],
        out_specs=blk,
        out_shape=jax.ShapeDtypeStruct((bsz, s, HG_WIDTH), BF16),
        scratch_shapes=[pltpu.VMEM((hp, HG_DK, HG_DK), F32)],
        compiler_params=pltpu.CompilerParams(
            dimension_semantics=("arbitrary", "arbitrary", "arbitrary"),
            vmem_limit_bytes=VMEM_LIMIT),
        name="hgrn",
    )(qs, lf, ks, vs, gs, norm_w.reshape(1, HG_DK), tmat, mask)


ATT_BLOCK = 512
ATT_COLS = 256
ATT_ONES = 16


def _attn_scores(k_ref, q_t, n, s_ref, slot):
    T, C, D = ATT_BLOCK, ATT_COLS, DA_HEAD_DIM
    kb = k_ref[0, pl.ds(pl.multiple_of(n * T, T), T), :]
    for t in range(2):
        for c in range(T // C):
            s_ref[slot, t * (T // C) + c] = _dot(kb[:, t * D:(t + 1) * D],
                                                 q_t[t * D:(t + 1) * D, c * C:(c + 1) * C])


def _attn_chain(s, v_t, acc_ref, m_ref, cols, masked):
    if masked:
        key = lax.broadcasted_iota(jnp.int32, s.shape, 0)
        qry = lax.broadcasted_iota(jnp.int32, s.shape, 1) + cols.start
        s = jnp.where(key <= qry, s, jnp.finfo(F32).min)
    m_prev = m_ref[:, cols]
    m_new = jnp.maximum(m_prev, jnp.max(s, axis=0, keepdims=True))
    alpha = jnp.exp2(m_prev - m_new)
    p = jnp.exp2(s - m_new)
    m_ref[:, cols] = m_new
    pv = _dot(v_t, p.astype(BF16))
    yield
    acc_ref[:, cols] = alpha * acc_ref[:, cols] + pv


def _attn_kernel(q_ref, k_ref, v_ref, lq1_ref, lk1_ref, lq2_ref, lk2_ref, sw_ref, o_ref,
                 vt_ref, s_ref, acc1_ref, acc2_ref, m1_ref, m2_ref):
    T = ATT_BLOCK
    C = ATT_COLS
    DV = 2 * DA_HEAD_DIM
    i = pl.program_id(2)

    @pl.when(i == 0)
    def _():
        for j in range(vt_ref.shape[0]):
            vt_ref[j, :DV, :] = v_ref[0, j * T:(j + 1) * T, :].T
            vt_ref[j, DV:, :] = jnp.ones((ATT_ONES, T), BF16)

    q_t = q_ref[0].T
    acc1_ref[...] = jnp.zeros_like(acc1_ref)
    acc2_ref[...] = jnp.zeros_like(acc2_ref)
    m1_ref[...] = jnp.full_like(m1_ref, -jnp.inf)
    m2_ref[...] = jnp.full_like(m2_ref, -jnp.inf)

    def step(n, slot, masked, prefetch):
        if prefetch:
            _attn_scores(k_ref, q_t, n + 1, s_ref, 1 - slot)
        vt = vt_ref[n]
        chains = []
        for t, (acc_ref, m_ref) in enumerate(((acc1_ref, m1_ref), (acc2_ref, m2_ref))):
            for c in range(T // C):
                cols = slice(c * C, (c + 1) * C)
                nk = (c + 1) * C if masked else T
                chains.append(_attn_chain(s_ref[slot, t * (T // C) + c, :nk, :], vt[:, :nk],
                                          acc_ref, m_ref, cols, masked))
        _run_interleaved(chains)

    _attn_scores(k_ref, q_t, 0, s_ref, 0)

    def pair(t, carry):
        step(2 * t, 0, False, True)
        step(2 * t + 1, 1, False, True)
        return carry

    lax.fori_loop(0, i // 2, pair, 0)

    @pl.when(i % 2 == 0)
    def _():
        step(i, 0, True, False)

    @pl.when(i % 2 == 1)
    def _():
        step(i - 1, 0, False, True)
        step(i, 1, True, False)

    lam = (jnp.exp(jnp.sum(lq1_ref[...] * lk1_ref[...], axis=-1, keepdims=True))
           - jnp.exp(jnp.sum(lq2_ref[...] * lk2_ref[...], axis=-1, keepdims=True))
           + LAMBDA_INIT)
    inv1 = 1.0 / acc1_ref[DV:DV + 1, :]
    inv2 = lam / acc2_ref[DV:DV + 1, :]
    o_t = acc1_ref[:DV, :] * inv1 - acc2_ref[:DV, :] * inv2
    o = o_t.T
    ms = jnp.mean(o * o, -1, keepdims=True)
    y = o * lax.rsqrt(ms + RMS_EPS) * sw_ref[...] * (1.0 - LAMBDA_INIT)
    o_ref[0] = y.astype(BF16)


def _attn(q, k, v, lq1, lk1, lq2, lk2, subln_w):
    bsz, s, _ = q.shape
    T = ATT_BLOCK
    hw = 2 * DA_HEAD_DIM
    qblk = pl.BlockSpec((1, T, hw), lambda b, h, i: (b, i, h))
    kvblk = pl.BlockSpec((1, s, hw), lambda b, h, i: (b, 0, h))
    vec = lambda n: pl.BlockSpec((1, n), lambda b, h, i: (0, 0))
    return pl.pallas_call(
        _attn_kernel,
        grid=(bsz, DA_HEADS, s // T),
        in_specs=[qblk, kvblk, kvblk, vec(DA_HEAD_DIM), vec(DA_HEAD_DIM), vec(DA_HEAD_DIM),
                  vec(DA_HEAD_DIM), vec(hw)],
        out_specs=qblk,
        out_shape=jax.ShapeDtypeStruct((bsz, s, DA_WIDTH), BF16),
        scratch_shapes=[pltpu.VMEM((s // T, hw + ATT_ONES, T), BF16),
                        pltpu.VMEM((2, 2 * (T // ATT_COLS), T, ATT_COLS), F32),
                        pltpu.VMEM((hw + ATT_ONES, T), F32), pltpu.VMEM((hw + ATT_ONES, T), F32),
                        pltpu.VMEM((1, T), F32), pltpu.VMEM((1, T), F32)],
        compiler_params=pltpu.CompilerParams(
            dimension_semantics=("arbitrary", "arbitrary", "arbitrary"),
            vmem_limit_bytes=VMEM_LIMIT),
        name="diff_attn",
    )(q, k, v, lq1.reshape(1, -1), lk1.reshape(1, -1), lq2.reshape(1, -1), lk2.reshape(1, -1),
      subln_w.reshape(1, -1))


OUT_ROWS = 128


def _outproj_chain(yh, ya, w_ref, x, m, g, b):
    y = _dot(yh, w_ref[:HG_WIDTH, :]) + _dot(ya, w_ref[HG_WIDTH:, :])
    yield
    r = DEEPNORM_ALPHA * x + m[2:3, :] * y
    h = _ln_plain(r) * g + b
    u = _ln_plain(h) * (1.0 + m[4:5, :]) + m[3:4, :]
    return h, u.astype(BF16)


def _outproj_kernel(yh_ref, ya_ref, w_ref, x_ref, mod_ref, g_ref, b_ref, h_ref, u_ref):
    m = mod_ref[0]
    rows = [slice(r, r + OUT_ROWS) for r in range(0, x_ref.shape[1], OUT_ROWS)]
    chains = [_outproj_chain(yh_ref[0, rs, :], ya_ref[0, rs, :], w_ref, x_ref[0, rs, :], m,
                             g_ref[...], b_ref[...]) for rs in rows]
    for rs, (h, u) in zip(rows, _run_interleaved(chains)):
        h_ref[0, rs, :] = h
        u_ref[0, rs, :] = u


def _outproj(y_hg, y_da, w_out, x, mod, ln_g, ln_b):
    bsz, s, d = x.shape
    tm = 512
    row = lambda b, i: (b, i, 0)
    const2 = lambda shape: pl.BlockSpec(shape, lambda b, i: (0, 0))
    return pl.pallas_call(
        _outproj_kernel,
        grid=(bsz, s // tm),
        in_specs=[pl.BlockSpec((1, tm, HG_WIDTH), row),
                  pl.BlockSpec((1, tm, DA_WIDTH), row),
                  const2(w_out.shape),
                  pl.BlockSpec((1, tm, d), row),
                  pl.BlockSpec((1, 6, d), lambda b, i: (b, 0, 0)),
                  const2((1, d)), const2((1, d))],
        out_specs=[pl.BlockSpec((1, tm, d), row), pl.BlockSpec((1, tm, d), row)],
        out_shape=[jax.ShapeDtypeStruct((bsz, s, d), F32), jax.ShapeDtypeStruct((bsz, s, d), BF16)],
        compiler_params=pltpu.CompilerParams(dimension_semantics=("arbitrary", "arbitrary"),
                                             vmem_limit_bytes=VMEM_LIMIT),
        name="outproj_ln1",
    )(y_hg, y_da, w_out, x, mod, ln_g.reshape(1, d), ln_b.reshape(1, d))


FFN_ROWS = 1024
FFN_COLS = 256


def _ffn_kernel(u_ref, wgu_ref, wd_ref, h_ref, mod_ref, g_ref, b_ref, o_ref):
    j = pl.program_id(2)

    @pl.when(j == 0)
    def _():
        o_ref[...] = jnp.zeros_like(o_ref)

    ab = _dot(u_ref[0], wgu_ref[0])
    a = ab[:, :FFN_COLS]
    b = ab[:, FFN_COLS:]
    z = (a * jax.nn.sigmoid(a) * b).astype(BF16)
    o_ref[0] += _dot(z, wd_ref[...])

    @pl.when(j == pl.num_programs(2) - 1)
    def _():
        m = mod_ref[0]
        r = DEEPNORM_ALPHA * h_ref[0] + m[5:6, :] * o_ref[0]
        o_ref[0] = _ln_plain(r) * g_ref[...] + b_ref[...]


def _ffn(u, w_gate, w_up, w_down, h, mod, ln_g, ln_b):
    bsz, s, d = h.shape
    f = w_gate.shape[1]
    tm, tf = FFN_ROWS, FFN_COLS
    wgu = jnp.concatenate([w_gate.astype(BF16).reshape(d, f // tf, tf),
                           w_up.astype(BF16).reshape(d, f // tf, tf)], axis=2).transpose(1, 0, 2)
    row = lambda b, i, j: (b, i, 0)
    const3 = lambda shape: pl.BlockSpec(shape, lambda b, i, j: (0, 0))
    return pl.pallas_call(
        _ffn_kernel,
        grid=(bsz, s // tm, f // tf),
        in_specs=[pl.BlockSpec((1, tm, d), row),
                  pl.BlockSpec((1, d, 2 * tf), lambda b, i, j: (j, 0, 0)),
                  pl.BlockSpec((tf, d), lambda b, i, j: (j, 0)),
                  pl.BlockSpec((1, tm, d), row),
                  pl.BlockSpec((1, 6, d), lambda b, i, j: (b, 0, 0)),
                  const3((1, d)), const3((1, d))],
        out_specs=pl.BlockSpec((1, tm, d), row),
        out_shape=jax.ShapeDtypeStruct((bsz, s, d), F32),
        compiler_params=pltpu.CompilerParams(
            dimension_semantics=("arbitrary", "arbitrary", "arbitrary"),
            vmem_limit_bytes=FFN_VMEM_LIMIT),
        name="ffn_ln2",
    )(u, wgu, w_down.astype(BF16), h, mod, ln_g.reshape(1, d), ln_b.reshape(1, d))


def kernel(x, c, positions, w_ada, b_ada, w_in, lb_logits, hg_norm_w, lam_q1, lam_k1, lam_q2, lam_k2,
           subln_w, w_out, ln1_g, ln1_b, w_gate, w_up, w_down, ln2_g, ln2_b):
    assert w_ada.shape[0] == DEPTH == 1
    bsz, s, d = x.shape

    mod = _adaln(c, w_ada[0], b_ada[0]).reshape(bsz, 6, d)
    cos, sin = _rope_tables(positions)

    qs, lf, ks, vs, gs, dq, dk, dv = _inproj(x, mod, w_in[0].astype(BF16), lb_logits, cos, sin)
    y_hg = _hgrn(qs, lf, ks, vs, gs, hg_norm_w[0])
    y_da = _attn(dq, dk, dv, lam_q1[0], lam_k1[0], lam_q2[0], lam_k2[0], subln_w[0])
    h1, u2 = _outproj(y_hg, y_da, w_out[0].astype(BF16), x, mod, ln1_g[0], ln1_b[0])
    return _ffn(u2, w_gate[0], w_up[0], w_down[0], h1, mod, ln2_g[0], ln2_b[0])
```

```python
import math

import numpy as np
import jax
import jax.numpy as jnp
from jax import lax
from jax.experimental import pallas as pl
from jax.experimental.pallas import tpu as pltpu

D_MODEL = 2048
HG_WIDTH = 1024
DA_WIDTH = 1024
HG_HEADS = 8
HG_DK = 128
HG_CHUNK = 32
DA_HEADS = 4
DA_HEAD_DIM = 128
ROPE_THETA = 10000.0
D_FF = 5632
DEPTH = 1
DEEPNORM_ALPHA = (2.0 * DEPTH) ** 0.25
LN_EPS = 1e-5
RMS_EPS = 1e-6
LAMBDA_INIT = 0.8 - 0.6 * math.exp(-0.3 * 0)
SEG = 1024
N_SEG = 7

VMEM_LIMIT = 56 * 1024 * 1024

F32 = jnp.float32
BF16 = jnp.bfloat16


def _dot(a, b):
    return jnp.dot(a, b, preferred_element_type=F32)


def _dot_nt(a, b):
    return lax.dot_general(a, b, (((1,), (1,)), ((), ())), preferred_element_type=F32)


def _dot_tn(a, b):
    return lax.dot_general(a, b, (((0,), (0,)), ((), ())), preferred_element_type=F32)


def _ln_plain(x):
    mu = jnp.mean(x, -1, keepdims=True)
    xc = x - mu
    var = jnp.mean(xc * xc, -1, keepdims=True)
    return xc * lax.rsqrt(var + LN_EPS)


def _run_interleaved(chains):
    pending = list(chains)
    results = {}
    while pending:
        for chain in list(pending):
            try:
                next(chain)
            except StopIteration as done:
                results[id(chain)] = done.value
                pending.remove(chain)
    return [results[id(chain)] for chain in chains]


def _adaln_kernel(c_ref, w_ref, b_ref, o_ref):
    cond = c_ref[...]
    cond = cond * jax.nn.sigmoid(cond)
    o_ref[...] = _dot(cond, w_ref[...]) + b_ref[...]


def _adaln(c, w, b):
    bsz, d = c.shape
    n = w.shape[1]
    tn = 1024
    return pl.pallas_call(
        _adaln_kernel,
        grid=(n // tn,),
        in_specs=[pl.BlockSpec((bsz, d), lambda j: (0, 0)),
                  pl.BlockSpec((d, tn), lambda j: (0, j)),
                  pl.BlockSpec((1, tn), lambda j: (0, j))],
        out_specs=pl.BlockSpec((bsz, tn), lambda j: (0, j)),
        out_shape=jax.ShapeDtypeStruct((bsz, n), F32),
        compiler_params=pltpu.CompilerParams(dimension_semantics=("arbitrary",),
                                             vmem_limit_bytes=VMEM_LIMIT),
        name="adaln",
    )(c, w, b.reshape(1, n))


def _rope_kernel(pos_ref, inv_ref, sign_ref, cos_ref, sin_ref):
    ang = pos_ref[0] * inv_ref[...]
    cos_ref[0] = jnp.cos(ang)
    sin_ref[0] = jnp.sin(ang) * sign_ref[...]


def _rope_tables(positions):
    bsz, s = positions.shape
    ts = 512
    half = DA_HEAD_DIM // 2
    inv = 1.0 / (ROPE_THETA ** (jnp.arange(0, DA_HEAD_DIM, 2, dtype=F32) / DA_HEAD_DIM))
    inv2 = jnp.concatenate([inv, inv]).reshape(1, DA_HEAD_DIM)
    sign = jnp.concatenate([-jnp.ones((half,), F32), jnp.ones((half,), F32)]).reshape(1, DA_HEAD_DIM)
    pos = positions.astype(F32).reshape(bsz, s, 1)
    tab = jax.ShapeDtypeStruct((bsz, s, DA_HEAD_DIM), F32)
    return pl.pallas_call(
        _rope_kernel,
        grid=(bsz, s // ts),
        in_specs=[pl.BlockSpec((1, ts, 1), lambda b, i: (b, i, 0)),
                  pl.BlockSpec((1, DA_HEAD_DIM), lambda b, i: (0, 0)),
                  pl.BlockSpec((1, DA_HEAD_DIM), lambda b, i: (0, 0))],
        out_specs=[pl.BlockSpec((1, ts, DA_HEAD_DIM), lambda b, i: (b, i, 0)),
                   pl.BlockSpec((1, ts, DA_HEAD_DIM), lambda b, i: (b, i, 0))],
        out_shape=[tab, tab],
        compiler_params=pltpu.CompilerParams(dimension_semantics=("arbitrary", "arbitrary")),
        name="rope_tables",
    )(pos, inv2, sign)


def _rope_apply(acc, cos, sin, scale):
    outs = []
    for h in range(SEG // DA_HEAD_DIM):
        xh = acc[:, h * DA_HEAD_DIM:(h + 1) * DA_HEAD_DIM]
        rot = pltpu.roll(xh, DA_HEAD_DIM // 2, 1)
        y = xh * cos + rot * sin
        if scale is not None:
            y = y * scale
        outs.append(y.astype(BF16))
    return jnp.concatenate(outs, axis=-1)


def _sigmoid_tanh(x):
    return 0.5 * jnp.tanh(0.5 * x) + 0.5


def _inproj_kernel(x_ref, mod_ref, w_ref, lbl_ref, cos_ref, sin_ref,
                   oq_ref, olf_ref, ok_ref, ov_ref, og_ref, odq_ref, odk_ref, odv_ref, u_ref):
    j = pl.program_id(2)

    @pl.when(j == 0)
    def _():
        m = mod_ref[0]
        u = _ln_plain(x_ref[0]) * (1.0 + m[1:2, :]) + m[0:1, :]
        u_ref[...] = u.astype(BF16)

    acc = _dot(u_ref[...], w_ref[...])

    @pl.when(j == 0)
    def _():
        oq_ref[0] = (acc * _sigmoid_tanh(acc) * (HG_DK ** -0.5)).astype(BF16)

    @pl.when(j == 1)
    def _():
        lbl = lbl_ref[...]
        e = jnp.exp(lbl - jnp.max(lbl, axis=0, keepdims=True))
        lb = e[0:1, :] / jnp.sum(e, axis=0, keepdims=True)
        f = lb + (1.0 - lb) * jax.nn.sigmoid(acc)
        olf_ref[0] = jnp.log(f)
        ok_ref[0] = (1.0 - f).astype(BF16)

    @pl.when(j == 2)
    def _():
        ov_ref[0] = acc.astype(BF16)

    @pl.when(j == 3)
    def _():
        og_ref[0] = _sigmoid_tanh(acc).astype(BF16)

    @pl.when(j == 4)
    def _():
        odq_ref[0] = _rope_apply(acc, cos_ref[0], sin_ref[0], DA_HEAD_DIM ** -0.5 * math.log2(math.e))

    @pl.when(j == 5)
    def _():
        odk_ref[0] = _rope_apply(acc, cos_ref[0], sin_ref[0], None)

    @pl.when(j == 6)
    def _():
        odv_ref[0] = acc.astype(BF16)


def _inproj(x, mod, w_in, lb_logits, cos, sin):
    bsz, s, d = x.shape
    tm = 512
    row = lambda b, i, j: (b, i, 0)
    seg_bf = jax.ShapeDtypeStruct((bsz, s, SEG), BF16)
    seg_f32 = jax.ShapeDtypeStruct((bsz, s, SEG), F32)
    out_spec = pl.BlockSpec((1, tm, SEG), row)
    return pl.pallas_call(
        _inproj_kernel,
        grid=(bsz, s // tm, N_SEG),
        in_specs=[pl.BlockSpec((1, tm, d), row),
                  pl.BlockSpec((1, 6, d), lambda b, i, j: (b, 0, 0)),
                  pl.BlockSpec((d, SEG), lambda b, i, j: (0, j)),
                  pl.BlockSpec(lb_logits.shape, lambda b, i, j: (0, 0)),
                  pl.BlockSpec((1, tm, DA_HEAD_DIM), row),
                  pl.BlockSpec((1, tm, DA_HEAD_DIM), row)],
        out_specs=[out_spec] * 8,
        out_shape=[seg_bf, seg_f32, seg_bf, seg_bf, seg_bf, seg_bf, seg_bf, seg_bf],
        scratch_shapes=[pltpu.VMEM((tm, d), BF16)],
        compiler_params=pltpu.CompilerParams(
            dimension_semantics=("arbitrary", "arbitrary", "arbitrary"),
            vmem_limit_bytes=VMEM_LIMIT),
        name="inproj",
    )(x, mod, w_in, lb_logits, cos, sin)


HG_BLOCK = 256
HG_HEADS_PER_STEP = 8


def _hgrn_consts():
    L, C = HG_BLOCK, HG_CHUNK
    r = np.arange(L)[:, None]
    c = np.arange(L)[None, :]
    t_cum = (((r // C) == (c // C)) & (c <= r)).astype(np.float32)
    return jnp.asarray(t_cum, BF16), jnp.asarray(t_cum)


def _bcast_chunk_rows(x, row_in_chunk):
    L, C = HG_BLOCK, HG_CHUNK
    parts = [jnp.broadcast_to(x[c * C + row_in_chunk:c * C + row_in_chunk + 1, :], (C, x.shape[1]))
             for c in range(L // C)]
    return jnp.concatenate(parts, axis=0)


def _hgrn_head(q, lf, k, v, g, nw, tmat, maskf, st):
    L, C = HG_BLOCK, HG_CHUNK

    hi = lf.astype(BF16)
    r1 = lf - hi.astype(F32)
    mid = r1.astype(BF16)
    lo = (r1 - mid.astype(F32)).astype(BF16)
    gg = _dot(tmat, jnp.concatenate([hi, mid, lo], axis=1))
    yield
    g_cum = gg[:, :HG_DK] + gg[:, HG_DK:2 * HG_DK] + gg[:, 2 * HG_DK:]
    g_mid = _bcast_chunk_rows(g_cum, C // 2 - 1)
    g_last = _bcast_chunk_rows(g_cum, C - 1)

    qa = (q * jnp.exp(g_cum - g_mid)).astype(BF16)
    ka = (k * jnp.exp(g_mid - g_cum)).astype(BF16)
    qd = (q * jnp.exp(g_cum)).astype(BF16)
    kd = (k * jnp.exp(g_last - g_cum)).astype(BF16)
    dl = jnp.exp(g_last)

    a = _dot_nt(qa, ka)
    yield
    a = jnp.where(maskf > 0.0, a, 0.0).astype(BF16)
    o_intra = _dot(a, v)
    yield

    zeros = jnp.zeros((C, HG_DK), BF16)
    upds = []
    for p in range(L // (2 * C)):
        r0 = 2 * p * C
        rhs = jnp.concatenate(
            [jnp.concatenate([kd[r0:r0 + C], zeros], axis=1),
             jnp.concatenate([zeros, kd[r0 + C:r0 + 2 * C]], axis=1)], axis=0)
        u2 = _dot_tn(v[r0:r0 + 2 * C], rhs)
        upds += [u2[:, :HG_DK], u2[:, HG_DK:]]
    yield

    starts = []
    for c in range(L // C):
        starts.append(st.astype(BF16))
        st = st * dl[c * C:c * C + 1, :] + upds[c]

    outs = [o_intra[c * C:(c + 1) * C] + _dot_nt(qd[c * C:(c + 1) * C], starts[c])
            for c in range(L // C)]
    yield
    o = jnp.concatenate(outs, axis=0)
    ms = jnp.mean(o * o, -1, keepdims=True)
    y = o * lax.rsqrt(ms + RMS_EPS) * nw * g
    return y.astype(BF16), st


def _hgrn_kernel(q_ref, lf_ref, k_ref, v_ref, g_ref, nw_ref, tmat_ref, mask_ref, o_ref, st_ref):
    @pl.when(pl.program_id(2) == 0)
    def _():
        st_ref[...] = jnp.zeros_like(st_ref)

    lanes = [slice(h * HG_DK, (h + 1) * HG_DK) for h in range(HG_HEADS_PER_STEP)]
    heads = [_hgrn_head(q_ref[0, :, ln].astype(F32), lf_ref[0, :, ln], k_ref[0, :, ln].astype(F32),
                        v_ref[0, :, ln], g_ref[0, :, ln].astype(F32), nw_ref[...], tmat_ref[...],
                        mask_ref[...], st_ref[h]) for h, ln in enumerate(lanes)]
    for h, (y, st) in enumerate(_run_interleaved(heads)):
        o_ref[0, :, lanes[h]] = y
        st_ref[h] = st


def _hgrn(qs, lf, ks, vs, gs, norm_w):
    bsz, s, _ = qs.shape
    L = HG_BLOCK
    hp = HG_HEADS_PER_STEP
    tmat, mask = _hgrn_consts()
    blk = pl.BlockSpec((1, L, hp * HG_DK), lambda b, h, n: (b, n, h))
    const = lambda shape: pl.BlockSpec(shape, lambda b, h, n: (0, 0))
    return pl.pallas_call(
        _hgrn_kernel,
        grid=(bsz, HG_HEADS // hp, s // L),
        in_specs=[blk, blk, blk, blk, blk, const((1, HG_DK)), const((L, L)), const((L, L))],
        out_specs=blk,
        out_shape=jax.ShapeDtypeStruct((bsz, s, HG_WIDTH), BF16),
        scratch_shapes=[pltpu.VMEM((hp, HG_DK, HG_DK), F32)],
        compiler_params=pltpu.CompilerParams(
            dimension_semantics=("arbitrary", "arbitrary", "arbitrary"),
            vmem_limit_bytes=VMEM_LIMIT),
        name="hgrn",
    )(qs, lf, ks, vs, gs, norm_w.reshape(1, HG_DK), tmat, mask)


ATT_BLOCK = 512
ATT_COLS = 256
ATT_ONES = 16


def _attn_scores(k_ref, q_t, n, s_ref, slot):
    T, C, D = ATT_BLOCK, ATT_COLS, DA_HEAD_DIM
    kb = k_ref[0, pl.ds(pl.multiple_of(n * T, T), T), :]
    for t in range(2):
        for c in range(T // C):
            s_ref[slot, t * (T // C) + c] = _dot(kb[:, t * D:(t + 1) * D],
                                                 q_t[t * D:(t + 1) * D, c * C:(c + 1) * C])


def _attn_chain(s, v_t, acc_ref, m_ref, cols, masked):
    if masked:
        key = lax.broadcasted_iota(jnp.int32, s.shape, 0)
        qry = lax.broadcasted_iota(jnp.int32, s.shape, 1) + cols.start
        s = jnp.where(key <= qry, s, jnp.finfo(F32).min)
    m_prev = m_ref[:, cols]
    m_new = jnp.maximum(m_prev, jnp.max(s, axis=0, keepdims=True))
    alpha = jnp.exp2(m_prev - m_new)
    p = jnp.exp2(s - m_new)
    m_ref[:, cols] = m_new
    pv = _dot(v_t, p.astype(BF16))
    yield
    acc_ref[:, cols] = alpha * acc_ref[:, cols] + pv


def _attn_kernel(q_ref, k_ref, v_ref, lq1_ref, lk1_ref, lq2_ref, lk2_ref, sw_ref, o_ref,
                 vt_ref, s_ref, acc1_ref, acc2_ref, m1_ref, m2_ref):
    T = ATT_BLOCK
    C = ATT_COLS
    DV = 2 * DA_HEAD_DIM
    nq = q_ref.shape[1] // T

    for j in range(nq):
        vt_ref[j, :DV, :] = v_ref[0, j * T:(j + 1) * T, :].T
        vt_ref[j, DV:, :] = jnp.ones((ATT_ONES, T), BF16)

    lam = (jnp.exp(jnp.sum(lq1_ref[...] * lk1_ref[...], axis=-1, keepdims=True))
           - jnp.exp(jnp.sum(lq2_ref[...] * lk2_ref[...], axis=-1, keepdims=True))
           + LAMBDA_INIT)

    def queries_t(i):
        return q_ref[0, pl.ds(pl.multiple_of(i * T, T), T), :].T

    def query_block(i, carry):
        q_t = queries_t(i)
        acc1_ref[...] = jnp.zeros_like(acc1_ref)
        acc2_ref[...] = jnp.zeros_like(acc2_ref)
        m1_ref[...] = jnp.full_like(m1_ref, -jnp.inf)
        m2_ref[...] = jnp.full_like(m2_ref, -jnp.inf)

        def step(n, slot, masked, prefetch):
            if prefetch:
                _attn_scores(k_ref, q_t, n + 1, s_ref, 1 - slot)
            vt = vt_ref[n]
            chains = []
            for t, (acc_ref, m_ref) in enumerate(((acc1_ref, m1_ref), (acc2_ref, m2_ref))):
                for c in range(T // C):
                    cols = slice(c * C, (c + 1) * C)
                    nk = (c + 1) * C if masked else T
                    chains.append(_attn_chain(s_ref[slot, t * (T // C) + c, :nk, :], vt[:, :nk],
                                              acc_ref, m_ref, cols, masked))
            _run_interleaved(chains)

        def pair(t, carry):
            step(2 * t, 0, False, True)
            step(2 * t + 1, 1, False, True)
            return carry

        lax.fori_loop(0, i // 2, pair, 0)

        @pl.when(i % 2 == 0)
        def _():
            step(i, 0, True, False)

        @pl.when(i % 2 == 1)
        def _():
            step(i - 1, 0, False, True)
            step(i, 1, True, False)

        _attn_scores(k_ref, queries_t(jnp.minimum(i + 1, nq - 1)), 0, s_ref, 0)

        inv1 = 1.0 / acc1_ref[DV:DV + 1, :]
        inv2 = lam / acc2_ref[DV:DV + 1, :]
        o_t = acc1_ref[:DV, :] * inv1 - acc2_ref[:DV, :] * inv2
        o = o_t.T
        ms = jnp.mean(o * o, -1, keepdims=True)
        y = o * lax.rsqrt(ms + RMS_EPS) * sw_ref[...] * (1.0 - LAMBDA_INIT)
        o_ref[0, pl.ds(pl.multiple_of(i * T, T), T), :] = y.astype(BF16)
        return carry

    _attn_scores(k_ref, queries_t(0), 0, s_ref, 0)
    lax.fori_loop(0, nq, query_block, 0)


def _attn(q, k, v, lq1, lk1, lq2, lk2, subln_w):
    bsz, s, _ = q.shape
    T = ATT_BLOCK
    hw = 2 * DA_HEAD_DIM
    blk = pl.BlockSpec((1, s, hw), lambda b, h: (b, 0, h))
    vec = lambda n: pl.BlockSpec((1, n), lambda b, h: (0, 0))
    return pl.pallas_call(
        _attn_kernel,
        grid=(bsz, DA_HEADS),
        in_specs=[blk, blk, blk, vec(DA_HEAD_DIM), vec(DA_HEAD_DIM), vec(DA_HEAD_DIM),
                  vec(DA_HEAD_DIM), vec(hw)],
        out_specs=blk,
        out_shape=jax.ShapeDtypeStruct((bsz, s, DA_WIDTH), BF16),
        scratch_shapes=[pltpu.VMEM((s // T, hw + ATT_ONES, T), BF16),
                        pltpu.VMEM((2, 2 * (T // ATT_COLS), T, ATT_COLS), F32),
                        pltpu.VMEM((hw + ATT_ONES, T), F32), pltpu.VMEM((hw + ATT_ONES, T), F32),
                        pltpu.VMEM((1, T), F32), pltpu.VMEM((1, T), F32)],
        compiler_params=pltpu.CompilerParams(
            dimension_semantics=("arbitrary", "arbitrary"),
            vmem_limit_bytes=VMEM_LIMIT),
        name="diff_attn",
    )(q, k, v, lq1.reshape(1, -1), lk1.reshape(1, -1), lq2.reshape(1, -1), lk2.reshape(1, -1),
      subln_w.reshape(1, -1))


OUT_ROWS = 128


def _outproj_chain(yh, ya, w_ref, x, m, g, b):
    y = _dot(yh, w_ref[:HG_WIDTH, :]) + _dot(ya, w_ref[HG_WIDTH:, :])
    yield
    r = DEEPNORM_ALPHA * x + m[2:3, :] * y
    h = _ln_plain(r) * g + b
    u = _ln_plain(h) * (1.0 + m[4:5, :]) + m[3:4, :]
    return h, u.astype(BF16)


def _outproj_kernel(yh_ref, ya_ref, w_ref, x_ref, mod_ref, g_ref, b_ref, h_ref, u_ref):
    m = mod_ref[0]
    rows = [slice(r, r + OUT_ROWS) for r in range(0, x_ref.shape[1], OUT_ROWS)]
    chains = [_outproj_chain(yh_ref[0, rs, :], ya_ref[0, rs, :], w_ref, x_ref[0, rs, :], m,
                             g_ref[...], b_ref[...]) for rs in rows]
    for rs, (h, u) in zip(rows, _run_interleaved(chains)):
        h_ref[0, rs, :] = h
        u_ref[0, rs, :] = u


def _outproj(y_hg, y_da, w_out, x, mod, ln_g, ln_b):
    bsz, s, d = x.shape
    tm = 512
    row = lambda b, i: (b, i, 0)
    const2 = lambda shape: pl.BlockSpec(shape, lambda b, i: (0, 0))
    return pl.pallas_call(
        _outproj_kernel,
        grid=(bsz, s // tm),
        in_specs=[pl.BlockSpec((1, tm, HG_WIDTH), row),
                  pl.BlockSpec((1, tm, DA_WIDTH), row),
                  const2(w_out.shape),
                  pl.BlockSpec((1, tm, d), row),
                  pl.BlockSpec((1, 6, d), lambda b, i: (b, 0, 0)),
                  const2((1, d)), const2((1, d))],
        out_specs=[pl.BlockSpec((1, tm, d), row), pl.BlockSpec((1, tm, d), row)],
        out_shape=[jax.ShapeDtypeStruct((bsz, s, d), F32), jax.ShapeDtypeStruct((bsz, s, d), BF16)],
        compiler_params=pltpu.CompilerParams(dimension_semantics=("arbitrary", "arbitrary"),
                                             vmem_limit_bytes=VMEM_LIMIT),
        name="outproj_ln1",
    )(y_hg, y_da, w_out, x, mod, ln_g.reshape(1, d), ln_b.reshape(1, d))


FFN_ROWS = 512
FFN_COLS = 512


def _ffn_kernel(u_ref, wg_ref, wu_ref, wd_ref, h_ref, mod_ref, g_ref, b_ref, o_ref, acc_ref):
    j = pl.program_id(2)

    @pl.when(j == 0)
    def _():
        acc_ref[...] = jnp.zeros_like(acc_ref)

    u = u_ref[0]
    a = _dot(u, wg_ref[...])
    b = _dot(u, wu_ref[...])
    z = (a * jax.nn.sigmoid(a) * b).astype(BF16)
    acc_ref[...] += _dot(z, wd_ref[...])

    @pl.when(j == pl.num_programs(2) - 1)
    def _():
        m = mod_ref[0]
        r = DEEPNORM_ALPHA * h_ref[0] + m[5:6, :] * acc_ref[...]
        o_ref[0] = _ln_plain(r) * g_ref[...] + b_ref[...]


def _ffn(u, w_gate, w_up, w_down, h, mod, ln_g, ln_b):
    bsz, s, d = h.shape
    f = w_gate.shape[1]
    tm, tf = FFN_ROWS, FFN_COLS
    row = lambda b, i, j: (b, i, 0)
    const3 = lambda shape: pl.BlockSpec(shape, lambda b, i, j: (0, 0))
    return pl.pallas_call(
        _ffn_kernel,
        grid=(bsz, s // tm, f // tf),
        in_specs=[pl.BlockSpec((1, tm, d), row),
                  pl.BlockSpec((d, tf), lambda b, i, j: (0, j)),
                  pl.BlockSpec((d, tf), lambda b, i, j: (0, j)),
                  pl.BlockSpec((tf, d), lambda b, i, j: (j, 0)),
                  pl.BlockSpec((1, tm, d), row),
                  pl.BlockSpec((1, 6, d), lambda b, i, j: (b, 0, 0)),
                  const3((1, d)), const3((1, d))],
        out_specs=pl.BlockSpec((1, tm, d), row),
        out_shape=jax.ShapeDtypeStruct((bsz, s, d), F32),
        scratch_shapes=[pltpu.VMEM((tm, d), F32)],
        compiler_params=pltpu.CompilerParams(
            dimension_semantics=("arbitrary", "arbitrary", "arbitrary"),
            vmem_limit_bytes=VMEM_LIMIT),
        name="ffn_ln2",
    )(u, w_gate.astype(BF16), w_up.astype(BF16), w_down.astype(BF16), h, mod,
      ln_g.reshape(1, d), ln_b.reshape(1, d))


def kernel(x, c, positions, w_ada, b_ada, w_in, lb_logits, hg_norm_w, lam_q1, lam_k1, lam_q2, lam_k2,
           subln_w, w_out, ln1_g, ln1_b, w_gate, w_up, w_down, ln2_g, ln2_b):
    assert w_ada.shape[0] == DEPTH == 1
    bsz, s, d = x.shape

    mod = _adaln(c, w_ada[0], b_ada[0]).reshape(bsz, 6, d)
    cos, sin = _rope_tables(positions)

    qs, lf, ks, vs, gs, dq, dk, dv = _inproj(x, mod, w_in[0].astype(BF16), lb_logits, cos, sin)
    y_hg = _hgrn(qs, lf, ks, vs, gs, hg_norm_w[0])
    y_da = _attn(dq, dk, dv, lam_q1[0], lam_k1[0], lam_q2[0], lam_k2[0], subln_w[0])
    h1, u2 = _outproj(y_hg, y_da, w_out[0].astype(BF16), x, mod, ln1_g[0], ln1_b[0])
    return _ffn(u2, w_gate[0], w_up[0], w_down[0], h1, mod, ln2_g[0], ln2_b[0])
```

```python
import functools
import math

import numpy as np
import jax
import jax.numpy as jnp
from jax import lax
from jax.experimental import pallas as pl
from jax.experimental.pallas import tpu as pltpu

D_MODEL = 2048
HG_WIDTH = 1024
DA_WIDTH = 1024
HG_HEADS = 8
HG_DK = 128
HG_CHUNK = 32
DA_HEADS = 4
DA_HEAD_DIM = 128
ROPE_THETA = 10000.0
D_FF = 5632
DEPTH = 1
DEEPNORM_ALPHA = (2.0 * DEPTH) ** 0.25
LN_EPS = 1e-5
RMS_EPS = 1e-6
LAMBDA_INIT = 0.8 - 0.6 * math.exp(-0.3 * 0)
SEG = 1024
N_SEG = 7

VMEM_LIMIT = 56 * 1024 * 1024

F32 = jnp.float32
BF16 = jnp.bfloat16


def _dot(a, b):
    return jnp.dot(a, b, preferred_element_type=F32)


def _dot_nt(a, b):
    return lax.dot_general(a, b, (((1,), (1,)), ((), ())), preferred_element_type=F32)


def _dot_tn(a, b):
    return lax.dot_general(a, b, (((0,), (0,)), ((), ())), preferred_element_type=F32)


def _ln_plain(x):
    mu = jnp.mean(x, -1, keepdims=True)
    xc = x - mu
    var = jnp.mean(xc * xc, -1, keepdims=True)
    return xc * lax.rsqrt(var + LN_EPS)


def _run_interleaved(chains):
    pending = list(chains)
    results = {}
    while pending:
        for chain in list(pending):
            try:
                next(chain)
            except StopIteration as done:
                results[id(chain)] = done.value
                pending.remove(chain)
    return [results[id(chain)] for chain in chains]


def _adaln_kernel(c_ref, w_ref, b_ref, o_ref):
    cond = c_ref[...]
    cond = cond * jax.nn.sigmoid(cond)
    o_ref[...] = _dot(cond, w_ref[...]) + b_ref[...]


def _adaln(c, w, b):
    bsz, d = c.shape
    n = w.shape[1]
    tn = 1024
    return pl.pallas_call(
        _adaln_kernel,
        grid=(n // tn,),
        in_specs=[pl.BlockSpec((bsz, d), lambda j: (0, 0)),
                  pl.BlockSpec((d, tn), lambda j: (0, j)),
                  pl.BlockSpec((1, tn), lambda j: (0, j))],
        out_specs=pl.BlockSpec((bsz, tn), lambda j: (0, j)),
        out_shape=jax.ShapeDtypeStruct((bsz, n), F32),
        compiler_params=pltpu.CompilerParams(dimension_semantics=("arbitrary",),
                                             vmem_limit_bytes=VMEM_LIMIT),
        name="adaln",
    )(c, w, b.reshape(1, n))


def _rope_kernel(pos_ref, inv_ref, sign_ref, cos_ref, sin_ref):
    ang = pos_ref[0] * inv_ref[...]
    cos_ref[0] = jnp.cos(ang)
    sin_ref[0] = jnp.sin(ang) * sign_ref[...]


def _rope_tables(positions):
    bsz, s = positions.shape
    ts = 512
    half = DA_HEAD_DIM // 2
    inv = 1.0 / (ROPE_THETA ** (jnp.arange(0, DA_HEAD_DIM, 2, dtype=F32) / DA_HEAD_DIM))
    inv2 = jnp.concatenate([inv, inv]).reshape(1, DA_HEAD_DIM)
    sign = jnp.concatenate([-jnp.ones((half,), F32), jnp.ones((half,), F32)]).reshape(1, DA_HEAD_DIM)
    pos = positions.astype(F32).reshape(bsz, s, 1)
    tab = jax.ShapeDtypeStruct((bsz, s, DA_HEAD_DIM), F32)
    return pl.pallas_call(
        _rope_kernel,
        grid=(bsz, s // ts),
        in_specs=[pl.BlockSpec((1, ts, 1), lambda b, i: (b, i, 0)),
                  pl.BlockSpec((1, DA_HEAD_DIM), lambda b, i: (0, 0)),
                  pl.BlockSpec((1, DA_HEAD_DIM), lambda b, i: (0, 0))],
        out_specs=[pl.BlockSpec((1, ts, DA_HEAD_DIM), lambda b, i: (b, i, 0)),
                   pl.BlockSpec((1, ts, DA_HEAD_DIM), lambda b, i: (b, i, 0))],
        out_shape=[tab, tab],
        compiler_params=pltpu.CompilerParams(dimension_semantics=("arbitrary", "arbitrary")),
        name="rope_tables",
    )(pos, inv2, sign)


PROJ_ROWS = 512


def _zero_after(x):
    bits = pltpu.bitcast(x, jnp.int32)
    acc = bits[0:8, :]
    for r in range(8, bits.shape[0], 8):
        acc = acc | bits[r:r + 8, :]
    word = acc[:, 0:128]
    for c in range(128, acc.shape[1], 128):
        word = word | acc[:, c:c + 128]
    word = lax.shift_right_logical(lax.shift_right_logical(word, 16), 16)
    return pltpu.bitcast(word, F32)[0:1, :].astype(BF16)


def _lagged_steps(matmul, finish, n_tiles):
    t = pl.program_id(0)

    @pl.when(t == 0)
    def _():
        matmul(0, None)

    for parity in (0, 1):
        @pl.when(jnp.logical_and(jnp.logical_and(t > 0, t < n_tiles), t % 2 == parity))
        def _():
            matmul(parity, finish(1 - parity))

    @pl.when(t == n_tiles)
    def _():
        finish((n_tiles - 1) % 2)


def _tile_maps(tiles_per_batch, n_tiles):
    def split(k):
        return k // tiles_per_batch, k % tiles_per_batch
    cur = lambda t: (*split(jnp.minimum(t, n_tiles - 1)), 0)
    lag = lambda t: (*split(jnp.maximum(t - 1, 0)), 0)
    lag_batch = lambda t: (jnp.maximum(t - 1, 0) // tiles_per_batch, 0, 0)
    return cur, lag, lag_batch


def _lnmod_kernel(x_ref, mod_ref, u_ref):
    m = mod_ref[0]
    u_ref[0] = (_ln_plain(x_ref[0]) * (1.0 + m[1:2, :]) + m[0:1, :]).astype(BF16)


def _lnmod(x, mod):
    bsz, s, d = x.shape
    tm = PROJ_ROWS
    row = lambda b, i: (b, i, 0)
    return pl.pallas_call(
        _lnmod_kernel,
        grid=(bsz, s // tm),
        in_specs=[pl.BlockSpec((1, tm, d), row), pl.BlockSpec((1, 6, d), lambda b, i: (b, 0, 0))],
        out_specs=pl.BlockSpec((1, tm, d), row),
        out_shape=jax.ShapeDtypeStruct((bsz, s, d), BF16),
        compiler_params=pltpu.CompilerParams(dimension_semantics=("arbitrary", "arbitrary"),
                                             vmem_limit_bytes=VMEM_LIMIT),
        name="ln_mod",
    )(x, mod)


def _rope_apply(acc, cos, sin, scale):
    outs = []
    for h in range(SEG // DA_HEAD_DIM):
        xh = acc[:, h * DA_HEAD_DIM:(h + 1) * DA_HEAD_DIM]
        rot = pltpu.roll(xh, DA_HEAD_DIM // 2, 1)
        y = xh * cos + rot * sin
        if scale is not None:
            y = y * scale
        outs.append(y.astype(BF16))
    return jnp.concatenate(outs, axis=-1)


def _sigmoid_tanh(x):
    return 0.5 * jnp.tanh(0.5 * x) + 0.5


def _finish_hg_q(y, side, outs):
    q = (y * _sigmoid_tanh(y) * (HG_DK ** -0.5)).astype(BF16)
    outs[0][0] = q
    return _zero_after(q)


def _finish_hg_f(y, side, outs):
    lbl = side[0][...]
    e = jnp.exp(lbl - jnp.max(lbl, axis=0, keepdims=True))
    lb = e[0:1, :] / jnp.sum(e, axis=0, keepdims=True)
    f = lb + (1.0 - lb) * jax.nn.sigmoid(y)
    log_f = jnp.log(f)
    key = (1.0 - f).astype(BF16)
    outs[0][0] = log_f
    outs[1][0] = key
    return _zero_after(log_f) + _zero_after(key)


def _finish_cast(y, side, outs):
    outs[0][0] = y.astype(BF16)
    return None


def _finish_hg_g(y, side, outs):
    g = _sigmoid_tanh(y).astype(BF16)
    outs[0][0] = g
    return _zero_after(g)


def _finish_da_q(y, side, outs):
    q = _rope_apply(y, side[0][0], side[1][0], DA_HEAD_DIM ** -0.5 * math.log2(math.e))
    outs[0][0] = q
    return _zero_after(q)


def _finish_da_k(y, side, outs):
    k = _rope_apply(y, side[0][0], side[1][0], None)
    outs[0][0] = k
    return _zero_after(k)


def _segment_kernel(finish, n_side, n_tiles, u_ref, w_ref, *refs):
    side, outs, y_ref = refs[:n_side], refs[n_side:-1], refs[-1]

    half = SEG // 2

    def matmul(slot, zero):
        u = u_ref[0]
        y_ref[slot, :, :half] = _dot(u, w_ref[:, :half])
        if zero is not None:
            u = u + jnp.tile(zero, (1, u.shape[1] // zero.shape[1]))
        y_ref[slot, :, half:] = _dot(u, w_ref[:, half:])

    _lagged_steps(matmul, lambda slot: finish(y_ref[slot], side, outs), n_tiles)


def _segment(u, w_in, seg, finish, side, side_specs, out_dtypes, name):
    bsz, s, d = u.shape
    tm = PROJ_ROWS
    n_tiles = bsz * (s // tm)
    cur, lag, _ = _tile_maps(s // tm, n_tiles)
    specs = [pl.BlockSpec((1, tm, d), cur), pl.BlockSpec((d, SEG), lambda t: (0, seg))]
    specs += [spec(lag) for spec in side_specs]
    return pl.pallas_call(
        functools.partial(_segment_kernel, finish, len(side), n_tiles),
        grid=(n_tiles + 1,),
        in_specs=specs,
        out_specs=[pl.BlockSpec((1, tm, SEG), lag) for _ in out_dtypes],
        out_shape=[jax.ShapeDtypeStruct((bsz, s, SEG), dt) for dt in out_dtypes],
        scratch_shapes=[pltpu.VMEM((2, tm, SEG), F32)],
        compiler_params=pltpu.CompilerParams(dimension_semantics=("arbitrary",),
                                             vmem_limit_bytes=VMEM_LIMIT),
        name=name,
    )(u, w_in, *side)


def _inproj(u, w_in, lb_logits, cos, sin):
    tm = PROJ_ROWS
    table = lambda lag: pl.BlockSpec((1, tm, DA_HEAD_DIM), lag)
    const = lambda lag: pl.BlockSpec(lb_logits.shape, lambda t: (0, 0))
    qs, = _segment(u, w_in, 0, _finish_hg_q, [], [], [BF16], "inproj_hg_q")
    lf, ks = _segment(u, w_in, 1, _finish_hg_f, [lb_logits], [const], [F32, BF16], "inproj_hg_f")
    vs, = _segment(u, w_in, 2, _finish_cast, [], [], [BF16], "inproj_hg_i")
    gs, = _segment(u, w_in, 3, _finish_hg_g, [], [], [BF16], "inproj_hg_g")
    dq, = _segment(u, w_in, 4, _finish_da_q, [cos, sin], [table, table], [BF16], "inproj_da_q")
    dk, = _segment(u, w_in, 5, _finish_da_k, [cos, sin], [table, table], [BF16], "inproj_da_k")
    dv, = _segment(u, w_in, 6, _finish_cast, [], [], [BF16], "inproj_da_v")
    return qs, lf, ks, vs, gs, dq, dk, dv


HG_BLOCK = 256
HG_HEADS_PER_STEP = 8


def _hgrn_consts():
    L, C = HG_BLOCK, HG_CHUNK
    r = np.arange(L)[:, None]
    c = np.arange(L)[None, :]
    t_cum = (((r // C) == (c // C)) & (c <= r)).astype(np.float32)
    return jnp.asarray(t_cum, BF16), jnp.asarray(t_cum)


def _bcast_chunk_rows(x, row_in_chunk):
    L, C = HG_BLOCK, HG_CHUNK
    parts = [jnp.broadcast_to(x[c * C + row_in_chunk:c * C + row_in_chunk + 1, :], (C, x.shape[1]))
             for c in range(L // C)]
    return jnp.concatenate(parts, axis=0)


def _hgrn_head(q, lf, k, v, g, nw, tmat, maskf, st):
    L, C = HG_BLOCK, HG_CHUNK

    hi = lf.astype(BF16)
    r1 = lf - hi.astype(F32)
    mid = r1.astype(BF16)
    lo = (r1 - mid.astype(F32)).astype(BF16)
    gg = _dot(tmat, jnp.concatenate([hi, mid, lo], axis=1))
    yield
    g_cum = gg[:, :HG_DK] + gg[:, HG_DK:2 * HG_DK] + gg[:, 2 * HG_DK:]
    g_mid = _bcast_chunk_rows(g_cum, C // 2 - 1)
    g_last = _bcast_chunk_rows(g_cum, C - 1)

    qa = (q * jnp.exp(g_cum - g_mid)).astype(BF16)
    ka = (k * jnp.exp(g_mid - g_cum)).astype(BF16)
    qd = (q * jnp.exp(g_cum)).astype(BF16)
    kd = (k * jnp.exp(g_last - g_cum)).astype(BF16)
    dl = jnp.exp(g_last)

    a = _dot_nt(qa, ka)
    yield
    a = jnp.where(maskf > 0.0, a, 0.0).astype(BF16)
    o_intra = _dot(a, v)
    yield

    zeros = jnp.zeros((C, HG_DK), BF16)
    upds = []
    for p in range(L // (2 * C)):
        r0 = 2 * p * C
        rhs = jnp.concatenate(
            [jnp.concatenate([kd[r0:r0 + C], zeros], axis=1),
             jnp.concatenate([zeros, kd[r0 + C:r0 + 2 * C]], axis=1)], axis=0)
        u2 = _dot_tn(v[r0:r0 + 2 * C], rhs)
        upds += [u2[:, :HG_DK], u2[:, HG_DK:]]
    yield

    starts = []
    for c in range(L // C):
        starts.append(st.astype(BF16))
        st = st * dl[c * C:c * C + 1, :] + upds[c]

    outs = [o_intra[c * C:(c + 1) * C] + _dot_nt(qd[c * C:(c + 1) * C], starts[c])
            for c in range(L // C)]
    yield
    o = jnp.concatenate(outs, axis=0)
    ms = jnp.mean(o * o, -1, keepdims=True)
    y = o * lax.rsqrt(ms + RMS_EPS) * nw * g
    return y.astype(BF16), st


def _hgrn_kernel(q_ref, lf_ref, k_ref, v_ref, g_ref, nw_ref, tmat_ref, mask_ref, o_ref, st_ref):
    @pl.when(pl.program_id(2) == 0)
    def _():
        st_ref[...] = jnp.zeros_like(st_ref)

    lanes = [slice(h * HG_DK, (h + 1) * HG_DK) for h in range(HG_HEADS_PER_STEP)]
    heads = [_hgrn_head(q_ref[0, :, ln].astype(F32), lf_ref[0, :, ln], k_ref[0, :, ln].astype(F32),
                        v_ref[0, :, ln], g_ref[0, :, ln].astype(F32), nw_ref[...], tmat_ref[...],
                        mask_ref[...], st_ref[h]) for h, ln in enumerate(lanes)]
    for h, (y, st) in enumerate(_run_interleaved(heads)):
        o_ref[0, :, lanes[h]] = y
        st_ref[h] = st


def _hgrn(qs, lf, ks, vs, gs, norm_w):
    bsz, s, _ = qs.shape
    L = HG_BLOCK
    hp = HG_HEADS_PER_STEP
    tmat, mask = _hgrn_consts()
    blk = pl.BlockSpec((1, L, hp * HG_DK), lambda b, h, n: (b, n, h))
    const = lambda shape: pl.BlockSpec(shape, lambda b, h, n: (0, 0))
    return pl.pallas_call(
        _hgrn_kernel,
        grid=(bsz, HG_HEADS // hp, s // L),
        in_specs=[blk, blk, blk, blk, blk, const((1, HG_DK)), const((L, L)), const((L, L))],
        out_specs=blk,
        out_shape=jax.ShapeDtypeStruct((bsz, s, HG_WIDTH), BF16),
        scratch_shapes=[pltpu.VMEM((hp, HG_DK, HG_DK), F32)],
        compiler_params=pltpu.CompilerParams(
            dimension_semantics=("arbitrary", "arbitrary", "arbitrary"),
            vmem_limit_bytes=VMEM_LIMIT),
        name="hgrn",
    )(qs, lf, ks, vs, gs, norm_w.reshape(1, HG_DK), tmat, mask)


ATT_BLOCK = 512
ATT_COLS = 256
ATT_ONES = 16


def _attn_scores(k_ref, q_t, n, s_ref, slot):
    T, C, D = ATT_BLOCK, ATT_COLS, DA_HEAD_DIM
    kb = k_ref[0, pl.ds(pl.multiple_of(n * T, T), T), :]
    for t in range(2):
        for c in range(T // C):
            s_ref[slot, t * (T // C) + c] = _dot(kb[:, t * D:(t + 1) * D],
                                                 q_t[t * D:(t + 1) * D, c * C:(c + 1) * C])


def _attn_chain(s, v_t, acc_ref, m_ref, cols, masked):
    if masked:
        key = lax.broadcasted_iota(jnp.int32, s.shape, 0)
        qry = lax.broadcasted_iota(jnp.int32, s.shape, 1) + cols.start
        s = jnp.where(key <= qry, s, jnp.finfo(F32).min)
    m_prev = m_ref[:, cols]
    m_new = jnp.maximum(m_prev, jnp.max(s, axis=0, keepdims=True))
    alpha = jnp.exp2(m_prev - m_new)
    p = jnp.exp2(s - m_new)
    m_ref[:, cols] = m_new
    pv = _dot(v_t, p.astype(BF16))
    yield
    acc_ref[:, cols] = alpha * acc_ref[:, cols] + pv


def _attn_kernel(q_ref, k_ref, v_ref, lq1_ref, lk1_ref, lq2_ref, lk2_ref, sw_ref, o_ref,
                 vt_ref, s_ref, acc1_ref, acc2_ref, m1_ref, m2_ref):
    T = ATT_BLOCK
    C = ATT_COLS
    DV = 2 * DA_HEAD_DIM
    nq = q_ref.shape[1] // T

    for j in range(nq):
        vt_ref[j, :DV, :] = v_ref[0, j * T:(j + 1) * T, :].T
        vt_ref[j, DV:, :] = jnp.ones((ATT_ONES, T), BF16)

    lam = (jnp.exp(jnp.sum(lq1_ref[...] * lk1_ref[...], axis=-1, keepdims=True))
           - jnp.exp(jnp.sum(lq2_ref[...] * lk2_ref[...], axis=-1, keepdims=True))
           + LAMBDA_INIT)

    def queries_t(i):
        return q_ref[0, pl.ds(pl.multiple_of(i * T, T), T), :].T

    def query_block(i, carry):
        q_t = queries_t(i)
        acc1_ref[...] = jnp.zeros_like(acc1_ref)
        acc2_ref[...] = jnp.zeros_like(acc2_ref)
        m1_ref[...] = jnp.full_like(m1_ref, -jnp.inf)
        m2_ref[...] = jnp.full_like(m2_ref, -jnp.inf)

        def step(n, slot, masked, prefetch):
            if prefetch:
                _attn_scores(k_ref, q_t, n + 1, s_ref, 1 - slot)
            vt = vt_ref[n]
            chains = []
            for t, (acc_ref, m_ref) in enumerate(((acc1_ref, m1_ref), (acc2_ref, m2_ref))):
                for c in range(T // C):
                    cols = slice(c * C, (c + 1) * C)
                    nk = (c + 1) * C if masked else T
                    chains.append(_attn_chain(s_ref[slot, t * (T // C) + c, :nk, :], vt[:, :nk],
                                              acc_ref, m_ref, cols, masked))
            _run_interleaved(chains)

        def pair(t, carry):
            step(2 * t, 0, False, True)
            step(2 * t + 1, 1, False, True)
            return carry

        lax.fori_loop(0, i // 2, pair, 0)

        @pl.when(i % 2 == 0)
        def _():
            step(i, 0, True, False)

        @pl.when(i % 2 == 1)
        def _():
            step(i - 1, 0, False, True)
            step(i, 1, True, False)

        _attn_scores(k_ref, queries_t(jnp.minimum(i + 1, nq - 1)), 0, s_ref, 0)

        inv1 = 1.0 / acc1_ref[DV:DV + 1, :]
        inv2 = lam / acc2_ref[DV:DV + 1, :]
        o_t = acc1_ref[:DV, :] * inv1 - acc2_ref[:DV, :] * inv2
        o = o_t.T
        ms = jnp.mean(o * o, -1, keepdims=True)
        y = o * lax.rsqrt(ms + RMS_EPS) * sw_ref[...] * (1.0 - LAMBDA_INIT)
        o_ref[0, pl.ds(pl.multiple_of(i * T, T), T), :] = y.astype(BF16)
        return carry

    _attn_scores(k_ref, queries_t(0), 0, s_ref, 0)
    lax.fori_loop(0, nq, query_block, 0)


def _attn(q, k, v, lq1, lk1, lq2, lk2, subln_w):
    bsz, s, _ = q.shape
    T = ATT_BLOCK
    hw = 2 * DA_HEAD_DIM
    blk = pl.BlockSpec((1, s, hw), lambda b, h: (b, 0, h))
    vec = lambda n: pl.BlockSpec((1, n), lambda b, h: (0, 0))
    return pl.pallas_call(
        _attn_kernel,
        grid=(bsz, DA_HEADS),
        in_specs=[blk, blk, blk, vec(DA_HEAD_DIM), vec(DA_HEAD_DIM), vec(DA_HEAD_DIM),
                  vec(DA_HEAD_DIM), vec(hw)],
        out_specs=blk,
        out_shape=jax.ShapeDtypeStruct((bsz, s, DA_WIDTH), BF16),
        scratch_shapes=[pltpu.VMEM((s // T, hw + ATT_ONES, T), BF16),
                        pltpu.VMEM((2, 2 * (T // ATT_COLS), T, ATT_COLS), F32),
                        pltpu.VMEM((hw + ATT_ONES, T), F32), pltpu.VMEM((hw + ATT_ONES, T), F32),
                        pltpu.VMEM((1, T), F32), pltpu.VMEM((1, T), F32)],
        compiler_params=pltpu.CompilerParams(
            dimension_semantics=("arbitrary", "arbitrary"),
            vmem_limit_bytes=VMEM_LIMIT),
        name="diff_attn",
    )(q, k, v, lq1.reshape(1, -1), lk1.reshape(1, -1), lq2.reshape(1, -1), lk2.reshape(1, -1),
      subln_w.reshape(1, -1))


def _outproj_kernel(n_tiles, yh_ref, ya_ref, w_ref, x_ref, mod_ref, g_ref, b_ref, h_ref, u_ref, y_ref):
    half = w_ref.shape[1] // 2

    def matmul(slot, zero):
        yh = yh_ref[0]
        ya = ya_ref[0]
        y_ref[slot, :, :half] = _dot(yh, w_ref[:HG_WIDTH, :half]) + _dot(ya, w_ref[HG_WIDTH:, :half])
        if zero is not None:
            ya = ya + jnp.tile(zero, (1, ya.shape[1] // zero.shape[1]))
        y_ref[slot, :, half:] = _dot(yh, w_ref[:HG_WIDTH, half:]) + _dot(ya, w_ref[HG_WIDTH:, half:])

    def finish(slot):
        m = mod_ref[0]
        r = DEEPNORM_ALPHA * x_ref[0] + m[2:3, :] * y_ref[slot]
        h = _ln_plain(r) * g_ref[...] + b_ref[...]
        h_ref[0] = h
        u = (_ln_plain(h) * (1.0 + m[4:5, :]) + m[3:4, :]).astype(BF16)
        u_ref[0] = u
        return _zero_after(u)

    _lagged_steps(matmul, finish, n_tiles)


def _outproj(y_hg, y_da, w_out, x, mod, ln_g, ln_b):
    bsz, s, d = x.shape
    tm = PROJ_ROWS
    n_tiles = bsz * (s // tm)
    cur, lag, lag_batch = _tile_maps(s // tm, n_tiles)
    const = lambda shape: pl.BlockSpec(shape, lambda t: (0, 0))
    return pl.pallas_call(
        functools.partial(_outproj_kernel, n_tiles),
        grid=(n_tiles + 1,),
        in_specs=[pl.BlockSpec((1, tm, HG_WIDTH), cur),
                  pl.BlockSpec((1, tm, DA_WIDTH), cur),
                  const(w_out.shape),
                  pl.BlockSpec((1, tm, d), lag),
                  pl.BlockSpec((1, 6, d), lag_batch),
                  const((1, d)), const((1, d))],
        out_specs=[pl.BlockSpec((1, tm, d), lag), pl.BlockSpec((1, tm, d), lag)],
        out_shape=[jax.ShapeDtypeStruct((bsz, s, d), F32), jax.ShapeDtypeStruct((bsz, s, d), BF16)],
        scratch_shapes=[pltpu.VMEM((2, tm, d), F32)],
        compiler_params=pltpu.CompilerParams(dimension_semantics=("arbitrary",),
                                             vmem_limit_bytes=VMEM_LIMIT),
        name="outproj_ln1",
    )(y_hg, y_da, w_out, x, mod, ln_g.reshape(1, d), ln_b.reshape(1, d))


FFN_ROWS = 512
FFN_COLS = 512


def _ffn_kernel(u_ref, wg_ref, wu_ref, wd_ref, h_ref, mod_ref, g_ref, b_ref, o_ref, acc_ref):
    j = pl.program_id(2)

    @pl.when(j == 0)
    def _():
        acc_ref[...] = jnp.zeros_like(acc_ref)

    u = u_ref[0]
    a = _dot(u, wg_ref[...])
    b = _dot(u, wu_ref[...])
    z = (a * jax.nn.sigmoid(a) * b).astype(BF16)
    acc_ref[...] += _dot(z, wd_ref[...])

    @pl.when(j == pl.num_programs(2) - 1)
    def _():
        m = mod_ref[0]
        r = DEEPNORM_ALPHA * h_ref[0] + m[5:6, :] * acc_ref[...]
        o_ref[0] = _ln_plain(r) * g_ref[...] + b_ref[...]


def _ffn(u, w_gate, w_up, w_down, h, mod, ln_g, ln_b):
    bsz, s, d = h.shape
    f = w_gate.shape[1]
    tm, tf = FFN_ROWS, FFN_COLS
    row = lambda b, i, j: (b, i, 0)
    const3 = lambda shape: pl.BlockSpec(shape, lambda b, i, j: (0, 0))
    return pl.pallas_call(
        _ffn_kernel,
        grid=(bsz, s // tm, f // tf),
        in_specs=[pl.BlockSpec((1, tm, d), row),
                  pl.BlockSpec((d, tf), lambda b, i, j: (0, j)),
                  pl.BlockSpec((d, tf), lambda b, i, j: (0, j)),
                  pl.BlockSpec((tf, d), lambda b, i, j: (j, 0)),
                  pl.BlockSpec((1, tm, d), row),
                  pl.BlockSpec((1, 6, d), lambda b, i, j: (b, 0, 0)),
                  const3((1, d)), const3((1, d))],
        out_specs=pl.BlockSpec((1, tm, d), row),
        out_shape=jax.ShapeDtypeStruct((bsz, s, d), F32),
        scratch_shapes=[pltpu.VMEM((tm, d), F32)],
        compiler_params=pltpu.CompilerParams(
            dimension_semantics=("arbitrary", "arbitrary", "arbitrary"),
            vmem_limit_bytes=VMEM_LIMIT),
        name="ffn_ln2",
    )(u, w_gate.astype(BF16), w_up.astype(BF16), w_down.astype(BF16), h, mod,
      ln_g.reshape(1, d), ln_b.reshape(1, d))


def kernel(x, c, positions, w_ada, b_ada, w_in, lb_logits, hg_norm_w, lam_q1, lam_k1, lam_q2, lam_k2,
           subln_w, w_out, ln1_g, ln1_b, w_gate, w_up, w_down, ln2_g, ln2_b):
    assert w_ada.shape[0] == DEPTH == 1
    bsz, s, d = x.shape

    mod = _adaln(c, w_ada[0], b_ada[0]).reshape(bsz, 6, d)
    cos, sin = _rope_tables(positions)

    u1 = _lnmod(x, mod)
    qs, lf, ks, vs, gs, dq, dk, dv = _inproj(u1, w_in[0].astype(BF16), lb_logits, cos, sin)
    y_hg = _hgrn(qs, lf, ks, vs, gs, hg_norm_w[0])
    y_da = _attn(dq, dk, dv, lam_q1[0], lam_k1[0], lam_q2[0], lam_k2[0], subln_w[0])
    h1, u2 = _outproj(y_hg, y_da, w_out[0].astype(BF16), x, mod, ln1_g[0], ln1_b[0])
    return _ffn(u2, w_gate[0], w_up[0], w_down[0], h1, mod, ln2_g[0], ln2_b[0])
```

```python
import functools
import math

import numpy as np
import jax
import jax.numpy as jnp
from jax import lax
from jax.experimental import pallas as pl
from jax.experimental.pallas import tpu as pltpu

D_MODEL = 2048
HG_WIDTH = 1024
DA_WIDTH = 1024
HG_HEADS = 8
HG_DK = 128
HG_CHUNK = 32
DA_HEADS = 4
DA_HEAD_DIM = 128
ROPE_THETA = 10000.0
D_FF = 5632
DEPTH = 1
DEEPNORM_ALPHA = (2.0 * DEPTH) ** 0.25
LN_EPS = 1e-5
RMS_EPS = 1e-6
LAMBDA_INIT = 0.8 - 0.6 * math.exp(-0.3 * 0)
SEG = 1024
N_SEG = 7

VMEM_LIMIT = 56 * 1024 * 1024

F32 = jnp.float32
BF16 = jnp.bfloat16


def _dot(a, b):
    return jnp.dot(a, b, preferred_element_type=F32)


def _dot_nt(a, b):
    return lax.dot_general(a, b, (((1,), (1,)), ((), ())), preferred_element_type=F32)


def _dot_tn(a, b):
    return lax.dot_general(a, b, (((0,), (0,)), ((), ())), preferred_element_type=F32)


def _ln_plain(x):
    mu = jnp.mean(x, -1, keepdims=True)
    xc = x - mu
    var = jnp.mean(xc * xc, -1, keepdims=True)
    return xc * lax.rsqrt(var + LN_EPS)


def _run_interleaved(chains):
    pending = list(chains)
    results = {}
    while pending:
        for chain in list(pending):
            try:
                next(chain)
            except StopIteration as done:
                results[id(chain)] = done.value
                pending.remove(chain)
    return [results[id(chain)] for chain in chains]


def _adaln_kernel(c_ref, w_ref, b_ref, o_ref):
    cond = c_ref[...]
    cond = cond * jax.nn.sigmoid(cond)
    o_ref[...] = _dot(cond, w_ref[...]) + b_ref[...]


def _adaln(c, w, b):
    bsz, d = c.shape
    n = w.shape[1]
    tn = 1024
    return pl.pallas_call(
        _adaln_kernel,
        grid=(n // tn,),
        in_specs=[pl.BlockSpec((bsz, d), lambda j: (0, 0)),
                  pl.BlockSpec((d, tn), lambda j: (0, j)),
                  pl.BlockSpec((1, tn), lambda j: (0, j))],
        out_specs=pl.BlockSpec((bsz, tn), lambda j: (0, j)),
        out_shape=jax.ShapeDtypeStruct((bsz, n), F32),
        compiler_params=pltpu.CompilerParams(dimension_semantics=("arbitrary",),
                                             vmem_limit_bytes=VMEM_LIMIT),
        name="adaln",
    )(c, w, b.reshape(1, n))


def _rope_kernel(pos_ref, inv_ref, sign_ref, cos_ref, sin_ref):
    hs = pos_ref.shape[1] // 2
    lo = lax.broadcasted_iota(jnp.int32, (hs, DA_HEAD_DIM), 1) < DA_HEAD_DIM // 2
    ang = jnp.where(lo, pos_ref[0, :hs, :], pos_ref[0, hs:, :]) * inv_ref[...]
    for fn, ref, scale in ((jnp.cos, cos_ref, None), (jnp.sin, sin_ref, sign_ref[...])):
        val = fn(ang)
        swapped = pltpu.roll(val, DA_HEAD_DIM // 2, 1)
        top = jnp.where(lo, val, swapped)
        bottom = jnp.where(lo, swapped, val)
        ref[0, :hs, :] = top if scale is None else top * scale
        ref[0, hs:, :] = bottom if scale is None else bottom * scale


def _rope_tables(positions):
    bsz, s = positions.shape
    ts = 512
    half = DA_HEAD_DIM // 2
    inv = 1.0 / (ROPE_THETA ** (jnp.arange(0, DA_HEAD_DIM, 2, dtype=F32) / DA_HEAD_DIM))
    inv2 = jnp.concatenate([inv, inv]).reshape(1, DA_HEAD_DIM)
    sign = jnp.concatenate([-jnp.ones((half,), F32), jnp.ones((half,), F32)]).reshape(1, DA_HEAD_DIM)
    pos = positions.astype(F32).reshape(bsz, s, 1)
    tab = jax.ShapeDtypeStruct((bsz, s, DA_HEAD_DIM), F32)
    return pl.pallas_call(
        _rope_kernel,
        grid=(bsz, s // ts),
        in_specs=[pl.BlockSpec((1, ts, 1), lambda b, i: (b, i, 0)),
                  pl.BlockSpec((1, DA_HEAD_DIM), lambda b, i: (0, 0)),
                  pl.BlockSpec((1, DA_HEAD_DIM), lambda b, i: (0, 0))],
        out_specs=[pl.BlockSpec((1, ts, DA_HEAD_DIM), lambda b, i: (b, i, 0)),
                   pl.BlockSpec((1, ts, DA_HEAD_DIM), lambda b, i: (b, i, 0))],
        out_shape=[tab, tab],
        compiler_params=pltpu.CompilerParams(dimension_semantics=("arbitrary", "arbitrary")),
        name="rope_tables",
    )(pos, inv2, sign)


PROJ_ROWS = 512


def _zero_after(x):
    bits = pltpu.bitcast(x, jnp.int32)
    acc = bits[0:8, :]
    for r in range(8, bits.shape[0], 8):
        acc = acc | bits[r:r + 8, :]
    word = acc[:, 0:128]
    for c in range(128, acc.shape[1], 128):
        word = word | acc[:, c:c + 128]
    word = lax.shift_right_logical(lax.shift_right_logical(word, 16), 16)
    return pltpu.bitcast(word, F32)[0:1, :].astype(BF16)


def _lagged_steps(matmul, finish, n_tiles):
    t = pl.program_id(0)

    @pl.when(t == 0)
    def _():
        matmul(0, None)

    for parity in (0, 1):
        @pl.when(jnp.logical_and(jnp.logical_and(t > 0, t < n_tiles), t % 2 == parity))
        def _():
            matmul(parity, finish(1 - parity))

    @pl.when(t == n_tiles)
    def _():
        finish((n_tiles - 1) % 2)


def _tile_maps(tiles_per_batch, n_tiles):
    def split(k):
        return k // tiles_per_batch, k % tiles_per_batch
    cur = lambda t: (*split(jnp.minimum(t, n_tiles - 1)), 0)
    lag = lambda t: (*split(jnp.maximum(t - 1, 0)), 0)
    lag_batch = lambda t: (jnp.maximum(t - 1, 0) // tiles_per_batch, 0, 0)
    return cur, lag, lag_batch


def _rope_apply(acc, cos, sin, scale):
    outs = []
    for h in range(SEG // DA_HEAD_DIM):
        xh = acc[:, h * DA_HEAD_DIM:(h + 1) * DA_HEAD_DIM]
        rot = pltpu.roll(xh, DA_HEAD_DIM // 2, 1)
        y = xh * cos + rot * sin
        if scale is not None:
            y = y * scale
        outs.append(y.astype(BF16))
    return jnp.concatenate(outs, axis=-1)


def _sigmoid_tanh(x):
    return 0.5 * jnp.tanh(0.5 * x) + 0.5


def _finish_hg_q(y, side, outs):
    q = (y * _sigmoid_tanh(y) * (HG_DK ** -0.5)).astype(BF16)
    outs[0][0] = q
    return _zero_after(q)


def _finish_hg_f(y, side, outs):
    lbl = side[0][...]
    e = jnp.exp(lbl - jnp.max(lbl, axis=0, keepdims=True))
    lb = e[0:1, :] / jnp.sum(e, axis=0, keepdims=True)
    f = lb + (1.0 - lb) * jax.nn.sigmoid(y)
    log_f = jnp.log(f)
    key = (1.0 - f).astype(BF16)
    outs[0][0] = log_f
    outs[1][0] = key
    return _zero_after(log_f) + _zero_after(key)


def _finish_cast(y, side, outs):
    outs[0][0] = y.astype(BF16)
    return None


def _finish_hg_g(y, side, outs):
    g = _sigmoid_tanh(y).astype(BF16)
    outs[0][0] = g
    return _zero_after(g)


def _finish_da_q(y, side, outs):
    q = _rope_apply(y, side[0][0], side[1][0], DA_HEAD_DIM ** -0.5 * math.log2(math.e))
    outs[0][0] = q
    return _zero_after(q)


def _finish_da_k(y, side, outs):
    k = _rope_apply(y, side[0][0], side[1][0], None)
    outs[0][0] = k
    return _zero_after(k)


def _segment_kernel(finish, n_side, n_tiles, u_ref, w_ref, *refs):
    side, outs, y_ref = refs[:n_side], refs[n_side:-1], refs[-1]

    half = SEG // 2

    def matmul(slot, zero):
        u = u_ref[0]
        y_ref[slot, :, :half] = _dot(u, w_ref[:, :half])
        if zero is not None:
            u = u + jnp.tile(zero, (1, u.shape[1] // zero.shape[1]))
        y_ref[slot, :, half:] = _dot(u, w_ref[:, half:])

    _lagged_steps(matmul, lambda slot: finish(y_ref[slot], side, outs), n_tiles)


def _segment(u, w_in, seg, finish, side, side_specs, out_dtypes, name):
    bsz, s, d = u.shape
    tm = PROJ_ROWS
    n_tiles = bsz * (s // tm)
    cur, lag, _ = _tile_maps(s // tm, n_tiles)
    specs = [pl.BlockSpec((1, tm, d), cur), pl.BlockSpec((d, SEG), lambda t: (0, seg))]
    specs += [spec(lag) for spec in side_specs]
    return pl.pallas_call(
        functools.partial(_segment_kernel, finish, len(side), n_tiles),
        grid=(n_tiles + 1,),
        in_specs=specs,
        out_specs=[pl.BlockSpec((1, tm, SEG), lag) for _ in out_dtypes],
        out_shape=[jax.ShapeDtypeStruct((bsz, s, SEG), dt) for dt in out_dtypes],
        scratch_shapes=[pltpu.VMEM((2, tm, SEG), F32)],
        compiler_params=pltpu.CompilerParams(dimension_semantics=("arbitrary",),
                                             vmem_limit_bytes=VMEM_LIMIT),
        name=name,
    )(u, w_in, *side)


LN_CHUNK_ROWS = 256


def _first_segment_kernel(finish, n_tiles, x_ref, mod_ref, w_ref, u_ref, o_ref, y_ref):
    half = SEG // 2

    def matmul(slot, zero):
        m = mod_ref[0]
        for r0 in range(0, x_ref.shape[1], LN_CHUNK_ROWS):
            rows = slice(r0, r0 + LN_CHUNK_ROWS)
            u = (_ln_plain(x_ref[0, rows, :]) * (1.0 + m[1:2, :]) + m[0:1, :]).astype(BF16)
            u_ref[0, rows, :] = u
            y_ref[slot, rows, :half] = _dot(u, w_ref[:, :half])
            if zero is not None:
                u = u + jnp.tile(zero, (1, u.shape[1] // zero.shape[1]))
            y_ref[slot, rows, half:] = _dot(u, w_ref[:, half:])

    _lagged_steps(matmul, lambda slot: finish(y_ref[slot], (), (o_ref,)), n_tiles)


def _first_segment(x, mod, w_in, finish, name):
    bsz, s, d = x.shape
    tm = PROJ_ROWS
    n_tiles = bsz * (s // tm)
    cur, lag, _ = _tile_maps(s // tm, n_tiles)
    cur_batch = lambda t: (jnp.minimum(t, n_tiles - 1) // (s // tm), 0, 0)
    return pl.pallas_call(
        functools.partial(_first_segment_kernel, finish, n_tiles),
        grid=(n_tiles + 1,),
        in_specs=[pl.BlockSpec((1, tm, d), cur), pl.BlockSpec((1, 6, d), cur_batch),
                  pl.BlockSpec((d, SEG), lambda t: (0, 0))],
        out_specs=[pl.BlockSpec((1, tm, d), cur), pl.BlockSpec((1, tm, SEG), lag)],
        out_shape=[jax.ShapeDtypeStruct((bsz, s, d), BF16), jax.ShapeDtypeStruct((bsz, s, SEG), BF16)],
        scratch_shapes=[pltpu.VMEM((2, tm, SEG), F32)],
        compiler_params=pltpu.CompilerParams(dimension_semantics=("arbitrary",),
                                             vmem_limit_bytes=VMEM_LIMIT),
        name=name,
    )(x, mod, w_in)


def _inproj(x, mod, w_in, lb_logits, cos, sin):
    tm = PROJ_ROWS
    table = lambda lag: pl.BlockSpec((1, tm, DA_HEAD_DIM), lag)
    const = lambda lag: pl.BlockSpec(lb_logits.shape, lambda t: (0, 0))
    u, qs = _first_segment(x, mod, w_in, _finish_hg_q, "inproj_ln_hg_q")
    lf, ks = _segment(u, w_in, 1, _finish_hg_f, [lb_logits], [const], [F32, BF16], "inproj_hg_f")
    vs, = _segment(u, w_in, 2, _finish_cast, [], [], [BF16], "inproj_hg_i")
    gs, = _segment(u, w_in, 3, _finish_hg_g, [], [], [BF16], "inproj_hg_g")
    dq, = _segment(u, w_in, 4, _finish_da_q, [cos, sin], [table, table], [BF16], "inproj_da_q")
    dk, = _segment(u, w_in, 5, _finish_da_k, [cos, sin], [table, table], [BF16], "inproj_da_k")
    dv, = _segment(u, w_in, 6, _finish_cast, [], [], [BF16], "inproj_da_v")
    return qs, lf, ks, vs, gs, dq, dk, dv


HG_BLOCK = 256
HG_HEADS_PER_STEP = 8


def _hgrn_consts():
    L, C = HG_BLOCK, HG_CHUNK
    r = np.arange(L)[:, None]
    c = np.arange(L)[None, :]
    t_cum = (((r // C) == (c // C)) & (c <= r)).astype(np.float32)
    return jnp.asarray(t_cum, BF16), jnp.asarray(t_cum)


def _bcast_chunk_rows(x, row_in_chunk):
    L, C = HG_BLOCK, HG_CHUNK
    parts = [jnp.broadcast_to(x[c * C + row_in_chunk:c * C + row_in_chunk + 1, :], (C, x.shape[1]))
             for c in range(L // C)]
    return jnp.concatenate(parts, axis=0)


def _hgrn_head(q, lf, k, v, g, nw, tmat, maskf, st):
    L, C = HG_BLOCK, HG_CHUNK

    hi = lf.astype(BF16)
    r1 = lf - hi.astype(F32)
    mid = r1.astype(BF16)
    lo = (r1 - mid.astype(F32)).astype(BF16)
    gg = _dot(tmat, jnp.concatenate([hi, mid, lo], axis=1))
    yield
    g_cum = gg[:, :HG_DK] + gg[:, HG_DK:2 * HG_DK] + gg[:, 2 * HG_DK:]
    g_mid = _bcast_chunk_rows(g_cum, C // 2 - 1)
    g_last = _bcast_chunk_rows(g_cum, C - 1)

    qa = (q * jnp.exp(g_cum - g_mid)).astype(BF16)
    ka = (k * jnp.exp(g_mid - g_cum)).astype(BF16)
    qd = (q * jnp.exp(g_cum)).astype(BF16)
    kd = (k * jnp.exp(g_last - g_cum)).astype(BF16)
    dl = jnp.exp(g_last)

    a = _dot_nt(qa, ka)
    yield
    a = jnp.where(maskf > 0.0, a, 0.0).astype(BF16)
    o_intra = _dot(a, v)
    yield

    zeros = jnp.zeros((C, HG_DK), BF16)
    upds = []
    for p in range(L // (2 * C)):
        r0 = 2 * p * C
        rhs = jnp.concatenate(
            [jnp.concatenate([kd[r0:r0 + C], zeros], axis=1),
             jnp.concatenate([zeros, kd[r0 + C:r0 + 2 * C]], axis=1)], axis=0)
        u2 = _dot_tn(v[r0:r0 + 2 * C], rhs)
        upds += [u2[:, :HG_DK], u2[:, HG_DK:]]
    yield

    starts = []
    for c in range(L // C):
        starts.append(st.astype(BF16))
        st = st * dl[c * C:c * C + 1, :] + upds[c]

    outs = [o_intra[c * C:(c + 1) * C] + _dot_nt(qd[c * C:(c + 1) * C], starts[c])
            for c in range(L // C)]
    yield
    o = jnp.concatenate(outs, axis=0)
    ms = jnp.mean(o * o, -1, keepdims=True)
    y = o * lax.rsqrt(ms + RMS_EPS) * nw * g
    return y.astype(BF16), st


def _hgrn_kernel(q_ref, lf_ref, k_ref, v_ref, g_ref, nw_ref, tmat_ref, mask_ref, o_ref, st_ref):
    @pl.when(pl.program_id(2) == 0)
    def _():
        st_ref[...] = jnp.zeros_like(st_ref)

    lanes = [slice(h * HG_DK, (h + 1) * HG_DK) for h in range(HG_HEADS_PER_STEP)]
    heads = [_hgrn_head(q_ref[0, :, ln].astype(F32), lf_ref[0, :, ln], k_ref[0, :, ln].astype(F32),
                        v_ref[0, :, ln], g_ref[0, :, ln].astype(F32), nw_ref[...], tmat_ref[...],
                        mask_ref[...], st_ref[h]) for h, ln in enumerate(lanes)]
    for h, (y, st) in enumerate(_run_interleaved(heads)):
        o_ref[0, :, lanes[h]] = y
        st_ref[h] = st


def _hgrn(qs, lf, ks, vs, gs, norm_w):
    bsz, s, _ = qs.shape
    L = HG_BLOCK
    hp = HG_HEADS_PER_STEP
    tmat, mask = _hgrn_consts()
    blk = pl.BlockSpec((1, L, hp * HG_DK), lambda b, h, n: (b, n, h))
    const = lambda shape: pl.BlockSpec(shape, lambda b, h, n: (0, 0))
    return pl.pallas_call(
        _hgrn_kernel,
        grid=(bsz, HG_HEADS // hp, s // L),
        in_specs=[blk, blk, blk, blk, blk, const((1, HG_DK)), const((L, L)), const((L, L))],
        out_specs=blk,
        out_shape=jax.ShapeDtypeStruct((bsz, s, HG_WIDTH), BF16),
        scratch_shapes=[pltpu.VMEM((hp, HG_DK, HG_DK), F32)],
        compiler_params=pltpu.CompilerParams(
            dimension_semantics=("arbitrary", "arbitrary", "arbitrary"),
            vmem_limit_bytes=VMEM_LIMIT),
        name="hgrn",
    )(qs, lf, ks, vs, gs, norm_w.reshape(1, HG_DK), tmat, mask)


ATT_BLOCK = 512
ATT_COLS = 256
ATT_ONES = 16


def _attn_scores(k_ref, q_t, n, s_ref, slot):
    T, C, D = ATT_BLOCK, ATT_COLS, DA_HEAD_DIM
    kb = k_ref[0, pl.ds(pl.multiple_of(n * T, T), T), :]
    for t in range(2):
        for c in range(T // C):
            s_ref[slot, t * (T // C) + c] = _dot(kb[:, t * D:(t + 1) * D],
                                                 q_t[t * D:(t + 1) * D, c * C:(c + 1) * C])


def _attn_chain(s, v_t, acc_ref, m_ref, cols, masked):
    if masked:
        key = lax.broadcasted_iota(jnp.int32, s.shape, 0)
        qry = lax.broadcasted_iota(jnp.int32, s.shape, 1) + cols.start
        s = jnp.where(key <= qry, s, jnp.finfo(F32).min)
    m_prev = m_ref[:, cols]
    m_new = jnp.maximum(m_prev, jnp.max(s, axis=0, keepdims=True))
    alpha = jnp.exp2(m_prev - m_new)
    p = jnp.exp2(s - m_new)
    m_ref[:, cols] = m_new
    pv = _dot(v_t, p.astype(BF16))
    yield
    acc_ref[:, cols] = alpha * acc_ref[:, cols] + pv


def _attn_kernel(q_ref, k_ref, v_ref, lq1_ref, lk1_ref, lq2_ref, lk2_ref, sw_ref, o_ref,
                 vt_ref, s_ref, acc1_ref, acc2_ref, m1_ref, m2_ref):
    T = ATT_BLOCK
    C = ATT_COLS
    DV = 2 * DA_HEAD_DIM
    nq = q_ref.shape[1] // T

    for j in range(nq):
        vt_ref[j, :DV, :] = v_ref[0, j * T:(j + 1) * T, :].T
        vt_ref[j, DV:, :] = jnp.ones((ATT_ONES, T), BF16)

    lam = (jnp.exp(jnp.sum(lq1_ref[...] * lk1_ref[...], axis=-1, keepdims=True))
           - jnp.exp(jnp.sum(lq2_ref[...] * lk2_ref[...], axis=-1, keepdims=True))
           + LAMBDA_INIT)

    def queries_t(i):
        return q_ref[0, pl.ds(pl.multiple_of(i * T, T), T), :].T

    def query_block(i, carry):
        q_t = queries_t(i)
        acc1_ref[...] = jnp.zeros_like(acc1_ref)
        acc2_ref[...] = jnp.zeros_like(acc2_ref)
        m1_ref[...] = jnp.full_like(m1_ref, -jnp.inf)
        m2_ref[...] = jnp.full_like(m2_ref, -jnp.inf)

        def step(n, slot, masked, prefetch):
            if prefetch:
                _attn_scores(k_ref, q_t, n + 1, s_ref, 1 - slot)
            vt = vt_ref[n]
            chains = []
            for t, (acc_ref, m_ref) in enumerate(((acc1_ref, m1_ref), (acc2_ref, m2_ref))):
                for c in range(T // C):
                    cols = slice(c * C, (c + 1) * C)
                    nk = (c + 1) * C if masked else T
                    chains.append(_attn_chain(s_ref[slot, t * (T // C) + c, :nk, :], vt[:, :nk],
                                              acc_ref, m_ref, cols, masked))
            _run_interleaved(chains)

        def pair(t, carry):
            step(2 * t, 0, False, True)
            step(2 * t + 1, 1, False, True)
            return carry

        lax.fori_loop(0, i // 2, pair, 0)

        @pl.when(i % 2 == 0)
        def _():
            step(i, 0, True, False)

        @pl.when(i % 2 == 1)
        def _():
            step(i - 1, 0, False, True)
            step(i, 1, True, False)

        _attn_scores(k_ref, queries_t(jnp.minimum(i + 1, nq - 1)), 0, s_ref, 0)

        inv1 = 1.0 / acc1_ref[DV:DV + 1, :]
        inv2 = lam / acc2_ref[DV:DV + 1, :]
        o_t = acc1_ref[:DV, :] * inv1 - acc2_ref[:DV, :] * inv2
        o = o_t.T
        ms = jnp.mean(o * o, -1, keepdims=True)
        y = o * lax.rsqrt(ms + RMS_EPS) * sw_ref[...] * (1.0 - LAMBDA_INIT)
        o_ref[0, pl.ds(pl.multiple_of(i * T, T), T), :] = y.astype(BF16)
        return carry

    _attn_scores(k_ref, queries_t(0), 0, s_ref, 0)
    lax.fori_loop(0, nq, query_block, 0)


def _attn(q, k, v, lq1, lk1, lq2, lk2, subln_w):
    bsz, s, _ = q.shape
    T = ATT_BLOCK
    hw = 2 * DA_HEAD_DIM
    blk = pl.BlockSpec((1, s, hw), lambda b, h: (b, 0, h))
    vec = lambda n: pl.BlockSpec((1, n), lambda b, h: (0, 0))
    return pl.pallas_call(
        _attn_kernel,
        grid=(bsz, DA_HEADS),
        in_specs=[blk, blk, blk, vec(DA_HEAD_DIM), vec(DA_HEAD_DIM), vec(DA_HEAD_DIM),
                  vec(DA_HEAD_DIM), vec(hw)],
        out_specs=blk,
        out_shape=jax.ShapeDtypeStruct((bsz, s, DA_WIDTH), BF16),
        scratch_shapes=[pltpu.VMEM((s // T, hw + ATT_ONES, T), BF16),
                        pltpu.VMEM((2, 2 * (T // ATT_COLS), T, ATT_COLS), F32),
                        pltpu.VMEM((hw + ATT_ONES, T), F32), pltpu.VMEM((hw + ATT_ONES, T), F32),
                        pltpu.VMEM((1, T), F32), pltpu.VMEM((1, T), F32)],
        compiler_params=pltpu.CompilerParams(
            dimension_semantics=("arbitrary", "arbitrary"),
            vmem_limit_bytes=VMEM_LIMIT),
        name="diff_attn",
    )(q, k, v, lq1.reshape(1, -1), lk1.reshape(1, -1), lq2.reshape(1, -1), lk2.reshape(1, -1),
      subln_w.reshape(1, -1))


def _outproj_kernel(n_tiles, yh_ref, ya_ref, w_ref, x_ref, mod_ref, g_ref, b_ref, h_ref, u_ref, y_ref):
    half = w_ref.shape[1] // 2

    def matmul(slot, zero):
        yh = yh_ref[0]
        ya = ya_ref[0]
        y_ref[slot, :, :half] = _dot(yh, w_ref[:HG_WIDTH, :half]) + _dot(ya, w_ref[HG_WIDTH:, :half])
        if zero is not None:
            ya = ya + jnp.tile(zero, (1, ya.shape[1] // zero.shape[1]))
        y_ref[slot, :, half:] = _dot(yh, w_ref[:HG_WIDTH, half:]) + _dot(ya, w_ref[HG_WIDTH:, half:])

    def finish(slot):
        m = mod_ref[0]
        r = DEEPNORM_ALPHA * x_ref[0] + m[2:3, :] * y_ref[slot]
        h = _ln_plain(r) * g_ref[...] + b_ref[...]
        h_ref[0] = h
        u = (_ln_plain(h) * (1.0 + m[4:5, :]) + m[3:4, :]).astype(BF16)
        u_ref[0] = u
        return _zero_after(u)

    _lagged_steps(matmul, finish, n_tiles)


def _outproj(y_hg, y_da, w_out, x, mod, ln_g, ln_b):
    bsz, s, d = x.shape
    tm = PROJ_ROWS
    n_tiles = bsz * (s // tm)
    cur, lag, lag_batch = _tile_maps(s // tm, n_tiles)
    const = lambda shape: pl.BlockSpec(shape, lambda t: (0, 0))
    return pl.pallas_call(
        functools.partial(_outproj_kernel, n_tiles),
        grid=(n_tiles + 1,),
        in_specs=[pl.BlockSpec((1, tm, HG_WIDTH), cur),
                  pl.BlockSpec((1, tm, DA_WIDTH), cur),
                  const(w_out.shape),
                  pl.BlockSpec((1, tm, d), lag),
                  pl.BlockSpec((1, 6, d), lag_batch),
                  const((1, d)), const((1, d))],
        out_specs=[pl.BlockSpec((1, tm, d), lag), pl.BlockSpec((1, tm, d), lag)],
        out_shape=[jax.ShapeDtypeStruct((bsz, s, d), F32), jax.ShapeDtypeStruct((bsz, s, d), BF16)],
        scratch_shapes=[pltpu.VMEM((2, tm, d), F32)],
        compiler_params=pltpu.CompilerParams(dimension_semantics=("arbitrary",),
                                             vmem_limit_bytes=VMEM_LIMIT),
        name="outproj_ln1",
    )(y_hg, y_da, w_out, x, mod, ln_g.reshape(1, d), ln_b.reshape(1, d))


FFN_ROWS = 512
FFN_COLS = 512


def _ffn_kernel(u_ref, wg_ref, wu_ref, wd_ref, h_ref, mod_ref, g_ref, b_ref, o_ref, acc_ref):
    j = pl.program_id(2)

    @pl.when(j == 0)
    def _():
        acc_ref[...] = jnp.zeros_like(acc_ref)

    u = u_ref[0]
    a = _dot(u, wg_ref[...])
    b = _dot(u, wu_ref[...])
    z = (a * jax.nn.sigmoid(a) * b).astype(BF16)
    acc_ref[...] += _dot(z, wd_ref[...])

    @pl.when(j == pl.num_programs(2) - 1)
    def _():
        m = mod_ref[0]
        r = DEEPNORM_ALPHA * h_ref[0] + m[5:6, :] * acc_ref[...]
        o_ref[0] = _ln_plain(r) * g_ref[...] + b_ref[...]


def _ffn(u, w_gate, w_up, w_down, h, mod, ln_g, ln_b):
    bsz, s, d = h.shape
    f = w_gate.shape[1]
    tm, tf = FFN_ROWS, FFN_COLS
    row = lambda b, i, j: (b, i, 0)
    const3 = lambda shape: pl.BlockSpec(shape, lambda b, i, j: (0, 0))
    return pl.pallas_call(
        _ffn_kernel,
        grid=(bsz, s // tm, f // tf),
        in_specs=[pl.BlockSpec((1, tm, d), row),
                  pl.BlockSpec((d, tf), lambda b, i, j: (0, j)),
                  pl.BlockSpec((d, tf), lambda b, i, j: (0, j)),
                  pl.BlockSpec((tf, d), lambda b, i, j: (j, 0)),
                  pl.BlockSpec((1, tm, d), row),
                  pl.BlockSpec((1, 6, d), lambda b, i, j: (b, 0, 0)),
                  const3((1, d)), const3((1, d))],
        out_specs=pl.BlockSpec((1, tm, d), row),
        out_shape=jax.ShapeDtypeStruct((bsz, s, d), F32),
        scratch_shapes=[pltpu.VMEM((tm, d), F32)],
        compiler_params=pltpu.CompilerParams(
            dimension_semantics=("arbitrary", "arbitrary", "arbitrary"),
            vmem_limit_bytes=VMEM_LIMIT),
        name="ffn_ln2",
    )(u, w_gate.astype(BF16), w_up.astype(BF16), w_down.astype(BF16), h, mod,
      ln_g.reshape(1, d), ln_b.reshape(1, d))


def kernel(x, c, positions, w_ada, b_ada, w_in, lb_logits, hg_norm_w, lam_q1, lam_k1, lam_q2, lam_k2,
           subln_w, w_out, ln1_g, ln1_b, w_gate, w_up, w_down, ln2_g, ln2_b):
    assert w_ada.shape[0] == DEPTH == 1
    bsz, s, d = x.shape

    mod = _adaln(c, w_ada[0], b_ada[0]).reshape(bsz, 6, d)
    cos, sin = _rope_tables(positions)

    qs, lf, ks, vs, gs, dq, dk, dv = _inproj(x, mod, w_in[0].astype(BF16), lb_logits, cos, sin)
    y_hg = _hgrn(qs, lf, ks, vs, gs, hg_norm_w[0])
    y_da = _attn(dq, dk, dv, lam_q1[0], lam_k1[0], lam_q2[0], lam_k2[0], subln_w[0])
    h1, u2 = _outproj(y_hg, y_da, w_out[0].astype(BF16), x, mod, ln1_g[0], ln1_b[0])
    return _ffn(u2, w_gate[0], w_up[0], w_down[0], h1, mod, ln2_g[0], ln2_b[0])
```

```python
import functools
import math

import numpy as np
import jax
import jax.numpy as jnp
from jax import lax
from jax.experimental import pallas as pl
from jax.experimental.pallas import tpu as pltpu

D_MODEL = 2048
HG_WIDTH = 1024
DA_WIDTH = 1024
HG_HEADS = 8
HG_DK = 128
HG_CHUNK = 32
DA_HEADS = 4
DA_HEAD_DIM = 128
ROPE_THETA = 10000.0
D_FF = 5632
DEPTH = 1
DEEPNORM_ALPHA = (2.0 * DEPTH) ** 0.25
LN_EPS = 1e-5
RMS_EPS = 1e-6
LAMBDA_INIT = 0.8 - 0.6 * math.exp(-0.3 * 0)
SEG = 1024
N_SEG = 7

VMEM_LIMIT = 56 * 1024 * 1024

F32 = jnp.float32
BF16 = jnp.bfloat16


def _dot(a, b):
    return jnp.dot(a, b, preferred_element_type=F32)


def _dot_nt(a, b):
    return lax.dot_general(a, b, (((1,), (1,)), ((), ())), preferred_element_type=F32)


def _dot_tn(a, b):
    return lax.dot_general(a, b, (((0,), (0,)), ((), ())), preferred_element_type=F32)


def _ln_plain(x):
    mu = jnp.mean(x, -1, keepdims=True)
    xc = x - mu
    var = jnp.mean(xc * xc, -1, keepdims=True)
    return xc * lax.rsqrt(var + LN_EPS)


def _run_interleaved(chains):
    pending = list(chains)
    results = {}
    while pending:
        for chain in list(pending):
            try:
                next(chain)
            except StopIteration as done:
                results[id(chain)] = done.value
                pending.remove(chain)
    return [results[id(chain)] for chain in chains]


def _adaln_kernel(c_ref, w_ref, b_ref, o_ref):
    cond = c_ref[...]
    cond = cond * jax.nn.sigmoid(cond)
    o_ref[...] = _dot(cond, w_ref[...]) + b_ref[...]


def _adaln(c, w, b):
    bsz, d = c.shape
    n = w.shape[1]
    tn = 1024
    return pl.pallas_call(
        _adaln_kernel,
        grid=(n // tn,),
        in_specs=[pl.BlockSpec((bsz, d), lambda j: (0, 0)),
                  pl.BlockSpec((d, tn), lambda j: (0, j)),
                  pl.BlockSpec((1, tn), lambda j: (0, j))],
        out_specs=pl.BlockSpec((bsz, tn), lambda j: (0, j)),
        out_shape=jax.ShapeDtypeStruct((bsz, n), F32),
        compiler_params=pltpu.CompilerParams(dimension_semantics=("arbitrary",),
                                             vmem_limit_bytes=VMEM_LIMIT),
        name="adaln",
    )(c, w, b.reshape(1, n))


def _rope_kernel(pos_ref, inv_ref, sign_ref, cos_ref, sin_ref):
    hs = pos_ref.shape[1] // 2
    lo = lax.broadcasted_iota(jnp.int32, (hs, DA_HEAD_DIM), 1) < DA_HEAD_DIM // 2
    ang = jnp.where(lo, pos_ref[0, :hs, :], pos_ref[0, hs:, :]) * inv_ref[...]
    for fn, ref, scale in ((jnp.cos, cos_ref, None), (jnp.sin, sin_ref, sign_ref[...])):
        val = fn(ang)
        swapped = pltpu.roll(val, DA_HEAD_DIM // 2, 1)
        top = jnp.where(lo, val, swapped)
        bottom = jnp.where(lo, swapped, val)
        ref[0, :hs, :] = top if scale is None else top * scale
        ref[0, hs:, :] = bottom if scale is None else bottom * scale


def _rope_tables(positions):
    bsz, s = positions.shape
    ts = 2048
    half = DA_HEAD_DIM // 2
    inv = 1.0 / (ROPE_THETA ** (jnp.arange(0, DA_HEAD_DIM, 2, dtype=F32) / DA_HEAD_DIM))
    inv2 = jnp.concatenate([inv, inv]).reshape(1, DA_HEAD_DIM)
    sign = jnp.concatenate([-jnp.ones((half,), F32), jnp.ones((half,), F32)]).reshape(1, DA_HEAD_DIM)
    pos = positions.astype(F32).reshape(bsz, s, 1)
    tab = jax.ShapeDtypeStruct((bsz, s, DA_HEAD_DIM), F32)
    return pl.pallas_call(
        _rope_kernel,
        grid=(bsz, s // ts),
        in_specs=[pl.BlockSpec((1, ts, 1), lambda b, i: (b, i, 0)),
                  pl.BlockSpec((1, DA_HEAD_DIM), lambda b, i: (0, 0)),
                  pl.BlockSpec((1, DA_HEAD_DIM), lambda b, i: (0, 0))],
        out_specs=[pl.BlockSpec((1, ts, DA_HEAD_DIM), lambda b, i: (b, i, 0)),
                   pl.BlockSpec((1, ts, DA_HEAD_DIM), lambda b, i: (b, i, 0))],
        out_shape=[tab, tab],
        compiler_params=pltpu.CompilerParams(dimension_semantics=("arbitrary", "arbitrary")),
        name="rope_tables",
    )(pos, inv2, sign)


PROJ_ROWS = 512


def _zero_after(x):
    bits = pltpu.bitcast(x, jnp.int32)
    acc = bits[0:8, :]
    for r in range(8, bits.shape[0], 8):
        acc = acc | bits[r:r + 8, :]
    word = acc[:, 0:128]
    for c in range(128, acc.shape[1], 128):
        word = word | acc[:, c:c + 128]
    word = lax.shift_right_logical(lax.shift_right_logical(word, 16), 16)
    return pltpu.bitcast(word, F32)[0:1, :].astype(BF16)


def _lagged_steps(matmul, finish, n_tiles):
    t = pl.program_id(0)

    @pl.when(t == 0)
    def _():
        matmul(0, None)

    for parity in (0, 1):
        @pl.when(jnp.logical_and(jnp.logical_and(t > 0, t < n_tiles), t % 2 == parity))
        def _():
            matmul(parity, finish(1 - parity))

    @pl.when(t == n_tiles)
    def _():
        finish((n_tiles - 1) % 2)


def _tile_maps(tiles_per_batch, n_tiles):
    def split(k):
        return k // tiles_per_batch, k % tiles_per_batch
    cur = lambda t: (*split(jnp.minimum(t, n_tiles - 1)), 0)
    lag = lambda t: (*split(jnp.maximum(t - 1, 0)), 0)
    lag_batch = lambda t: (jnp.maximum(t - 1, 0) // tiles_per_batch, 0, 0)
    return cur, lag, lag_batch


def _rope_apply(acc, cos, sin, scale):
    outs = []
    for h in range(SEG // DA_HEAD_DIM):
        xh = acc[:, h * DA_HEAD_DIM:(h + 1) * DA_HEAD_DIM]
        rot = pltpu.roll(xh, DA_HEAD_DIM // 2, 1)
        y = xh * cos + rot * sin
        if scale is not None:
            y = y * scale
        outs.append(y.astype(BF16))
    return jnp.concatenate(outs, axis=-1)


def _sigmoid_tanh(x):
    return 0.5 * jnp.tanh(0.5 * x) + 0.5


def _finish_hg_q(y, side, outs):
    q = (y * _sigmoid_tanh(y) * (HG_DK ** -0.5)).astype(BF16)
    outs[0][0] = q
    return _zero_after(q)


def _finish_hg_f(y, side, outs):
    lbl = side[0][...]
    e = jnp.exp(lbl - jnp.max(lbl, axis=0, keepdims=True))
    lb = e[0:1, :] / jnp.sum(e, axis=0, keepdims=True)
    f = lb + (1.0 - lb) * jax.nn.sigmoid(y)
    log_f = jnp.log(f)
    key = (1.0 - f).astype(BF16)
    outs[0][0] = log_f
    outs[1][0] = key
    return _zero_after(log_f) + _zero_after(key)


def _finish_cast(y, side, outs):
    outs[0][0] = y.astype(BF16)
    return None


def _finish_hg_g(y, side, outs):
    g = _sigmoid_tanh(y).astype(BF16)
    outs[0][0] = g
    return _zero_after(g)


def _finish_da_q(y, side, outs):
    q = _rope_apply(y, side[0][0], side[1][0], DA_HEAD_DIM ** -0.5 * math.log2(math.e))
    outs[0][0] = q
    return _zero_after(q)


def _finish_da_k(y, side, outs):
    k = _rope_apply(y, side[0][0], side[1][0], None)
    outs[0][0] = k
    return _zero_after(k)


def _segment_kernel(finish, n_side, n_tiles, u_ref, w_ref, *refs):
    side, outs, y_ref = refs[:n_side], refs[n_side:-1], refs[-1]

    half = SEG // 2

    def matmul(slot, zero):
        u = u_ref[0]
        y_ref[slot, :, :half] = _dot(u, w_ref[:, :half])
        if zero is not None:
            u = u + jnp.tile(zero, (1, u.shape[1] // zero.shape[1]))
        y_ref[slot, :, half:] = _dot(u, w_ref[:, half:])

    _lagged_steps(matmul, lambda slot: finish(y_ref[slot], side, outs), n_tiles)


def _segment(u, w_in, seg, finish, side, side_specs, out_dtypes, name):
    bsz, s, d = u.shape
    tm = PROJ_ROWS
    n_tiles = bsz * (s // tm)
    cur, lag, _ = _tile_maps(s // tm, n_tiles)
    specs = [pl.BlockSpec((1, tm, d), cur), pl.BlockSpec((d, SEG), lambda t: (0, seg))]
    specs += [spec(lag) for spec in side_specs]
    return pl.pallas_call(
        functools.partial(_segment_kernel, finish, len(side), n_tiles),
        grid=(n_tiles + 1,),
        in_specs=specs,
        out_specs=[pl.BlockSpec((1, tm, SEG), lag) for _ in out_dtypes],
        out_shape=[jax.ShapeDtypeStruct((bsz, s, SEG), dt) for dt in out_dtypes],
        scratch_shapes=[pltpu.VMEM((2, tm, SEG), F32)],
        compiler_params=pltpu.CompilerParams(dimension_semantics=("arbitrary",),
                                             vmem_limit_bytes=VMEM_LIMIT),
        name=name,
    )(u, w_in, *side)


LN_CHUNK_ROWS = 256


def _first_segment_kernel(finish, n_tiles, x_ref, mod_ref, w_ref, u_ref, o_ref, y_ref):
    half = SEG // 2

    def matmul(slot, zero):
        m = mod_ref[0]
        for r0 in range(0, x_ref.shape[1], LN_CHUNK_ROWS):
            rows = slice(r0, r0 + LN_CHUNK_ROWS)
            u = (_ln_plain(x_ref[0, rows, :]) * (1.0 + m[1:2, :]) + m[0:1, :]).astype(BF16)
            u_ref[0, rows, :] = u
            y_ref[slot, rows, :half] = _dot(u, w_ref[:, :half])
            if zero is not None:
                u = u + jnp.tile(zero, (1, u.shape[1] // zero.shape[1]))
            y_ref[slot, rows, half:] = _dot(u, w_ref[:, half:])

    _lagged_steps(matmul, lambda slot: finish(y_ref[slot], (), (o_ref,)), n_tiles)


def _first_segment(x, mod, w_in, finish, name):
    bsz, s, d = x.shape
    tm = PROJ_ROWS
    n_tiles = bsz * (s // tm)
    cur, lag, _ = _tile_maps(s // tm, n_tiles)
    cur_batch = lambda t: (jnp.minimum(t, n_tiles - 1) // (s // tm), 0, 0)
    return pl.pallas_call(
        functools.partial(_first_segment_kernel, finish, n_tiles),
        grid=(n_tiles + 1,),
        in_specs=[pl.BlockSpec((1, tm, d), cur), pl.BlockSpec((1, 6, d), cur_batch),
                  pl.BlockSpec((d, SEG), lambda t: (0, 0))],
        out_specs=[pl.BlockSpec((1, tm, d), cur), pl.BlockSpec((1, tm, SEG), lag)],
        out_shape=[jax.ShapeDtypeStruct((bsz, s, d), BF16), jax.ShapeDtypeStruct((bsz, s, SEG), BF16)],
        scratch_shapes=[pltpu.VMEM((2, tm, SEG), F32)],
        compiler_params=pltpu.CompilerParams(dimension_semantics=("arbitrary",),
                                             vmem_limit_bytes=VMEM_LIMIT),
        name=name,
    )(x, mod, w_in)


def _inproj(x, mod, w_in, lb_logits, cos, sin):
    tm = PROJ_ROWS
    table = lambda lag: pl.BlockSpec((1, tm, DA_HEAD_DIM), lag)
    const = lambda lag: pl.BlockSpec(lb_logits.shape, lambda t: (0, 0))
    u, qs = _first_segment(x, mod, w_in, _finish_hg_q, "inproj_ln_hg_q")
    lf, ks = _segment(u, w_in, 1, _finish_hg_f, [lb_logits], [const], [F32, BF16], "inproj_hg_f")
    vs, = _segment(u, w_in, 2, _finish_cast, [], [], [BF16], "inproj_hg_i")
    gs, = _segment(u, w_in, 3, _finish_hg_g, [], [], [BF16], "inproj_hg_g")
    dq, = _segment(u, w_in, 4, _finish_da_q, [cos, sin], [table, table], [BF16], "inproj_da_q")
    dk, = _segment(u, w_in, 5, _finish_da_k, [cos, sin], [table, table], [BF16], "inproj_da_k")
    dv, = _segment(u, w_in, 6, _finish_cast, [], [], [BF16], "inproj_da_v")
    return qs, lf, ks, vs, gs, dq, dk, dv


HG_BLOCK = 256
HG_HEADS_PER_STEP = 8


def _hgrn_consts():
    L, C = HG_BLOCK, HG_CHUNK
    r = np.arange(L)[:, None]
    c = np.arange(L)[None, :]
    t_cum = (((r // C) == (c // C)) & (c <= r)).astype(np.float32)
    return jnp.asarray(t_cum, BF16), jnp.asarray(t_cum)


def _bcast_chunk_rows(x, row_in_chunk):
    L, C = HG_BLOCK, HG_CHUNK
    parts = [jnp.broadcast_to(x[c * C + row_in_chunk:c * C + row_in_chunk + 1, :], (C, x.shape[1]))
             for c in range(L // C)]
    return jnp.concatenate(parts, axis=0)


def _hgrn_head(q, lf, k, v, g, nw, tmat, maskf, st):
    L, C = HG_BLOCK, HG_CHUNK

    hi = lf.astype(BF16)
    r1 = lf - hi.astype(F32)
    mid = r1.astype(BF16)
    lo = (r1 - mid.astype(F32)).astype(BF16)
    gg = _dot(tmat, jnp.concatenate([hi, mid, lo], axis=1))
    yield
    g_cum = gg[:, :HG_DK] + gg[:, HG_DK:2 * HG_DK] + gg[:, 2 * HG_DK:]
    g_mid = _bcast_chunk_rows(g_cum, C // 2 - 1)
    g_last = _bcast_chunk_rows(g_cum, C - 1)

    qa = (q * jnp.exp(g_cum - g_mid)).astype(BF16)
    ka = (k * jnp.exp(g_mid - g_cum)).astype(BF16)
    qd = (q * jnp.exp(g_cum)).astype(BF16)
    kd = (k * jnp.exp(g_last - g_cum)).astype(BF16)
    dl = jnp.exp(g_last)

    a = _dot_nt(qa, ka)
    yield
    a = jnp.where(maskf > 0.0, a, 0.0).astype(BF16)
    o_intra = _dot(a, v)
    yield

    zeros = jnp.zeros((C, HG_DK), BF16)
    upds = []
    for p in range(L // (2 * C)):
        r0 = 2 * p * C
        rhs = jnp.concatenate(
            [jnp.concatenate([kd[r0:r0 + C], zeros], axis=1),
             jnp.concatenate([zeros, kd[r0 + C:r0 + 2 * C]], axis=1)], axis=0)
        u2 = _dot_tn(v[r0:r0 + 2 * C], rhs)
        upds += [u2[:, :HG_DK], u2[:, HG_DK:]]
    yield

    starts = []
    for c in range(L // C):
        starts.append(st.astype(BF16))
        st = st * dl[c * C:c * C + 1, :] + upds[c]

    outs = [o_intra[c * C:(c + 1) * C] + _dot_nt(qd[c * C:(c + 1) * C], starts[c])
            for c in range(L // C)]
    yield
    o = jnp.concatenate(outs, axis=0)
    ms = jnp.mean(o * o, -1, keepdims=True)
    y = o * lax.rsqrt(ms + RMS_EPS) * nw * g
    return y.astype(BF16), st


def _hgrn_kernel(q_ref, lf_ref, k_ref, v_ref, g_ref, nw_ref, tmat_ref, mask_ref, o_ref, st_ref):
    @pl.when(pl.program_id(2) == 0)
    def _():
        st_ref[...] = jnp.zeros_like(st_ref)

    lanes = [slice(h * HG_DK, (h + 1) * HG_DK) for h in range(HG_HEADS_PER_STEP)]
    heads = [_hgrn_head(q_ref[0, :, ln].astype(F32), lf_ref[0, :, ln], k_ref[0, :, ln].astype(F32),
                        v_ref[0, :, ln], g_ref[0, :, ln].astype(F32), nw_ref[...], tmat_ref[...],
                        mask_ref[...], st_ref[h]) for h, ln in enumerate(lanes)]
    for h, (y, st) in enumerate(_run_interleaved(heads)):
        o_ref[0, :, lanes[h]] = y
        st_ref[h] = st


def _hgrn(qs, lf, ks, vs, gs, norm_w):
    bsz, s, _ = qs.shape
    L = HG_BLOCK
    hp = HG_HEADS_PER_STEP
    tmat, mask = _hgrn_consts()
    blk = pl.BlockSpec((1, L, hp * HG_DK), lambda b, h, n: (b, n, h))
    const = lambda shape: pl.BlockSpec(shape, lambda b, h, n: (0, 0))
    return pl.pallas_call(
        _hgrn_kernel,
        grid=(bsz, HG_HEADS // hp, s // L),
        in_specs=[blk, blk, blk, blk, blk, const((1, HG_DK)), const((L, L)), const((L, L))],
        out_specs=blk,
        out_shape=jax.ShapeDtypeStruct((bsz, s, HG_WIDTH), BF16),
        scratch_shapes=[pltpu.VMEM((hp, HG_DK, HG_DK), F32)],
        compiler_params=pltpu.CompilerParams(
            dimension_semantics=("arbitrary", "arbitrary", "arbitrary"),
            vmem_limit_bytes=VMEM_LIMIT),
        name="hgrn",
    )(qs, lf, ks, vs, gs, norm_w.reshape(1, HG_DK), tmat, mask)


ATT_BLOCK = 512
ATT_COLS = 256
ATT_ONES = 16


def _attn_scores(k_ref, q_t, n, s_ref, slot):
    T, C, D = ATT_BLOCK, ATT_COLS, DA_HEAD_DIM
    kb = k_ref[0, pl.ds(pl.multiple_of(n * T, T), T), :]
    for t in range(2):
        for c in range(T // C):
            s_ref[slot, t * (T // C) + c] = _dot(kb[:, t * D:(t + 1) * D],
                                                 q_t[t * D:(t + 1) * D, c * C:(c + 1) * C])


def _attn_chain(s, v_t, acc_ref, m_ref, cols, masked):
    if masked:
        key = lax.broadcasted_iota(jnp.int32, s.shape, 0)
        qry = lax.broadcasted_iota(jnp.int32, s.shape, 1) + cols.start
        s = jnp.where(key <= qry, s, jnp.finfo(F32).min)
    m_prev = m_ref[:, cols]
    m_new = jnp.maximum(m_prev, jnp.max(s, axis=0, keepdims=True))
    alpha = jnp.exp2(m_prev - m_new)
    p = jnp.exp2(s - m_new)
    m_ref[:, cols] = m_new
    pv = _dot(v_t, p.astype(BF16))
    yield
    acc_ref[:, cols] = alpha * acc_ref[:, cols] + pv


def _attn_kernel(q_ref, k_ref, v_ref, lq1_ref, lk1_ref, lq2_ref, lk2_ref, sw_ref, o_ref,
                 vt_ref, s_ref, acc1_ref, acc2_ref, m1_ref, m2_ref):
    T = ATT_BLOCK
    C = ATT_COLS
    DV = 2 * DA_HEAD_DIM
    nq = q_ref.shape[1] // T

    for j in range(nq):
        vt_ref[j, :DV, :] = v_ref[0, j * T:(j + 1) * T, :].T
        vt_ref[j, DV:, :] = jnp.ones((ATT_ONES, T), BF16)

    lam = (jnp.exp(jnp.sum(lq1_ref[...] * lk1_ref[...], axis=-1, keepdims=True))
           - jnp.exp(jnp.sum(lq2_ref[...] * lk2_ref[...], axis=-1, keepdims=True))
           + LAMBDA_INIT)

    def queries_t(i):
        return q_ref[0, pl.ds(pl.multiple_of(i * T, T), T), :].T

    def query_block(i, carry):
        q_t = queries_t(i)
        acc1_ref[...] = jnp.zeros_like(acc1_ref)
        acc2_ref[...] = jnp.zeros_like(acc2_ref)
        m1_ref[...] = jnp.full_like(m1_ref, -jnp.inf)
        m2_ref[...] = jnp.full_like(m2_ref, -jnp.inf)

        def step(n, slot, masked, prefetch):
            if prefetch:
                _attn_scores(k_ref, q_t, n + 1, s_ref, 1 - slot)
            vt = vt_ref[n]
            chains = []
            for t, (acc_ref, m_ref) in enumerate(((acc1_ref, m1_ref), (acc2_ref, m2_ref))):
                for c in range(T // C):
                    cols = slice(c * C, (c + 1) * C)
                    nk = (c + 1) * C if masked else T
                    chains.append(_attn_chain(s_ref[slot, t * (T // C) + c, :nk, :], vt[:, :nk],
                                              acc_ref, m_ref, cols, masked))
            _run_interleaved(chains)

        def pair(t, carry):
            step(2 * t, 0, False, True)
            step(2 * t + 1, 1, False, True)
            return carry

        lax.fori_loop(0, i // 2, pair, 0)

        @pl.when(i % 2 == 0)
        def _():
            step(i, 0, True, False)

        @pl.when(i % 2 == 1)
        def _():
            step(i - 1, 0, False, True)
            step(i, 1, True, False)

        _attn_scores(k_ref, queries_t(jnp.minimum(i + 1, nq - 1)), 0, s_ref, 0)

        inv1 = 1.0 / acc1_ref[DV:DV + 1, :]
        inv2 = lam / acc2_ref[DV:DV + 1, :]
        o_t = acc1_ref[:DV, :] * inv1 - acc2_ref[:DV, :] * inv2
        o = o_t.T
        ms = jnp.mean(o * o, -1, keepdims=True)
        y = o * lax.rsqrt(ms + RMS_EPS) * sw_ref[...] * (1.0 - LAMBDA_INIT)
        o_ref[0, pl.ds(pl.multiple_of(i * T, T), T), :] = y.astype(BF16)
        return carry

    _attn_scores(k_ref, queries_t(0), 0, s_ref, 0)
    lax.fori_loop(0, nq, query_block, 0)


def _attn(q, k, v, lq1, lk1, lq2, lk2, subln_w):
    bsz, s, _ = q.shape
    T = ATT_BLOCK
    hw = 2 * DA_HEAD_DIM
    blk = pl.BlockSpec((1, s, hw), lambda b, h: (b, 0, h))
    vec = lambda n: pl.BlockSpec((1, n), lambda b, h: (0, 0))
    return pl.pallas_call(
        _attn_kernel,
        grid=(bsz, DA_HEADS),
        in_specs=[blk, blk, blk, vec(DA_HEAD_DIM), vec(DA_HEAD_DIM), vec(DA_HEAD_DIM),
                  vec(DA_HEAD_DIM), vec(hw)],
        out_specs=blk,
        out_shape=jax.ShapeDtypeStruct((bsz, s, DA_WIDTH), BF16),
        scratch_shapes=[pltpu.VMEM((s // T, hw + ATT_ONES, T), BF16),
                        pltpu.VMEM((2, 2 * (T // ATT_COLS), T, ATT_COLS), F32),
                        pltpu.VMEM((hw + ATT_ONES, T), F32), pltpu.VMEM((hw + ATT_ONES, T), F32),
                        pltpu.VMEM((1, T), F32), pltpu.VMEM((1, T), F32)],
        compiler_params=pltpu.CompilerParams(
            dimension_semantics=("arbitrary", "arbitrary"),
            vmem_limit_bytes=VMEM_LIMIT),
        name="diff_attn",
    )(q, k, v, lq1.reshape(1, -1), lk1.reshape(1, -1), lq2.reshape(1, -1), lk2.reshape(1, -1),
      subln_w.reshape(1, -1))


def _outproj_kernel(n_tiles, yh_ref, ya_ref, w_ref, x_ref, mod_ref, g_ref, b_ref, h_ref, u_ref, y_ref):
    half = w_ref.shape[1] // 2

    def matmul(slot, zero):
        yh = yh_ref[0]
        ya = ya_ref[0]
        y_ref[slot, :, :half] = _dot(yh, w_ref[:HG_WIDTH, :half]) + _dot(ya, w_ref[HG_WIDTH:, :half])
        if zero is not None:
            ya = ya + jnp.tile(zero, (1, ya.shape[1] // zero.shape[1]))
        y_ref[slot, :, half:] = _dot(yh, w_ref[:HG_WIDTH, half:]) + _dot(ya, w_ref[HG_WIDTH:, half:])

    def finish(slot):
        m = mod_ref[0]
        r = DEEPNORM_ALPHA * x_ref[0] + m[2:3, :] * y_ref[slot]
        h = _ln_plain(r) * g_ref[...] + b_ref[...]
        h_ref[0] = h
        u = (_ln_plain(h) * (1.0 + m[4:5, :]) + m[3:4, :]).astype(BF16)
        u_ref[0] = u
        return _zero_after(u)

    _lagged_steps(matmul, finish, n_tiles)


def _outproj(y_hg, y_da, w_out, x, mod, ln_g, ln_b):
    bsz, s, d = x.shape
    tm = PROJ_ROWS
    n_tiles = bsz * (s // tm)
    cur, lag, lag_batch = _tile_maps(s // tm, n_tiles)
    const = lambda shape: pl.BlockSpec(shape, lambda t: (0, 0))
    return pl.pallas_call(
        functools.partial(_outproj_kernel, n_tiles),
        grid=(n_tiles + 1,),
        in_specs=[pl.BlockSpec((1, tm, HG_WIDTH), cur),
                  pl.BlockSpec((1, tm, DA_WIDTH), cur),
                  const(w_out.shape),
                  pl.BlockSpec((1, tm, d), lag),
                  pl.BlockSpec((1, 6, d), lag_batch),
                  const((1, d)), const((1, d))],
        out_specs=[pl.BlockSpec((1, tm, d), lag), pl.BlockSpec((1, tm, d), lag)],
        out_shape=[jax.ShapeDtypeStruct((bsz, s, d), F32), jax.ShapeDtypeStruct((bsz, s, d), BF16)],
        scratch_shapes=[pltpu.VMEM((2, tm, d), F32)],
        compiler_params=pltpu.CompilerParams(dimension_semantics=("arbitrary",),
                                             vmem_limit_bytes=VMEM_LIMIT),
        name="outproj_ln1",
    )(y_hg, y_da, w_out, x, mod, ln_g.reshape(1, d), ln_b.reshape(1, d))


FFN_ROWS = 512
FFN_COLS = 512


def _ffn_kernel(n_tiles, n_cols, u_ref, wg_ref, wu_ref, wd_ref, h_ref, mod_ref, g_ref, b_ref, o_ref, acc_ref):
    t = pl.program_id(0)
    j = t % n_cols
    last = n_tiles * n_cols

    def finish():
        m = mod_ref[0]
        r = DEEPNORM_ALPHA * h_ref[0] + m[5:6, :] * acc_ref[...]
        o = _ln_plain(r) * g_ref[...] + b_ref[...]
        o_ref[0] = o
        return _zero_after(o)

    def multiply(zero, first_block):
        u = u_ref[0]
        a = _dot(u, wg_ref[...])
        b = _dot(u, wu_ref[...])
        z = (a * jax.nn.sigmoid(a) * b).astype(BF16)
        if zero is not None:
            z = z + jnp.tile(zero, (1, z.shape[1] // zero.shape[1]))
        y = _dot(z, wd_ref[...])
        if first_block:
            acc_ref[...] = y
        else:
            acc_ref[...] += y

    @pl.when(t == 0)
    def _():
        multiply(None, True)

    @pl.when(jnp.logical_and(j == 0, jnp.logical_and(t > 0, t < last)))
    def _():
        multiply(finish(), True)

    @pl.when(jnp.logical_and(j > 0, t < last))
    def _():
        multiply(None, False)

    @pl.when(t == last)
    def _():
        finish()


def _ffn(u, w_gate, w_up, w_down, h, mod, ln_g, ln_b):
    bsz, s, d = h.shape
    f = w_gate.shape[1]
    tm, tf = FFN_ROWS, FFN_COLS
    n_cols = f // tf
    tiles_per_batch = s // tm
    n_tiles = bsz * tiles_per_batch
    last = n_tiles * n_cols

    def split(k):
        return k // tiles_per_batch, k % tiles_per_batch
    cur = lambda t: (*split(jnp.minimum(t // n_cols, n_tiles - 1)), 0)
    lag = lambda t: (*split(jnp.maximum(t // n_cols - 1, 0)), 0)
    lag_batch = lambda t: (jnp.maximum(t // n_cols - 1, 0) // tiles_per_batch, 0, 0)
    col = lambda t: jnp.where(t == last, n_cols - 1, t % n_cols)
    const = lambda shape: pl.BlockSpec(shape, lambda t: (0, 0))
    return pl.pallas_call(
        functools.partial(_ffn_kernel, n_tiles, n_cols),
        grid=(last + 1,),
        in_specs=[pl.BlockSpec((1, tm, d), cur),
                  pl.BlockSpec((d, tf), lambda t: (0, col(t))),
                  pl.BlockSpec((d, tf), lambda t: (0, col(t))),
                  pl.BlockSpec((tf, d), lambda t: (col(t), 0)),
                  pl.BlockSpec((1, tm, d), lag),
                  pl.BlockSpec((1, 6, d), lag_batch),
                  const((1, d)), const((1, d))],
        out_specs=pl.BlockSpec((1, tm, d), lag),
        out_shape=jax.ShapeDtypeStruct((bsz, s, d), F32),
        scratch_shapes=[pltpu.VMEM((tm, d), F32)],
        compiler_params=pltpu.CompilerParams(dimension_semantics=("arbitrary",),
                                             vmem_limit_bytes=VMEM_LIMIT),
        name="ffn_ln2",
    )(u, w_gate.astype(BF16), w_up.astype(BF16), w_down.astype(BF16), h, mod,
      ln_g.reshape(1, d), ln_b.reshape(1, d))


def kernel(x, c, positions, w_ada, b_ada, w_in, lb_logits, hg_norm_w, lam_q1, lam_k1, lam_q2, lam_k2,
           subln_w, w_out, ln1_g, ln1_b, w_gate, w_up, w_down, ln2_g, ln2_b):
    assert w_ada.shape[0] == DEPTH == 1
    bsz, s, d = x.shape

    mod = _adaln(c, w_ada[0], b_ada[0]).reshape(bsz, 6, d)
    cos, sin = _rope_tables(positions)

    qs, lf, ks, vs, gs, dq, dk, dv = _inproj(x, mod, w_in[0].astype(BF16), lb_logits, cos, sin)
    y_hg = _hgrn(qs, lf, ks, vs, gs, hg_norm_w[0])
    y_da = _attn(dq, dk, dv, lam_q1[0], lam_k1[0], lam_q2[0], lam_k2[0], subln_w[0])
    h1, u2 = _outproj(y_hg, y_da, w_out[0].astype(BF16), x, mod, ln1_g[0], ln1_b[0])
    return _ffn(u2, w_gate[0], w_up[0], w_down[0], h1, mod, ln2_g[0], ln2_b[0])
```

```python
import functools
import math

import numpy as np
import jax
import jax.numpy as jnp
from jax import lax
from jax.experimental import pallas as pl
from jax.experimental.pallas import tpu as pltpu

D_MODEL = 2048
HG_WIDTH = 1024
DA_WIDTH = 1024
HG_HEADS = 8
HG_DK = 128
HG_CHUNK = 32
DA_HEADS = 4
DA_HEAD_DIM = 128
ROPE_THETA = 10000.0
D_FF = 5632
DEPTH = 1
DEEPNORM_ALPHA = (2.0 * DEPTH) ** 0.25
LN_EPS = 1e-5
RMS_EPS = 1e-6
LAMBDA_INIT = 0.8 - 0.6 * math.exp(-0.3 * 0)
SEG = 1024
N_SEG = 7

VMEM_LIMIT = 56 * 1024 * 1024

F32 = jnp.float32
BF16 = jnp.bfloat16


def _dot(a, b):
    return jnp.dot(a, b, preferred_element_type=F32)


def _dot_nt(a, b):
    return lax.dot_general(a, b, (((1,), (1,)), ((), ())), preferred_element_type=F32)


def _dot_tn(a, b):
    return lax.dot_general(a, b, (((0,), (0,)), ((), ())), preferred_element_type=F32)


def _ln_plain(x):
    mu = jnp.mean(x, -1, keepdims=True)
    xc = x - mu
    var = jnp.mean(xc * xc, -1, keepdims=True)
    return xc * lax.rsqrt(var + LN_EPS)


def _run_interleaved(chains):
    pending = list(chains)
    results = {}
    while pending:
        for chain in list(pending):
            try:
                next(chain)
            except StopIteration as done:
                results[id(chain)] = done.value
                pending.remove(chain)
    return [results[id(chain)] for chain in chains]


def _adaln_kernel(c_ref, w_ref, b_ref, o_ref):
    cond = c_ref[...]
    cond = cond * jax.nn.sigmoid(cond)
    o_ref[...] = _dot(cond, w_ref[...]) + b_ref[...]


def _adaln(c, w, b):
    bsz, d = c.shape
    n = w.shape[1]
    tn = 1024
    return pl.pallas_call(
        _adaln_kernel,
        grid=(n // tn,),
        in_specs=[pl.BlockSpec((bsz, d), lambda j: (0, 0)),
                  pl.BlockSpec((d, tn), lambda j: (0, j)),
                  pl.BlockSpec((1, tn), lambda j: (0, j))],
        out_specs=pl.BlockSpec((bsz, tn), lambda j: (0, j)),
        out_shape=jax.ShapeDtypeStruct((bsz, n), F32),
        compiler_params=pltpu.CompilerParams(dimension_semantics=("arbitrary",),
                                             vmem_limit_bytes=VMEM_LIMIT),
        name="adaln",
    )(c, w, b.reshape(1, n))


def _rope_kernel(pos_ref, inv_ref, sign_ref, cos_ref, sin_ref):
    hs = pos_ref.shape[1] // 2
    lo = lax.broadcasted_iota(jnp.int32, (hs, DA_HEAD_DIM), 1) < DA_HEAD_DIM // 2
    ang = jnp.where(lo, pos_ref[0, :hs, :], pos_ref[0, hs:, :]) * inv_ref[...]
    for fn, ref, scale in ((jnp.cos, cos_ref, None), (jnp.sin, sin_ref, sign_ref[...])):
        val = fn(ang)
        swapped = pltpu.roll(val, DA_HEAD_DIM // 2, 1)
        top = jnp.where(lo, val, swapped)
        bottom = jnp.where(lo, swapped, val)
        ref[0, :hs, :] = top if scale is None else top * scale
        ref[0, hs:, :] = bottom if scale is None else bottom * scale


def _rope_tables(positions):
    bsz, s = positions.shape
    ts = 2048
    half = DA_HEAD_DIM // 2
    inv = 1.0 / (ROPE_THETA ** (jnp.arange(0, DA_HEAD_DIM, 2, dtype=F32) / DA_HEAD_DIM))
    inv2 = jnp.concatenate([inv, inv]).reshape(1, DA_HEAD_DIM)
    sign = jnp.concatenate([-jnp.ones((half,), F32), jnp.ones((half,), F32)]).reshape(1, DA_HEAD_DIM)
    pos = positions.astype(F32).reshape(bsz, s, 1)
    tab = jax.ShapeDtypeStruct((bsz, s, DA_HEAD_DIM), F32)
    return pl.pallas_call(
        _rope_kernel,
        grid=(bsz, s // ts),
        in_specs=[pl.BlockSpec((1, ts, 1), lambda b, i: (b, i, 0)),
                  pl.BlockSpec((1, DA_HEAD_DIM), lambda b, i: (0, 0)),
                  pl.BlockSpec((1, DA_HEAD_DIM), lambda b, i: (0, 0))],
        out_specs=[pl.BlockSpec((1, ts, DA_HEAD_DIM), lambda b, i: (b, i, 0)),
                   pl.BlockSpec((1, ts, DA_HEAD_DIM), lambda b, i: (b, i, 0))],
        out_shape=[tab, tab],
        compiler_params=pltpu.CompilerParams(dimension_semantics=("arbitrary", "arbitrary")),
        name="rope_tables",
    )(pos, inv2, sign)


PROJ_ROWS = 512


def _zero_after(x):
    bits = pltpu.bitcast(x, jnp.int32)
    acc = bits[0:8, :]
    for r in range(8, bits.shape[0], 8):
        acc = acc | bits[r:r + 8, :]
    word = acc[:, 0:128]
    for c in range(128, acc.shape[1], 128):
        word = word | acc[:, c:c + 128]
    word = lax.shift_right_logical(lax.shift_right_logical(word, 16), 16)
    return pltpu.bitcast(word, F32)[0:1, :].astype(BF16)


def _lagged_steps(matmul, finish, n_tiles):
    t = pl.program_id(0)

    @pl.when(t == 0)
    def _():
        matmul(0, None)

    for parity in (0, 1):
        @pl.when(jnp.logical_and(jnp.logical_and(t > 0, t < n_tiles), t % 2 == parity))
        def _():
            matmul(parity, finish(1 - parity))

    @pl.when(t == n_tiles)
    def _():
        finish((n_tiles - 1) % 2)


def _tile_maps(tiles_per_batch, n_tiles):
    def split(k):
        return k // tiles_per_batch, k % tiles_per_batch
    cur = lambda t: (*split(jnp.minimum(t, n_tiles - 1)), 0)
    lag = lambda t: (*split(jnp.maximum(t - 1, 0)), 0)
    lag_batch = lambda t: (jnp.maximum(t - 1, 0) // tiles_per_batch, 0, 0)
    return cur, lag, lag_batch


def _rope_apply(acc, cos, sin, scale):
    outs = []
    for h in range(SEG // DA_HEAD_DIM):
        xh = acc[:, h * DA_HEAD_DIM:(h + 1) * DA_HEAD_DIM]
        rot = pltpu.roll(xh, DA_HEAD_DIM // 2, 1)
        y = xh * cos + rot * sin
        if scale is not None:
            y = y * scale
        outs.append(y.astype(BF16))
    return jnp.concatenate(outs, axis=-1)


def _sigmoid_tanh(x):
    return 0.5 * jnp.tanh(0.5 * x) + 0.5


def _finish_hg_q(y, side, outs):
    q = (y * _sigmoid_tanh(y) * (HG_DK ** -0.5)).astype(BF16)
    outs[0][0] = q
    return _zero_after(q)


def _finish_hg_f(y, side, outs):
    lbl = side[0][...]
    e = jnp.exp(lbl - jnp.max(lbl, axis=0, keepdims=True))
    lb = e[0:1, :] / jnp.sum(e, axis=0, keepdims=True)
    f = lb + (1.0 - lb) * jax.nn.sigmoid(y)
    log_f = jnp.log(f)
    key = (1.0 - f).astype(BF16)
    outs[0][0] = log_f
    outs[1][0] = key
    return _zero_after(log_f) + _zero_after(key)


def _finish_cast(y, side, outs):
    outs[0][0] = y.astype(BF16)
    return None


def _finish_hg_g(y, side, outs):
    g = _sigmoid_tanh(y).astype(BF16)
    outs[0][0] = g
    return _zero_after(g)


def _finish_da_q(y, side, outs):
    q = _rope_apply(y, side[0][0], side[1][0], DA_HEAD_DIM ** -0.5 * math.log2(math.e))
    outs[0][0] = q
    return _zero_after(q)


def _finish_da_k(y, side, outs):
    k = _rope_apply(y, side[0][0], side[1][0], None)
    outs[0][0] = k
    return _zero_after(k)


def _segment_kernel(finish, n_side, n_tiles, u_ref, w_ref, *refs):
    side, outs, y_ref = refs[:n_side], refs[n_side:-1], refs[-1]

    half = SEG // 2

    def matmul(slot, zero):
        u = u_ref[0]
        y_ref[slot, :, :half] = _dot(u, w_ref[:, :half])
        if zero is not None:
            u = u + jnp.tile(zero, (1, u.shape[1] // zero.shape[1]))
        y_ref[slot, :, half:] = _dot(u, w_ref[:, half:])

    _lagged_steps(matmul, lambda slot: finish(y_ref[slot], side, outs), n_tiles)


def _segment(u, w_in, seg, finish, side, side_specs, out_dtypes, name):
    bsz, s, d = u.shape
    tm = PROJ_ROWS
    n_tiles = bsz * (s // tm)
    cur, lag, _ = _tile_maps(s // tm, n_tiles)
    specs = [pl.BlockSpec((1, tm, d), cur), pl.BlockSpec((d, SEG), lambda t: (0, seg))]
    specs += [spec(lag) for spec in side_specs]
    return pl.pallas_call(
        functools.partial(_segment_kernel, finish, len(side), n_tiles),
        grid=(n_tiles + 1,),
        in_specs=specs,
        out_specs=[pl.BlockSpec((1, tm, SEG), lag) for _ in out_dtypes],
        out_shape=[jax.ShapeDtypeStruct((bsz, s, SEG), dt) for dt in out_dtypes],
        scratch_shapes=[pltpu.VMEM((2, tm, SEG), F32)],
        compiler_params=pltpu.CompilerParams(dimension_semantics=("arbitrary",),
                                             vmem_limit_bytes=VMEM_LIMIT),
        name=name,
    )(u, w_in, *side)


LN_CHUNK_ROWS = 256


def _first_segment_kernel(finish, n_tiles, x_ref, mod_ref, w_ref, u_ref, o_ref, y_ref):
    half = SEG // 2

    def matmul(slot, zero):
        m = mod_ref[0]
        for r0 in range(0, x_ref.shape[1], LN_CHUNK_ROWS):
            rows = slice(r0, r0 + LN_CHUNK_ROWS)
            u = (_ln_plain(x_ref[0, rows, :]) * (1.0 + m[1:2, :]) + m[0:1, :]).astype(BF16)
            u_ref[0, rows, :] = u
            y_ref[slot, rows, :half] = _dot(u, w_ref[:, :half])
            if zero is not None:
                u = u + jnp.tile(zero, (1, u.shape[1] // zero.shape[1]))
            y_ref[slot, rows, half:] = _dot(u, w_ref[:, half:])

    _lagged_steps(matmul, lambda slot: finish(y_ref[slot], (), (o_ref,)), n_tiles)


def _first_segment(x, mod, w_in, finish, name):
    bsz, s, d = x.shape
    tm = PROJ_ROWS
    n_tiles = bsz * (s // tm)
    cur, lag, _ = _tile_maps(s // tm, n_tiles)
    cur_batch = lambda t: (jnp.minimum(t, n_tiles - 1) // (s // tm), 0, 0)
    return pl.pallas_call(
        functools.partial(_first_segment_kernel, finish, n_tiles),
        grid=(n_tiles + 1,),
        in_specs=[pl.BlockSpec((1, tm, d), cur), pl.BlockSpec((1, 6, d), cur_batch),
                  pl.BlockSpec((d, SEG), lambda t: (0, 0))],
        out_specs=[pl.BlockSpec((1, tm, d), cur), pl.BlockSpec((1, tm, SEG), lag)],
        out_shape=[jax.ShapeDtypeStruct((bsz, s, d), BF16), jax.ShapeDtypeStruct((bsz, s, SEG), BF16)],
        scratch_shapes=[pltpu.VMEM((2, tm, SEG), F32)],
        compiler_params=pltpu.CompilerParams(dimension_semantics=("arbitrary",),
                                             vmem_limit_bytes=VMEM_LIMIT),
        name=name,
    )(x, mod, w_in)


def _inproj(x, mod, w_in, lb_logits, cos, sin):
    tm = PROJ_ROWS
    table = lambda lag: pl.BlockSpec((1, tm, DA_HEAD_DIM), lag)
    const = lambda lag: pl.BlockSpec(lb_logits.shape, lambda t: (0, 0))
    u, qs = _first_segment(x, mod, w_in, _finish_hg_q, "inproj_ln_hg_q")
    lf, ks = _segment(u, w_in, 1, _finish_hg_f, [lb_logits], [const], [F32, BF16], "inproj_hg_f")
    vs, = _segment(u, w_in, 2, _finish_cast, [], [], [BF16], "inproj_hg_i")
    gs, = _segment(u, w_in, 3, _finish_hg_g, [], [], [BF16], "inproj_hg_g")
    dq, = _segment(u, w_in, 4, _finish_da_q, [cos, sin], [table, table], [BF16], "inproj_da_q")
    dk, = _segment(u, w_in, 5, _finish_da_k, [cos, sin], [table, table], [BF16], "inproj_da_k")
    dv, = _segment(u, w_in, 6, _finish_cast, [], [], [BF16], "inproj_da_v")
    return qs, lf, ks, vs, gs, dq, dk, dv


def _cast_specs(mats, n_steps, step_of):
    specs, shapes = [], []
    for m in mats:
        rows = m.shape[0] // n_steps
        assert rows * n_steps == m.shape[0] and rows % 16 == 0, (m.shape, n_steps)
        specs.append(pl.BlockSpec((rows, m.shape[1]), lambda *idx: (step_of(*idx), 0)))
        shapes.append(jax.ShapeDtypeStruct(m.shape, BF16))
    return specs, shapes


HG_BLOCK = 256
HG_HEADS_PER_STEP = 8


def _hgrn_consts():
    L, C = HG_BLOCK, HG_CHUNK
    r = np.arange(L)[:, None]
    c = np.arange(L)[None, :]
    t_cum = (((r // C) == (c // C)) & (c <= r)).astype(np.float32)
    return jnp.asarray(t_cum, BF16), jnp.asarray(t_cum)


def _bcast_chunk_rows(x, row_in_chunk):
    L, C = HG_BLOCK, HG_CHUNK
    parts = [jnp.broadcast_to(x[c * C + row_in_chunk:c * C + row_in_chunk + 1, :], (C, x.shape[1]))
             for c in range(L // C)]
    return jnp.concatenate(parts, axis=0)


def _hgrn_head(q, lf, k, v, g, nw, tmat, maskf, st):
    L, C = HG_BLOCK, HG_CHUNK

    hi = lf.astype(BF16)
    r1 = lf - hi.astype(F32)
    mid = r1.astype(BF16)
    lo = (r1 - mid.astype(F32)).astype(BF16)
    gg = _dot(tmat, jnp.concatenate([hi, mid, lo], axis=1))
    yield
    g_cum = gg[:, :HG_DK] + gg[:, HG_DK:2 * HG_DK] + gg[:, 2 * HG_DK:]
    g_mid = _bcast_chunk_rows(g_cum, C // 2 - 1)
    g_last = _bcast_chunk_rows(g_cum, C - 1)

    qa = (q * jnp.exp(g_cum - g_mid)).astype(BF16)
    ka = (k * jnp.exp(g_mid - g_cum)).astype(BF16)
    qd = (q * jnp.exp(g_cum)).astype(BF16)
    kd = (k * jnp.exp(g_last - g_cum)).astype(BF16)
    dl = jnp.exp(g_last)

    a = _dot_nt(qa, ka)
    yield
    a = jnp.where(maskf > 0.0, a, 0.0).astype(BF16)
    o_intra = _dot(a, v)
    yield

    zeros = jnp.zeros((C, HG_DK), BF16)
    upds = []
    for p in range(L // (2 * C)):
        r0 = 2 * p * C
        rhs = jnp.concatenate(
            [jnp.concatenate([kd[r0:r0 + C], zeros], axis=1),
             jnp.concatenate([zeros, kd[r0 + C:r0 + 2 * C]], axis=1)], axis=0)
        u2 = _dot_tn(v[r0:r0 + 2 * C], rhs)
        upds += [u2[:, :HG_DK], u2[:, HG_DK:]]
    yield

    starts = []
    for c in range(L // C):
        starts.append(st.astype(BF16))
        st = st * dl[c * C:c * C + 1, :] + upds[c]

    outs = [o_intra[c * C:(c + 1) * C] + _dot_nt(qd[c * C:(c + 1) * C], starts[c])
            for c in range(L // C)]
    yield
    o = jnp.concatenate(outs, axis=0)
    ms = jnp.mean(o * o, -1, keepdims=True)
    y = o * lax.rsqrt(ms + RMS_EPS) * nw * g
    return y.astype(BF16), st


def _hgrn_kernel(q_ref, lf_ref, k_ref, v_ref, g_ref, nw_ref, tmat_ref, mask_ref, w_ref,
                 o_ref, wb_ref, st_ref):
    @pl.when(pl.program_id(2) == 0)
    def _():
        st_ref[...] = jnp.zeros_like(st_ref)

    wb_ref[...] = w_ref[...].astype(BF16)

    lanes = [slice(h * HG_DK, (h + 1) * HG_DK) for h in range(HG_HEADS_PER_STEP)]
    heads = [_hgrn_head(q_ref[0, :, ln].astype(F32), lf_ref[0, :, ln], k_ref[0, :, ln].astype(F32),
                        v_ref[0, :, ln], g_ref[0, :, ln].astype(F32), nw_ref[...], tmat_ref[...],
                        mask_ref[...], st_ref[h]) for h, ln in enumerate(lanes)]
    for h, (y, st) in enumerate(_run_interleaved(heads)):
        o_ref[0, :, lanes[h]] = y
        st_ref[h] = st


def _hgrn(qs, lf, ks, vs, gs, norm_w, to_cast):
    bsz, s, _ = qs.shape
    L = HG_BLOCK
    hp = HG_HEADS_PER_STEP
    tmat, mask = _hgrn_consts()
    blk = pl.BlockSpec((1, L, hp * HG_DK), lambda b, h, n: (b, n, h))
    const = lambda shape: pl.BlockSpec(shape, lambda b, h, n: (0, 0))
    heads, blocks = HG_HEADS // hp, s // L
    cast_specs, cast_shapes = _cast_specs([to_cast], bsz * heads * blocks,
                                          lambda b, h, n: (b * heads + h) * blocks + n)
    return pl.pallas_call(
        _hgrn_kernel,
        grid=(bsz, heads, blocks),
        in_specs=[blk, blk, blk, blk, blk, const((1, HG_DK)), const((L, L)), const((L, L))] + cast_specs,
        out_specs=[blk] + cast_specs,
        out_shape=[jax.ShapeDtypeStruct((bsz, s, HG_WIDTH), BF16)] + cast_shapes,
        scratch_shapes=[pltpu.VMEM((hp, HG_DK, HG_DK), F32)],
        compiler_params=pltpu.CompilerParams(
            dimension_semantics=("arbitrary", "arbitrary", "arbitrary"),
            vmem_limit_bytes=VMEM_LIMIT),
        name="hgrn",
    )(qs, lf, ks, vs, gs, norm_w.reshape(1, HG_DK), tmat, mask, to_cast)


ATT_BLOCK = 512
ATT_COLS = 256
ATT_ONES = 16


def _attn_scores(k_ref, q_t, n, s_ref, slot):
    T, C, D = ATT_BLOCK, ATT_COLS, DA_HEAD_DIM
    kb = k_ref[0, pl.ds(pl.multiple_of(n * T, T), T), :]
    for t in range(2):
        for c in range(T // C):
            s_ref[slot, t * (T // C) + c] = _dot(kb[:, t * D:(t + 1) * D],
                                                 q_t[t * D:(t + 1) * D, c * C:(c + 1) * C])


def _attn_chain(s, v_t, acc_ref, m_ref, cols, masked):
    if masked:
        key = lax.broadcasted_iota(jnp.int32, s.shape, 0)
        qry = lax.broadcasted_iota(jnp.int32, s.shape, 1) + cols.start
        s = jnp.where(key <= qry, s, jnp.finfo(F32).min)
    m_prev = m_ref[:, cols]
    m_new = jnp.maximum(m_prev, jnp.max(s, axis=0, keepdims=True))
    alpha = jnp.exp2(m_prev - m_new)
    p = jnp.exp2(s - m_new)
    m_ref[:, cols] = m_new
    pv = _dot(v_t, p.astype(BF16))
    yield
    acc_ref[:, cols] = alpha * acc_ref[:, cols] + pv


def _attn_kernel(q_ref, k_ref, v_ref, lq1_ref, lk1_ref, lq2_ref, lk2_ref, sw_ref, *refs):
    n_cast = (len(refs) - 7) // 2
    cast_in, o_ref, cast_out = refs[:n_cast], refs[n_cast], refs[n_cast + 1:2 * n_cast + 1]
    vt_ref, s_ref, acc1_ref, acc2_ref, m1_ref, m2_ref = refs[2 * n_cast + 1:]
    T = ATT_BLOCK
    C = ATT_COLS
    DV = 2 * DA_HEAD_DIM
    nq = q_ref.shape[1] // T

    for src_ref, dst_ref in zip(cast_in, cast_out):
        dst_ref[...] = src_ref[...].astype(BF16)

    for j in range(nq):
        vt_ref[j, :DV, :] = v_ref[0, j * T:(j + 1) * T, :].T
        vt_ref[j, DV:, :] = jnp.ones((ATT_ONES, T), BF16)

    lam = (jnp.exp(jnp.sum(lq1_ref[...] * lk1_ref[...], axis=-1, keepdims=True))
           - jnp.exp(jnp.sum(lq2_ref[...] * lk2_ref[...], axis=-1, keepdims=True))
           + LAMBDA_INIT)

    def queries_t(i):
        return q_ref[0, pl.ds(pl.multiple_of(i * T, T), T), :].T

    def query_block(i, carry):
        q_t = queries_t(i)
        acc1_ref[...] = jnp.zeros_like(acc1_ref)
        acc2_ref[...] = jnp.zeros_like(acc2_ref)
        m1_ref[...] = jnp.full_like(m1_ref, -jnp.inf)
        m2_ref[...] = jnp.full_like(m2_ref, -jnp.inf)

        def step(n, slot, masked, prefetch):
            if prefetch:
                _attn_scores(k_ref, q_t, n + 1, s_ref, 1 - slot)
            vt = vt_ref[n]
            chains = []
            for t, (acc_ref, m_ref) in enumerate(((acc1_ref, m1_ref), (acc2_ref, m2_ref))):
                for c in range(T // C):
                    cols = slice(c * C, (c + 1) * C)
                    nk = (c + 1) * C if masked else T
                    chains.append(_attn_chain(s_ref[slot, t * (T // C) + c, :nk, :], vt[:, :nk],
                                              acc_ref, m_ref, cols, masked))
            _run_interleaved(chains)

        def pair(t, carry):
            step(2 * t, 0, False, True)
            step(2 * t + 1, 1, False, True)
            return carry

        lax.fori_loop(0, i // 2, pair, 0)

        @pl.when(i % 2 == 0)
        def _():
            step(i, 0, True, False)

        @pl.when(i % 2 == 1)
        def _():
            step(i - 1, 0, False, True)
            step(i, 1, True, False)

        _attn_scores(k_ref, queries_t(jnp.minimum(i + 1, nq - 1)), 0, s_ref, 0)

        inv1 = 1.0 / acc1_ref[DV:DV + 1, :]
        inv2 = lam / acc2_ref[DV:DV + 1, :]
        o_t = acc1_ref[:DV, :] * inv1 - acc2_ref[:DV, :] * inv2
        o = o_t.T
        ms = jnp.mean(o * o, -1, keepdims=True)
        y = o * lax.rsqrt(ms + RMS_EPS) * sw_ref[...] * (1.0 - LAMBDA_INIT)
        o_ref[0, pl.ds(pl.multiple_of(i * T, T), T), :] = y.astype(BF16)
        return carry

    _attn_scores(k_ref, queries_t(0), 0, s_ref, 0)
    lax.fori_loop(0, nq, query_block, 0)


def _attn(q, k, v, lq1, lk1, lq2, lk2, subln_w, to_cast):
    bsz, s, _ = q.shape
    T = ATT_BLOCK
    hw = 2 * DA_HEAD_DIM
    blk = pl.BlockSpec((1, s, hw), lambda b, h: (b, 0, h))
    vec = lambda n: pl.BlockSpec((1, n), lambda b, h: (0, 0))
    cast_specs, cast_shapes = _cast_specs(to_cast, bsz * DA_HEADS, lambda b, h: b * DA_HEADS + h)
    return pl.pallas_call(
        _attn_kernel,
        grid=(bsz, DA_HEADS),
        in_specs=[blk, blk, blk, vec(DA_HEAD_DIM), vec(DA_HEAD_DIM), vec(DA_HEAD_DIM),
                  vec(DA_HEAD_DIM), vec(hw)] + cast_specs,
        out_specs=[blk] + cast_specs,
        out_shape=[jax.ShapeDtypeStruct((bsz, s, DA_WIDTH), BF16)] + cast_shapes,
        scratch_shapes=[pltpu.VMEM((s // T, hw + ATT_ONES, T), BF16),
                        pltpu.VMEM((2, 2 * (T // ATT_COLS), T, ATT_COLS), F32),
                        pltpu.VMEM((hw + ATT_ONES, T), F32), pltpu.VMEM((hw + ATT_ONES, T), F32),
                        pltpu.VMEM((1, T), F32), pltpu.VMEM((1, T), F32)],
        compiler_params=pltpu.CompilerParams(
            dimension_semantics=("arbitrary", "arbitrary"),
            vmem_limit_bytes=VMEM_LIMIT),
        name="diff_attn",
    )(q, k, v, lq1.reshape(1, -1), lk1.reshape(1, -1), lq2.reshape(1, -1), lk2.reshape(1, -1),
      subln_w.reshape(1, -1), *to_cast)


def _outproj_kernel(n_tiles, yh_ref, ya_ref, w_ref, x_ref, mod_ref, g_ref, b_ref, h_ref, u_ref, y_ref):
    half = w_ref.shape[1] // 2

    def matmul(slot, zero):
        yh = yh_ref[0]
        ya = ya_ref[0]
        y_ref[slot, :, :half] = _dot(yh, w_ref[:HG_WIDTH, :half]) + _dot(ya, w_ref[HG_WIDTH:, :half])
        if zero is not None:
            ya = ya + jnp.tile(zero, (1, ya.shape[1] // zero.shape[1]))
        y_ref[slot, :, half:] = _dot(yh, w_ref[:HG_WIDTH, half:]) + _dot(ya, w_ref[HG_WIDTH:, half:])

    def finish(slot):
        m = mod_ref[0]
        r = DEEPNORM_ALPHA * x_ref[0] + m[2:3, :] * y_ref[slot]
        h = _ln_plain(r) * g_ref[...] + b_ref[...]
        h_ref[0] = h
        u = (_ln_plain(h) * (1.0 + m[4:5, :]) + m[3:4, :]).astype(BF16)
        u_ref[0] = u
        return _zero_after(u)

    _lagged_steps(matmul, finish, n_tiles)


def _outproj(y_hg, y_da, w_out, x, mod, ln_g, ln_b):
    bsz, s, d = x.shape
    tm = PROJ_ROWS
    n_tiles = bsz * (s // tm)
    cur, lag, lag_batch = _tile_maps(s // tm, n_tiles)
    const = lambda shape: pl.BlockSpec(shape, lambda t: (0, 0))
    return pl.pallas_call(
        functools.partial(_outproj_kernel, n_tiles),
        grid=(n_tiles + 1,),
        in_specs=[pl.BlockSpec((1, tm, HG_WIDTH), cur),
                  pl.BlockSpec((1, tm, DA_WIDTH), cur),
                  const(w_out.shape),
                  pl.BlockSpec((1, tm, d), lag),
                  pl.BlockSpec((1, 6, d), lag_batch),
                  const((1, d)), const((1, d))],
        out_specs=[pl.BlockSpec((1, tm, d), lag), pl.BlockSpec((1, tm, d), lag)],
        out_shape=[jax.ShapeDtypeStruct((bsz, s, d), F32), jax.ShapeDtypeStruct((bsz, s, d), BF16)],
        scratch_shapes=[pltpu.VMEM((2, tm, d), F32)],
        compiler_params=pltpu.CompilerParams(dimension_semantics=("arbitrary",),
                                             vmem_limit_bytes=VMEM_LIMIT),
        name="outproj_ln1",
    )(y_hg, y_da, w_out, x, mod, ln_g.reshape(1, d), ln_b.reshape(1, d))


FFN_ROWS = 512
FFN_COLS = 512


def _ffn_kernel(u_ref, wg_ref, wu_ref, wd_ref, h_ref, mod_ref, g_ref, b_ref, o_ref, acc_ref):
    j = pl.program_id(2)

    @pl.when(j == 0)
    def _():
        acc_ref[...] = jnp.zeros_like(acc_ref)

    u = u_ref[0]
    a = _dot(u, wg_ref[...])
    b = _dot(u, wu_ref[...])
    z = (a * jax.nn.sigmoid(a) * b).astype(BF16)
    acc_ref[...] += _dot(z, wd_ref[...])

    @pl.when(j == pl.num_programs(2) - 1)
    def _():
        m = mod_ref[0]
        r = DEEPNORM_ALPHA * h_ref[0] + m[5:6, :] * acc_ref[...]
        o_ref[0] = _ln_plain(r) * g_ref[...] + b_ref[...]


def _ffn(u, w_gate, w_up, w_down, h, mod, ln_g, ln_b):
    bsz, s, d = h.shape
    f = w_gate.shape[1]
    tm, tf = FFN_ROWS, FFN_COLS
    row = lambda b, i, j: (b, i, 0)
    const3 = lambda shape: pl.BlockSpec(shape, lambda b, i, j: (0, 0))
    return pl.pallas_call(
        _ffn_kernel,
        grid=(bsz, s // tm, f // tf),
        in_specs=[pl.BlockSpec((1, tm, d), row),
                  pl.BlockSpec((d, tf), lambda b, i, j: (0, j)),
                  pl.BlockSpec((d, tf), lambda b, i, j: (0, j)),
                  pl.BlockSpec((tf, d), lambda b, i, j: (j, 0)),
                  pl.BlockSpec((1, tm, d), row),
                  pl.BlockSpec((1, 6, d), lambda b, i, j: (b, 0, 0)),
                  const3((1, d)), const3((1, d))],
        out_specs=pl.BlockSpec((1, tm, d), row),
        out_shape=jax.ShapeDtypeStruct((bsz, s, d), F32),
        scratch_shapes=[pltpu.VMEM((tm, d), F32)],
        compiler_params=pltpu.CompilerParams(
            dimension_semantics=("arbitrary", "arbitrary", "arbitrary"),
            vmem_limit_bytes=VMEM_LIMIT),
        name="ffn_ln2",
    )(u, w_gate, w_up, w_down, h, mod, ln_g.reshape(1, d), ln_b.reshape(1, d))


def kernel(x, c, positions, w_ada, b_ada, w_in, lb_logits, hg_norm_w, lam_q1, lam_k1, lam_q2, lam_k2,
           subln_w, w_out, ln1_g, ln1_b, w_gate, w_up, w_down, ln2_g, ln2_b):
    assert w_ada.shape[0] == DEPTH == 1
    bsz, s, d = x.shape

    mod = _adaln(c, w_ada[0], b_ada[0]).reshape(bsz, 6, d)
    cos, sin = _rope_tables(positions)

    qs, lf, ks, vs, gs, dq, dk, dv = _inproj(x, mod, w_in[0].astype(BF16), lb_logits, cos, sin)
    y_hg, w_out_b = _hgrn(qs, lf, ks, vs, gs, hg_norm_w[0], w_out[0])
    y_da, w_gate_b, w_up_b, w_down_b = _attn(dq, dk, dv, lam_q1[0], lam_k1[0], lam_q2[0], lam_k2[0],
                                             subln_w[0], [w_gate[0], w_up[0], w_down[0]])
    h1, u2 = _outproj(y_hg, y_da, w_out_b, x, mod, ln1_g[0], ln1_b[0])
    return _ffn(u2, w_gate_b, w_up_b, w_down_b, h1, mod, ln2_g[0], ln2_b[0])
```

```python
import functools
import math

import numpy as np
import jax
import jax.numpy as jnp
from jax import lax
from jax.experimental import pallas as pl
from jax.experimental.pallas import tpu as pltpu

D_MODEL = 2048
HG_WIDTH = 1024
DA_WIDTH = 1024
HG_HEADS = 8
HG_DK = 128
HG_CHUNK = 32
DA_HEADS = 4
DA_HEAD_DIM = 128
ROPE_THETA = 10000.0
D_FF = 5632
DEPTH = 1
DEEPNORM_ALPHA = (2.0 * DEPTH) ** 0.25
LN_EPS = 1e-5
RMS_EPS = 1e-6
LAMBDA_INIT = 0.8 - 0.6 * math.exp(-0.3 * 0)
SEG = 1024
N_SEG = 7

VMEM_LIMIT = 56 * 1024 * 1024

F32 = jnp.float32
BF16 = jnp.bfloat16


def _dot(a, b):
    return jnp.dot(a, b, preferred_element_type=F32)


def _dot_nt(a, b):
    return lax.dot_general(a, b, (((1,), (1,)), ((), ())), preferred_element_type=F32)


def _dot_tn(a, b):
    return lax.dot_general(a, b, (((0,), (0,)), ((), ())), preferred_element_type=F32)


def _ln_plain(x):
    mu = jnp.mean(x, -1, keepdims=True)
    xc = x - mu
    var = jnp.mean(xc * xc, -1, keepdims=True)
    return xc * lax.rsqrt(var + LN_EPS)


def _run_interleaved(chains):
    pending = list(chains)
    results = {}
    while pending:
        for chain in list(pending):
            try:
                next(chain)
            except StopIteration as done:
                results[id(chain)] = done.value
                pending.remove(chain)
    return [results[id(chain)] for chain in chains]


def _cast_specs(mats, n_steps, step_of):
    specs, shapes = [], []
    for m in mats:
        rows = m.shape[0] // n_steps
        assert rows * n_steps == m.shape[0] and rows % 16 == 0, (m.shape, n_steps)
        specs.append(pl.BlockSpec((rows, m.shape[1]), lambda *idx: (step_of(*idx), 0)))
        shapes.append(jax.ShapeDtypeStruct(m.shape, BF16))
    return specs, shapes


def _adaln_kernel(c_ref, w_ref, b_ref, o_ref):
    cond = c_ref[...]
    cond = cond * jax.nn.sigmoid(cond)
    o_ref[...] = _dot(cond, w_ref[...]) + b_ref[...]


def _adaln(c, w, b):
    bsz, d = c.shape
    n = w.shape[1]
    tn = 1024
    return pl.pallas_call(
        _adaln_kernel,
        grid=(n // tn,),
        in_specs=[pl.BlockSpec((bsz, d), lambda j: (0, 0)),
                  pl.BlockSpec((d, tn), lambda j: (0, j)),
                  pl.BlockSpec((1, tn), lambda j: (0, j))],
        out_specs=pl.BlockSpec((bsz, tn), lambda j: (0, j)),
        out_shape=jax.ShapeDtypeStruct((bsz, n), F32),
        compiler_params=pltpu.CompilerParams(dimension_semantics=("arbitrary",),
                                             vmem_limit_bytes=VMEM_LIMIT),
        name="adaln",
    )(c, w, b.reshape(1, n))


def _rope_kernel(pos_ref, inv_ref, sign_ref, w_ref, cos_ref, sin_ref, wb_ref):
    wb_ref[...] = w_ref[...].astype(BF16)

    hs = pos_ref.shape[1] // 2
    lo = lax.broadcasted_iota(jnp.int32, (hs, DA_HEAD_DIM), 1) < DA_HEAD_DIM // 2
    ang = jnp.where(lo, pos_ref[0, :hs, :], pos_ref[0, hs:, :]) * inv_ref[...]
    for fn, ref, scale in ((jnp.cos, cos_ref, None), (jnp.sin, sin_ref, sign_ref[...])):
        val = fn(ang)
        swapped = pltpu.roll(val, DA_HEAD_DIM // 2, 1)
        top = jnp.where(lo, val, swapped)
        bottom = jnp.where(lo, swapped, val)
        ref[0, :hs, :] = top if scale is None else top * scale
        ref[0, hs:, :] = bottom if scale is None else bottom * scale


def _rope_tables(positions, to_cast):
    bsz, s = positions.shape
    ts = 2048
    half = DA_HEAD_DIM // 2
    inv = 1.0 / (ROPE_THETA ** (jnp.arange(0, DA_HEAD_DIM, 2, dtype=F32) / DA_HEAD_DIM))
    inv2 = jnp.concatenate([inv, inv]).reshape(1, DA_HEAD_DIM)
    sign = jnp.concatenate([-jnp.ones((half,), F32), jnp.ones((half,), F32)]).reshape(1, DA_HEAD_DIM)
    pos = positions.astype(F32).reshape(bsz, s, 1)
    tab = jax.ShapeDtypeStruct((bsz, s, DA_HEAD_DIM), F32)
    cast_specs, cast_shapes = _cast_specs([to_cast], bsz * (s // ts), lambda b, i: b * (s // ts) + i)
    return pl.pallas_call(
        _rope_kernel,
        grid=(bsz, s // ts),
        in_specs=[pl.BlockSpec((1, ts, 1), lambda b, i: (b, i, 0)),
                  pl.BlockSpec((1, DA_HEAD_DIM), lambda b, i: (0, 0)),
                  pl.BlockSpec((1, DA_HEAD_DIM), lambda b, i: (0, 0))] + cast_specs,
        out_specs=[pl.BlockSpec((1, ts, DA_HEAD_DIM), lambda b, i: (b, i, 0)),
                   pl.BlockSpec((1, ts, DA_HEAD_DIM), lambda b, i: (b, i, 0))] + cast_specs,
        out_shape=[tab, tab] + cast_shapes,
        compiler_params=pltpu.CompilerParams(dimension_semantics=("arbitrary", "arbitrary"),
                                             vmem_limit_bytes=VMEM_LIMIT),
        name="rope_tables",
    )(pos, inv2, sign, to_cast)


PROJ_ROWS = 512
SEG_ROWS = 1024


def _zero_after(x):
    bits = pltpu.bitcast(x, jnp.int32)
    acc = bits[0:8, :]
    for r in range(8, bits.shape[0], 8):
        acc = acc | bits[r:r + 8, :]
    word = acc[:, 0:128]
    for c in range(128, acc.shape[1], 128):
        word = word | acc[:, c:c + 128]
    word = lax.shift_right_logical(lax.shift_right_logical(word, 16), 16)
    return pltpu.bitcast(word, F32)[0:1, :].astype(BF16)


def _lagged_steps(matmul, finish, n_tiles):
    t = pl.program_id(0)

    @pl.when(t == 0)
    def _():
        matmul(0, None)

    for parity in (0, 1):
        @pl.when(jnp.logical_and(jnp.logical_and(t > 0, t < n_tiles), t % 2 == parity))
        def _():
            matmul(parity, finish(1 - parity))

    @pl.when(t == n_tiles)
    def _():
        finish((n_tiles - 1) % 2)


def _tile_maps(tiles_per_batch, n_tiles):
    def split(k):
        return k // tiles_per_batch, k % tiles_per_batch
    cur = lambda t: (*split(jnp.minimum(t, n_tiles - 1)), 0)
    lag = lambda t: (*split(jnp.maximum(t - 1, 0)), 0)
    lag_batch = lambda t: (jnp.maximum(t - 1, 0) // tiles_per_batch, 0, 0)
    return cur, lag, lag_batch


def _rope_apply(acc, cos, sin, scale):
    outs = []
    for h in range(SEG // DA_HEAD_DIM):
        xh = acc[:, h * DA_HEAD_DIM:(h + 1) * DA_HEAD_DIM]
        rot = pltpu.roll(xh, DA_HEAD_DIM // 2, 1)
        y = xh * cos + rot * sin
        if scale is not None:
            y = y * scale
        outs.append(y.astype(BF16))
    return jnp.concatenate(outs, axis=-1)


def _sigmoid_tanh(x):
    return 0.5 * jnp.tanh(0.5 * x) + 0.5


def _finish_hg_q(y, side, outs):
    q = (y * _sigmoid_tanh(y) * (HG_DK ** -0.5)).astype(BF16)
    outs[0][0] = q
    return _zero_after(q)


def _finish_hg_f(y, side, outs):
    lbl = side[0][...]
    e = jnp.exp(lbl - jnp.max(lbl, axis=0, keepdims=True))
    lb = e[0:1, :] / jnp.sum(e, axis=0, keepdims=True)
    f = lb + (1.0 - lb) * jax.nn.sigmoid(y)
    log_f = jnp.log(f)
    key = (1.0 - f).astype(BF16)
    outs[0][0] = log_f
    outs[1][0] = key
    return _zero_after(log_f) + _zero_after(key)


def _finish_cast(y, side, outs):
    outs[0][0] = y.astype(BF16)
    return None


def _finish_hg_g(y, side, outs):
    g = _sigmoid_tanh(y).astype(BF16)
    outs[0][0] = g
    return _zero_after(g)


def _finish_da_q(y, side, outs):
    q = _rope_apply(y, side[0][0], side[1][0], DA_HEAD_DIM ** -0.5 * math.log2(math.e))
    outs[0][0] = q
    return _zero_after(q)


def _finish_da_k(y, side, outs):
    k = _rope_apply(y, side[0][0], side[1][0], None)
    outs[0][0] = k
    return _zero_after(k)


def _segment_kernel(finish, n_side, n_tiles, u_ref, w_ref, *refs):
    side, outs, y_ref = refs[:n_side], refs[n_side:-1], refs[-1]

    half = SEG // 2

    def matmul(slot, zero):
        u = u_ref[0]
        y_ref[slot, :, :half] = _dot(u, w_ref[:, :half])
        if zero is not None:
            u = u + jnp.tile(zero, (1, u.shape[1] // zero.shape[1]))
        y_ref[slot, :, half:] = _dot(u, w_ref[:, half:])

    _lagged_steps(matmul, lambda slot: finish(y_ref[slot], side, outs), n_tiles)


def _segment(u, w_in, seg, finish, side, side_specs, out_dtypes, name):
    bsz, s, d = u.shape
    tm = SEG_ROWS
    n_tiles = bsz * (s // tm)
    cur, lag, _ = _tile_maps(s // tm, n_tiles)
    specs = [pl.BlockSpec((1, tm, d), cur), pl.BlockSpec((d, SEG), lambda t: (0, seg))]
    specs += [spec(lag) for spec in side_specs]
    return pl.pallas_call(
        functools.partial(_segment_kernel, finish, len(side), n_tiles),
        grid=(n_tiles + 1,),
        in_specs=specs,
        out_specs=[pl.BlockSpec((1, tm, SEG), lag) for _ in out_dtypes],
        out_shape=[jax.ShapeDtypeStruct((bsz, s, SEG), dt) for dt in out_dtypes],
        scratch_shapes=[pltpu.VMEM((2, tm, SEG), F32)],
        compiler_params=pltpu.CompilerParams(dimension_semantics=("arbitrary",),
                                             vmem_limit_bytes=VMEM_LIMIT),
        name=name,
    )(u, w_in, *side)


LN_CHUNK_ROWS = 256


def _first_segment_kernel(finish, n_tiles, x_ref, mod_ref, w_ref, u_ref, o_ref, y_ref):
    half = SEG // 2

    def matmul(slot, zero):
        m = mod_ref[0]
        for r0 in range(0, x_ref.shape[1], LN_CHUNK_ROWS):
            rows = slice(r0, r0 + LN_CHUNK_ROWS)
            u = (_ln_plain(x_ref[0, rows, :]) * (1.0 + m[1:2, :]) + m[0:1, :]).astype(BF16)
            u_ref[0, rows, :] = u
            y_ref[slot, rows, :half] = _dot(u, w_ref[:, :half])
            if zero is not None:
                u = u + jnp.tile(zero, (1, u.shape[1] // zero.shape[1]))
            y_ref[slot, rows, half:] = _dot(u, w_ref[:, half:])

    _lagged_steps(matmul, lambda slot: finish(y_ref[slot], (), (o_ref,)), n_tiles)


def _first_segment(x, mod, w_in, finish, name):
    bsz, s, d = x.shape
    tm = SEG_ROWS
    n_tiles = bsz * (s // tm)
    cur, lag, _ = _tile_maps(s // tm, n_tiles)
    cur_batch = lambda t: (jnp.minimum(t, n_tiles - 1) // (s // tm), 0, 0)
    return pl.pallas_call(
        functools.partial(_first_segment_kernel, finish, n_tiles),
        grid=(n_tiles + 1,),
        in_specs=[pl.BlockSpec((1, tm, d), cur), pl.BlockSpec((1, 6, d), cur_batch),
                  pl.BlockSpec((d, SEG), lambda t: (0, 0))],
        out_specs=[pl.BlockSpec((1, tm, d), cur), pl.BlockSpec((1, tm, SEG), lag)],
        out_shape=[jax.ShapeDtypeStruct((bsz, s, d), BF16), jax.ShapeDtypeStruct((bsz, s, SEG), BF16)],
        scratch_shapes=[pltpu.VMEM((2, tm, SEG), F32)],
        compiler_params=pltpu.CompilerParams(dimension_semantics=("arbitrary",),
                                             vmem_limit_bytes=VMEM_LIMIT),
        name=name,
    )(x, mod, w_in)


def _inproj(x, mod, w_in, lb_logits, cos, sin):
    tm = SEG_ROWS
    table = lambda lag: pl.BlockSpec((1, tm, DA_HEAD_DIM), lag)
    const = lambda lag: pl.BlockSpec(lb_logits.shape, lambda t: (0, 0))
    u, qs = _first_segment(x, mod, w_in, _finish_hg_q, "inproj_ln_hg_q")
    lf, ks = _segment(u, w_in, 1, _finish_hg_f, [lb_logits], [const], [F32, BF16], "inproj_hg_f")
    vs, = _segment(u, w_in, 2, _finish_cast, [], [], [BF16], "inproj_hg_i")
    gs, = _segment(u, w_in, 3, _finish_hg_g, [], [], [BF16], "inproj_hg_g")
    dq, = _segment(u, w_in, 4, _finish_da_q, [cos, sin], [table, table], [BF16], "inproj_da_q")
    dk, = _segment(u, w_in, 5, _finish_da_k, [cos, sin], [table, table], [BF16], "inproj_da_k")
    dv, = _segment(u, w_in, 6, _finish_cast, [], [], [BF16], "inproj_da_v")
    return qs, lf, ks, vs, gs, dq, dk, dv


HG_BLOCK = 256
HG_HEADS_PER_STEP = 8


def _hgrn_consts():
    L, C = HG_BLOCK, HG_CHUNK
    r = np.arange(L)[:, None]
    c = np.arange(L)[None, :]
    t_cum = (((r // C) == (c // C)) & (c <= r)).astype(np.float32)
    return jnp.asarray(t_cum, BF16), jnp.asarray(t_cum)


def _bcast_chunk_rows(x, row_in_chunk):
    L, C = HG_BLOCK, HG_CHUNK
    parts = [jnp.broadcast_to(x[c * C + row_in_chunk:c * C + row_in_chunk + 1, :], (C, x.shape[1]))
             for c in range(L // C)]
    return jnp.concatenate(parts, axis=0)


def _hgrn_head(q, lf, k, v, g, nw, tmat, maskf, st):
    L, C = HG_BLOCK, HG_CHUNK

    hi = lf.astype(BF16)
    r1 = lf - hi.astype(F32)
    mid = r1.astype(BF16)
    lo = (r1 - mid.astype(F32)).astype(BF16)
    gg = _dot(tmat, jnp.concatenate([hi, mid, lo], axis=1))
    yield
    g_cum = gg[:, :HG_DK] + gg[:, HG_DK:2 * HG_DK] + gg[:, 2 * HG_DK:]
    g_mid = _bcast_chunk_rows(g_cum, C // 2 - 1)
    g_last = _bcast_chunk_rows(g_cum, C - 1)

    qa = (q * jnp.exp(g_cum - g_mid)).astype(BF16)
    ka = (k * jnp.exp(g_mid - g_cum)).astype(BF16)
    qd = (q * jnp.exp(g_cum)).astype(BF16)
    kd = (k * jnp.exp(g_last - g_cum)).astype(BF16)
    dl = jnp.exp(g_last)

    a = _dot_nt(qa, ka)
    yield
    a = jnp.where(maskf > 0.0, a, 0.0).astype(BF16)
    o_intra = _dot(a, v)
    yield

    zeros = jnp.zeros((C, HG_DK), BF16)
    upds = []
    for p in range(L // (2 * C)):
        r0 = 2 * p * C
        rhs = jnp.concatenate(
            [jnp.concatenate([kd[r0:r0 + C], zeros], axis=1),
             jnp.concatenate([zeros, kd[r0 + C:r0 + 2 * C]], axis=1)], axis=0)
        u2 = _dot_tn(v[r0:r0 + 2 * C], rhs)
        upds += [u2[:, :HG_DK], u2[:, HG_DK:]]
    yield

    starts = []
    for c in range(L // C):
        starts.append(st.astype(BF16))
        st = st * dl[c * C:c * C + 1, :] + upds[c]

    outs = [o_intra[c * C:(c + 1) * C] + _dot_nt(qd[c * C:(c + 1) * C], starts[c])
            for c in range(L // C)]
    yield
    o = jnp.concatenate(outs, axis=0)
    ms = jnp.mean(o * o, -1, keepdims=True)
    y = o * lax.rsqrt(ms + RMS_EPS) * nw * g
    return y.astype(BF16), st


def _hgrn_kernel(q_ref, lf_ref, k_ref, v_ref, g_ref, nw_ref, tmat_ref, mask_ref, w_ref,
                 o_ref, wb_ref, st_ref):
    @pl.when(pl.program_id(2) == 0)
    def _():
        st_ref[...] = jnp.zeros_like(st_ref)

    wb_ref[...] = w_ref[...].astype(BF16)

    lanes = [slice(h * HG_DK, (h + 1) * HG_DK) for h in range(HG_HEADS_PER_STEP)]
    heads = [_hgrn_head(q_ref[0, :, ln].astype(F32), lf_ref[0, :, ln], k_ref[0, :, ln].astype(F32),
                        v_ref[0, :, ln], g_ref[0, :, ln].astype(F32), nw_ref[...], tmat_ref[...],
                        mask_ref[...], st_ref[h]) for h, ln in enumerate(lanes)]
    for h, (y, st) in enumerate(_run_interleaved(heads)):
        o_ref[0, :, lanes[h]] = y
        st_ref[h] = st


def _hgrn(qs, lf, ks, vs, gs, norm_w, to_cast):
    bsz, s, _ = qs.shape
    L = HG_BLOCK
    hp = HG_HEADS_PER_STEP
    tmat, mask = _hgrn_consts()
    blk = pl.BlockSpec((1, L, hp * HG_DK), lambda b, h, n: (b, n, h))
    const = lambda shape: pl.BlockSpec(shape, lambda b, h, n: (0, 0))
    heads, blocks = HG_HEADS // hp, s // L
    cast_specs, cast_shapes = _cast_specs([to_cast], bsz * heads * blocks,
                                          lambda b, h, n: (b * heads + h) * blocks + n)
    return pl.pallas_call(
        _hgrn_kernel,
        grid=(bsz, heads, blocks),
        in_specs=[blk, blk, blk, blk, blk, const((1, HG_DK)), const((L, L)), const((L, L))] + cast_specs,
        out_specs=[blk] + cast_specs,
        out_shape=[jax.ShapeDtypeStruct((bsz, s, HG_WIDTH), BF16)] + cast_shapes,
        scratch_shapes=[pltpu.VMEM((hp, HG_DK, HG_DK), F32)],
        compiler_params=pltpu.CompilerParams(
            dimension_semantics=("arbitrary", "arbitrary", "arbitrary"),
            vmem_limit_bytes=VMEM_LIMIT),
        name="hgrn",
    )(qs, lf, ks, vs, gs, norm_w.reshape(1, HG_DK), tmat, mask, to_cast)


ATT_BLOCK = 512
ATT_COLS = 256
ATT_ONES = 16


def _attn_scores(k_ref, q_t, n, s_ref, slot):
    T, C, D = ATT_BLOCK, ATT_COLS, DA_HEAD_DIM
    kb = k_ref[0, pl.ds(pl.multiple_of(n * T, T), T), :]
    for t in range(2):
        for c in range(T // C):
            s_ref[slot, t * (T // C) + c] = _dot(kb[:, t * D:(t + 1) * D],
                                                 q_t[t * D:(t + 1) * D, c * C:(c + 1) * C])


def _attn_chain(s, v_t, acc_ref, m_ref, cols, masked):
    if masked:
        key = lax.broadcasted_iota(jnp.int32, s.shape, 0)
        qry = lax.broadcasted_iota(jnp.int32, s.shape, 1) + cols.start
        s = jnp.where(key <= qry, s, jnp.finfo(F32).min)
    m_prev = m_ref[:, cols]
    m_new = jnp.maximum(m_prev, jnp.max(s, axis=0, keepdims=True))
    alpha = jnp.exp2(m_prev - m_new)
    p = jnp.exp2(s - m_new)
    m_ref[:, cols] = m_new
    pv = _dot(v_t, p.astype(BF16))
    yield
    acc_ref[:, cols] = alpha * acc_ref[:, cols] + pv


def _attn_kernel(q_ref, k_ref, v_ref, lq1_ref, lk1_ref, lq2_ref, lk2_ref, sw_ref, *refs):
    n_cast = (len(refs) - 7) // 2
    cast_in, o_ref, cast_out = refs[:n_cast], refs[n_cast], refs[n_cast + 1:2 * n_cast + 1]
    vt_ref, s_ref, acc1_ref, acc2_ref, m1_ref, m2_ref = refs[2 * n_cast + 1:]
    T = ATT_BLOCK
    C = ATT_COLS
    DV = 2 * DA_HEAD_DIM
    nq = q_ref.shape[1] // T

    for src_ref, dst_ref in zip(cast_in, cast_out):
        dst_ref[...] = src_ref[...].astype(BF16)

    for j in range(nq):
        vt_ref[j, :DV, :] = v_ref[0, j * T:(j + 1) * T, :].T
        vt_ref[j, DV:, :] = jnp.ones((ATT_ONES, T), BF16)

    lam = (jnp.exp(jnp.sum(lq1_ref[...] * lk1_ref[...], axis=-1, keepdims=True))
           - jnp.exp(jnp.sum(lq2_ref[...] * lk2_ref[...], axis=-1, keepdims=True))
           + LAMBDA_INIT)

    def queries_t(i):
        return q_ref[0, pl.ds(pl.multiple_of(i * T, T), T), :].T

    def query_block(i, carry):
        q_t = queries_t(i)
        acc1_ref[...] = jnp.zeros_like(acc1_ref)
        acc2_ref[...] = jnp.zeros_like(acc2_ref)
        m1_ref[...] = jnp.full_like(m1_ref, -jnp.inf)
        m2_ref[...] = jnp.full_like(m2_ref, -jnp.inf)

        def step(n, slot, masked, prefetch):
            if prefetch:
                _attn_scores(k_ref, q_t, n + 1, s_ref, 1 - slot)
            vt = vt_ref[n]
            chains = []
            for t, (acc_ref, m_ref) in enumerate(((acc1_ref, m1_ref), (acc2_ref, m2_ref))):
                for c in range(T // C):
                    cols = slice(c * C, (c + 1) * C)
                    nk = (c + 1) * C if masked else T
                    chains.append(_attn_chain(s_ref[slot, t * (T // C) + c, :nk, :], vt[:, :nk],
                                              acc_ref, m_ref, cols, masked))
            _run_interleaved(chains)

        def pair(t, carry):
            step(2 * t, 0, False, True)
            step(2 * t + 1, 1, False, True)
            return carry

        lax.fori_loop(0, i // 2, pair, 0)

        @pl.when(i % 2 == 0)
        def _():
            step(i, 0, True, False)

        @pl.when(i % 2 == 1)
        def _():
            step(i - 1, 0, False, True)
            step(i, 1, True, False)

        _attn_scores(k_ref, queries_t(jnp.minimum(i + 1, nq - 1)), 0, s_ref, 0)

        inv1 = 1.0 / acc1_ref[DV:DV + 1, :]
        inv2 = lam / acc2_ref[DV:DV + 1, :]
        o_t = acc1_ref[:DV, :] * inv1 - acc2_ref[:DV, :] * inv2
        o = o_t.T
        ms = jnp.mean(o * o, -1, keepdims=True)
        y = o * lax.rsqrt(ms + RMS_EPS) * sw_ref[...] * (1.0 - LAMBDA_INIT)
        o_ref[0, pl.ds(pl.multiple_of(i * T, T), T), :] = y.astype(BF16)
        return carry

    _attn_scores(k_ref, queries_t(0), 0, s_ref, 0)
    lax.fori_loop(0, nq, query_block, 0)


def _attn(q, k, v, lq1, lk1, lq2, lk2, subln_w, to_cast):
    bsz, s, _ = q.shape
    T = ATT_BLOCK
    hw = 2 * DA_HEAD_DIM
    blk = pl.BlockSpec((1, s, hw), lambda b, h: (b, 0, h))
    vec = lambda n: pl.BlockSpec((1, n), lambda b, h: (0, 0))
    cast_specs, cast_shapes = _cast_specs(to_cast, bsz * DA_HEADS, lambda b, h: b * DA_HEADS + h)
    return pl.pallas_call(
        _attn_kernel,
        grid=(bsz, DA_HEADS),
        in_specs=[blk, blk, blk, vec(DA_HEAD_DIM), vec(DA_HEAD_DIM), vec(DA_HEAD_DIM),
                  vec(DA_HEAD_DIM), vec(hw)] + cast_specs,
        out_specs=[blk] + cast_specs,
        out_shape=[jax.ShapeDtypeStruct((bsz, s, DA_WIDTH), BF16)] + cast_shapes,
        scratch_shapes=[pltpu.VMEM((s // T, hw + ATT_ONES, T), BF16),
                        pltpu.VMEM((2, 2 * (T // ATT_COLS), T, ATT_COLS), F32),
                        pltpu.VMEM((hw + ATT_ONES, T), F32), pltpu.VMEM((hw + ATT_ONES, T), F32),
                        pltpu.VMEM((1, T), F32), pltpu.VMEM((1, T), F32)],
        compiler_params=pltpu.CompilerParams(
            dimension_semantics=("arbitrary", "arbitrary"),
            vmem_limit_bytes=VMEM_LIMIT),
        name="diff_attn",
    )(q, k, v, lq1.reshape(1, -1), lk1.reshape(1, -1), lq2.reshape(1, -1), lk2.reshape(1, -1),
      subln_w.reshape(1, -1), *to_cast)


def _outproj_kernel(n_tiles, yh_ref, ya_ref, w_ref, x_ref, mod_ref, g_ref, b_ref, h_ref, u_ref, y_ref):
    half = w_ref.shape[1] // 2

    def matmul(slot, zero):
        yh = yh_ref[0]
        ya = ya_ref[0]
        y_ref[slot, :, :half] = _dot(yh, w_ref[:HG_WIDTH, :half]) + _dot(ya, w_ref[HG_WIDTH:, :half])
        if zero is not None:
            ya = ya + jnp.tile(zero, (1, ya.shape[1] // zero.shape[1]))
        y_ref[slot, :, half:] = _dot(yh, w_ref[:HG_WIDTH, half:]) + _dot(ya, w_ref[HG_WIDTH:, half:])

    def finish(slot):
        m = mod_ref[0]
        r = DEEPNORM_ALPHA * x_ref[0] + m[2:3, :] * y_ref[slot]
        h = _ln_plain(r) * g_ref[...] + b_ref[...]
        h_ref[0] = h
        u = (_ln_plain(h) * (1.0 + m[4:5, :]) + m[3:4, :]).astype(BF16)
        u_ref[0] = u
        return _zero_after(u)

    _lagged_steps(matmul, finish, n_tiles)


def _outproj(y_hg, y_da, w_out, x, mod, ln_g, ln_b):
    bsz, s, d = x.shape
    tm = PROJ_ROWS
    n_tiles = bsz * (s // tm)
    cur, lag, lag_batch = _tile_maps(s // tm, n_tiles)
    const = lambda shape: pl.BlockSpec(shape, lambda t: (0, 0))
    return pl.pallas_call(
        functools.partial(_outproj_kernel, n_tiles),
        grid=(n_tiles + 1,),
        in_specs=[pl.BlockSpec((1, tm, HG_WIDTH), cur),
                  pl.BlockSpec((1, tm, DA_WIDTH), cur),
                  const(w_out.shape),
                  pl.BlockSpec((1, tm, d), lag),
                  pl.BlockSpec((1, 6, d), lag_batch),
                  const((1, d)), const((1, d))],
        out_specs=[pl.BlockSpec((1, tm, d), lag), pl.BlockSpec((1, tm, d), lag)],
        out_shape=[jax.ShapeDtypeStruct((bsz, s, d), F32), jax.ShapeDtypeStruct((bsz, s, d), BF16)],
        scratch_shapes=[pltpu.VMEM((2, tm, d), F32)],
        compiler_params=pltpu.CompilerParams(dimension_semantics=("arbitrary",),
                                             vmem_limit_bytes=VMEM_LIMIT),
        name="outproj_ln1",
    )(y_hg, y_da, w_out, x, mod, ln_g.reshape(1, d), ln_b.reshape(1, d))


FFN_ROWS = 512
FFN_COLS = 512


def _ffn_kernel(u_ref, wg_ref, wu_ref, wd_ref, h_ref, mod_ref, g_ref, b_ref, o_ref, acc_ref):
    j = pl.program_id(2)

    @pl.when(j == 0)
    def _():
        acc_ref[...] = jnp.zeros_like(acc_ref)

    u = u_ref[0]
    a = _dot(u, wg_ref[...])
    b = _dot(u, wu_ref[...])
    z = (a * jax.nn.sigmoid(a) * b).astype(BF16)
    acc_ref[...] += _dot(z, wd_ref[...])

    @pl.when(j == pl.num_programs(2) - 1)
    def _():
        m = mod_ref[0]
        r = DEEPNORM_ALPHA * h_ref[0] + m[5:6, :] * acc_ref[...]
        o_ref[0] = _ln_plain(r) * g_ref[...] + b_ref[...]


def _ffn(u, w_gate, w_up, w_down, h, mod, ln_g, ln_b):
    bsz, s, d = h.shape
    f = w_gate.shape[1]
    tm, tf = FFN_ROWS, FFN_COLS
    row = lambda b, i, j: (b, i, 0)
    const3 = lambda shape: pl.BlockSpec(shape, lambda b, i, j: (0, 0))
    return pl.pallas_call(
        _ffn_kernel,
        grid=(bsz, s // tm, f // tf),
        in_specs=[pl.BlockSpec((1, tm, d), row),
                  pl.BlockSpec((d, tf), lambda b, i, j: (0, j)),
                  pl.BlockSpec((d, tf), lambda b, i, j: (0, j)),
                  pl.BlockSpec((tf, d), lambda b, i, j: (j, 0)),
                  pl.BlockSpec((1, tm, d), row),
                  pl.BlockSpec((1, 6, d), lambda b, i, j: (b, 0, 0)),
                  const3((1, d)), const3((1, d))],
        out_specs=pl.BlockSpec((1, tm, d), row),
        out_shape=jax.ShapeDtypeStruct((bsz, s, d), F32),
        scratch_shapes=[pltpu.VMEM((tm, d), F32)],
        compiler_params=pltpu.CompilerParams(
            dimension_semantics=("arbitrary", "arbitrary", "arbitrary"),
            vmem_limit_bytes=VMEM_LIMIT),
        name="ffn_ln2",
    )(u, w_gate, w_up, w_down, h, mod, ln_g.reshape(1, d), ln_b.reshape(1, d))


def kernel(x, c, positions, w_ada, b_ada, w_in, lb_logits, hg_norm_w, lam_q1, lam_k1, lam_q2, lam_k2,
           subln_w, w_out, ln1_g, ln1_b, w_gate, w_up, w_down, ln2_g, ln2_b):
    assert w_ada.shape[0] == DEPTH == 1
    bsz, s, d = x.shape

    mod = _adaln(c, w_ada[0], b_ada[0]).reshape(bsz, 6, d)
    cos, sin, w_in_b = _rope_tables(positions, w_in[0])

    qs, lf, ks, vs, gs, dq, dk, dv = _inproj(x, mod, w_in_b, lb_logits, cos, sin)
    y_hg, w_out_b = _hgrn(qs, lf, ks, vs, gs, hg_norm_w[0], w_out[0])
    y_da, w_gate_b, w_up_b, w_down_b = _attn(dq, dk, dv, lam_q1[0], lam_k1[0], lam_q2[0], lam_k2[0],
                                             subln_w[0], [w_gate[0], w_up[0], w_down[0]])
    h1, u2 = _outproj(y_hg, y_da, w_out_b, x, mod, ln1_g[0], ln1_b[0])
    return _ffn(u2, w_gate_b, w_up_b, w_down_b, h1, mod, ln2_g[0], ln2_b[0])
```

```python
import functools
import math

import numpy as np
import jax
import jax.numpy as jnp
from jax import lax
from jax.experimental import pallas as pl
from jax.experimental.pallas import tpu as pltpu

D_MODEL = 2048
HG_WIDTH = 1024
DA_WIDTH = 1024
HG_HEADS = 8
HG_DK = 128
HG_CHUNK = 32
DA_HEADS = 4
DA_HEAD_DIM = 128
ROPE_THETA = 10000.0
D_FF = 5632
DEPTH = 1
DEEPNORM_ALPHA = (2.0 * DEPTH) ** 0.25
LN_EPS = 1e-5
RMS_EPS = 1e-6
LAMBDA_INIT = 0.8 - 0.6 * math.exp(-0.3 * 0)
SEG = 1024
N_SEG = 7

VMEM_LIMIT = 56 * 1024 * 1024

F32 = jnp.float32
BF16 = jnp.bfloat16


def _dot(a, b):
    return jnp.dot(a, b, preferred_element_type=F32)


def _dot_nt(a, b):
    return lax.dot_general(a, b, (((1,), (1,)), ((), ())), preferred_element_type=F32)


def _dot_tn(a, b):
    return lax.dot_general(a, b, (((0,), (0,)), ((), ())), preferred_element_type=F32)


def _ln_plain(x):
    mu = jnp.mean(x, -1, keepdims=True)
    xc = x - mu
    var = jnp.mean(xc * xc, -1, keepdims=True)
    return xc * lax.rsqrt(var + LN_EPS)


def _run_interleaved(chains):
    pending = list(chains)
    results = {}
    while pending:
        for chain in list(pending):
            try:
                next(chain)
            except StopIteration as done:
                results[id(chain)] = done.value
                pending.remove(chain)
    return [results[id(chain)] for chain in chains]


def _cast_specs(mats, n_steps, step_of):
    specs, shapes = [], []
    for m in mats:
        rows = m.shape[0] // n_steps
        assert rows * n_steps == m.shape[0] and rows % 16 == 0, (m.shape, n_steps)
        specs.append(pl.BlockSpec((rows, m.shape[1]), lambda *idx: (step_of(*idx), 0)))
        shapes.append(jax.ShapeDtypeStruct(m.shape, BF16))
    return specs, shapes


def _adaln_kernel(c_ref, w_ref, b_ref, o_ref):
    cond = c_ref[...]
    cond = cond * jax.nn.sigmoid(cond)
    o_ref[...] = _dot(cond, w_ref[...]) + b_ref[...]


def _adaln(c, w, b):
    bsz, d = c.shape
    n = w.shape[1]
    tn = 1024
    return pl.pallas_call(
        _adaln_kernel,
        grid=(n // tn,),
        in_specs=[pl.BlockSpec((bsz, d), lambda j: (0, 0)),
                  pl.BlockSpec((d, tn), lambda j: (0, j)),
                  pl.BlockSpec((1, tn), lambda j: (0, j))],
        out_specs=pl.BlockSpec((bsz, tn), lambda j: (0, j)),
        out_shape=jax.ShapeDtypeStruct((bsz, n), F32),
        compiler_params=pltpu.CompilerParams(dimension_semantics=("arbitrary",),
                                             vmem_limit_bytes=VMEM_LIMIT),
        name="adaln",
    )(c, w, b.reshape(1, n))


def _rope_kernel(pos_ref, inv_ref, sign_ref, w_ref, cos_ref, sin_ref, wb_ref):
    wb_ref[...] = w_ref[...].astype(BF16)

    hs = pos_ref.shape[1] // 2
    lo = lax.broadcasted_iota(jnp.int32, (hs, DA_HEAD_DIM), 1) < DA_HEAD_DIM // 2
    ang = jnp.where(lo, pos_ref[0, :hs, :], pos_ref[0, hs:, :]) * inv_ref[...]
    for fn, ref, scale in ((jnp.cos, cos_ref, None), (jnp.sin, sin_ref, sign_ref[...])):
        val = fn(ang)
        swapped = pltpu.roll(val, DA_HEAD_DIM // 2, 1)
        top = jnp.where(lo, val, swapped)
        bottom = jnp.where(lo, swapped, val)
        ref[0, :hs, :] = top if scale is None else top * scale
        ref[0, hs:, :] = bottom if scale is None else bottom * scale


def _rope_tables(positions, to_cast):
    bsz, s = positions.shape
    ts = 2048
    half = DA_HEAD_DIM // 2
    inv = 1.0 / (ROPE_THETA ** (jnp.arange(0, DA_HEAD_DIM, 2, dtype=F32) / DA_HEAD_DIM))
    inv2 = jnp.concatenate([inv, inv]).reshape(1, DA_HEAD_DIM)
    sign = jnp.concatenate([-jnp.ones((half,), F32), jnp.ones((half,), F32)]).reshape(1, DA_HEAD_DIM)
    pos = positions.astype(F32).reshape(bsz, s, 1)
    tab = jax.ShapeDtypeStruct((bsz, s, DA_HEAD_DIM), F32)
    cast_specs, cast_shapes = _cast_specs([to_cast], bsz * (s // ts), lambda b, i: b * (s // ts) + i)
    return pl.pallas_call(
        _rope_kernel,
        grid=(bsz, s // ts),
        in_specs=[pl.BlockSpec((1, ts, 1), lambda b, i: (b, i, 0)),
                  pl.BlockSpec((1, DA_HEAD_DIM), lambda b, i: (0, 0)),
                  pl.BlockSpec((1, DA_HEAD_DIM), lambda b, i: (0, 0))] + cast_specs,
        out_specs=[pl.BlockSpec((1, ts, DA_HEAD_DIM), lambda b, i: (b, i, 0)),
                   pl.BlockSpec((1, ts, DA_HEAD_DIM), lambda b, i: (b, i, 0))] + cast_specs,
        out_shape=[tab, tab] + cast_shapes,
        compiler_params=pltpu.CompilerParams(dimension_semantics=("arbitrary", "arbitrary"),
                                             vmem_limit_bytes=VMEM_LIMIT),
        name="rope_tables",
    )(pos, inv2, sign, to_cast)


PROJ_ROWS = 512
SEG_ROWS = 1024


def _zero_after(x):
    bits = pltpu.bitcast(x, jnp.int32)
    acc = bits[0:8, :]
    for r in range(8, bits.shape[0], 8):
        acc = acc | bits[r:r + 8, :]
    word = acc[:, 0:128]
    for c in range(128, acc.shape[1], 128):
        word = word | acc[:, c:c + 128]
    word = lax.shift_right_logical(lax.shift_right_logical(word, 16), 16)
    return pltpu.bitcast(word, F32)[0:1, :].astype(BF16)


def _lagged_steps(matmul, finish, n_tiles):
    t = pl.program_id(0)

    @pl.when(t == 0)
    def _():
        matmul(0, None)

    for parity in (0, 1):
        @pl.when(jnp.logical_and(jnp.logical_and(t > 0, t < n_tiles), t % 2 == parity))
        def _():
            matmul(parity, finish(1 - parity))

    @pl.when(t == n_tiles)
    def _():
        finish((n_tiles - 1) % 2)


def _tile_maps(tiles_per_batch, n_tiles):
    def split(k):
        return k // tiles_per_batch, k % tiles_per_batch
    cur = lambda t: (*split(jnp.minimum(t, n_tiles - 1)), 0)
    lag = lambda t: (*split(jnp.maximum(t - 1, 0)), 0)
    lag_batch = lambda t: (jnp.maximum(t - 1, 0) // tiles_per_batch, 0, 0)
    return cur, lag, lag_batch


def _rope_apply(acc, cos, sin, scale):
    outs = []
    for h in range(SEG // DA_HEAD_DIM):
        xh = acc[:, h * DA_HEAD_DIM:(h + 1) * DA_HEAD_DIM]
        rot = pltpu.roll(xh, DA_HEAD_DIM // 2, 1)
        y = xh * cos + rot * sin
        if scale is not None:
            y = y * scale
        outs.append(y.astype(BF16))
    return jnp.concatenate(outs, axis=-1)


def _sigmoid_tanh(x):
    return 0.5 * jnp.tanh(0.5 * x) + 0.5


def _finish_hg_q(y, side, outs):
    q = (y * _sigmoid_tanh(y) * (HG_DK ** -0.5)).astype(BF16)
    outs[0][0] = q
    return _zero_after(q)


def _finish_hg_f(y, side, outs):
    lbl = side[0][...]
    e = jnp.exp(lbl - jnp.max(lbl, axis=0, keepdims=True))
    lb = e[0:1, :] / jnp.sum(e, axis=0, keepdims=True)
    f = lb + (1.0 - lb) * jax.nn.sigmoid(y)
    log_f = jnp.log(f)
    key = (1.0 - f).astype(BF16)
    outs[0][0] = log_f
    outs[1][0] = key
    return _zero_after(log_f) + _zero_after(key)


def _finish_cast(y, side, outs):
    outs[0][0] = y.astype(BF16)
    return None


def _finish_hg_g(y, side, outs):
    g = _sigmoid_tanh(y).astype(BF16)
    outs[0][0] = g
    return _zero_after(g)


def _finish_da_q(y, side, outs):
    q = _rope_apply(y, side[0][0], side[1][0], DA_HEAD_DIM ** -0.5 * math.log2(math.e))
    outs[0][0] = q
    return _zero_after(q)


def _finish_da_k(y, side, outs):
    k = _rope_apply(y, side[0][0], side[1][0], None)
    outs[0][0] = k
    return _zero_after(k)


def _segment_kernel(finish, n_side, n_tiles, u_ref, w_ref, *refs):
    side, outs, y_ref = refs[:n_side], refs[n_side:-1], refs[-1]

    half = SEG // 2

    def matmul(slot, zero):
        u = u_ref[0]
        y_ref[slot, :, :half] = _dot(u, w_ref[:, :half])
        if zero is not None:
            u = u + jnp.tile(zero, (1, u.shape[1] // zero.shape[1]))
        y_ref[slot, :, half:] = _dot(u, w_ref[:, half:])

    _lagged_steps(matmul, lambda slot: finish(y_ref[slot], side, outs), n_tiles)


def _segment(u, w_in, seg, finish, side, side_specs, out_dtypes, name):
    bsz, s, d = u.shape
    tm = SEG_ROWS
    n_tiles = bsz * (s // tm)
    cur, lag, _ = _tile_maps(s // tm, n_tiles)
    specs = [pl.BlockSpec((1, tm, d), cur), pl.BlockSpec((d, SEG), lambda t: (0, seg))]
    specs += [spec(lag) for spec in side_specs]
    return pl.pallas_call(
        functools.partial(_segment_kernel, finish, len(side), n_tiles),
        grid=(n_tiles + 1,),
        in_specs=specs,
        out_specs=[pl.BlockSpec((1, tm, SEG), lag) for _ in out_dtypes],
        out_shape=[jax.ShapeDtypeStruct((bsz, s, SEG), dt) for dt in out_dtypes],
        scratch_shapes=[pltpu.VMEM((2, tm, SEG), F32)],
        compiler_params=pltpu.CompilerParams(dimension_semantics=("arbitrary",),
                                             vmem_limit_bytes=VMEM_LIMIT),
        name=name,
    )(u, w_in, *side)


LN_CHUNK_ROWS = 256


def _first_segment_kernel(finish, n_tiles, x_ref, mod_ref, w_ref, u_ref, o_ref, y_ref):
    half = SEG // 2

    def matmul(slot, zero):
        m = mod_ref[0]
        for r0 in range(0, x_ref.shape[1], LN_CHUNK_ROWS):
            rows = slice(r0, r0 + LN_CHUNK_ROWS)
            u = (_ln_plain(x_ref[0, rows, :]) * (1.0 + m[1:2, :]) + m[0:1, :]).astype(BF16)
            u_ref[0, rows, :] = u
            y_ref[slot, rows, :half] = _dot(u, w_ref[:, :half])
            if zero is not None:
                u = u + jnp.tile(zero, (1, u.shape[1] // zero.shape[1]))
            y_ref[slot, rows, half:] = _dot(u, w_ref[:, half:])

    _lagged_steps(matmul, lambda slot: finish(y_ref[slot], (), (o_ref,)), n_tiles)


def _first_segment(x, mod, w_in, finish, name):
    bsz, s, d = x.shape
    tm = SEG_ROWS
    n_tiles = bsz * (s // tm)
    cur, lag, _ = _tile_maps(s // tm, n_tiles)
    cur_batch = lambda t: (jnp.minimum(t, n_tiles - 1) // (s // tm), 0, 0)
    return pl.pallas_call(
        functools.partial(_first_segment_kernel, finish, n_tiles),
        grid=(n_tiles + 1,),
        in_specs=[pl.BlockSpec((1, tm, d), cur), pl.BlockSpec((1, 6, d), cur_batch),
                  pl.BlockSpec((d, SEG), lambda t: (0, 0))],
        out_specs=[pl.BlockSpec((1, tm, d), cur), pl.BlockSpec((1, tm, SEG), lag)],
        out_shape=[jax.ShapeDtypeStruct((bsz, s, d), BF16), jax.ShapeDtypeStruct((bsz, s, SEG), BF16)],
        scratch_shapes=[pltpu.VMEM((2, tm, SEG), F32)],
        compiler_params=pltpu.CompilerParams(dimension_semantics=("arbitrary",),
                                             vmem_limit_bytes=VMEM_LIMIT),
        name=name,
    )(x, mod, w_in)


def _inproj(x, mod, w_in, lb_logits, cos, sin):
    tm = SEG_ROWS
    table = lambda lag: pl.BlockSpec((1, tm, DA_HEAD_DIM), lag)
    const = lambda lag: pl.BlockSpec(lb_logits.shape, lambda t: (0, 0))
    u, qs = _first_segment(x, mod, w_in, _finish_hg_q, "inproj_ln_hg_q")
    lf, ks = _segment(u, w_in, 1, _finish_hg_f, [lb_logits], [const], [F32, BF16], "inproj_hg_f")
    vs, = _segment(u, w_in, 2, _finish_cast, [], [], [BF16], "inproj_hg_i")
    gs, = _segment(u, w_in, 3, _finish_hg_g, [], [], [BF16], "inproj_hg_g")
    dq, = _segment(u, w_in, 4, _finish_da_q, [cos, sin], [table, table], [BF16], "inproj_da_q")
    dk, = _segment(u, w_in, 5, _finish_da_k, [cos, sin], [table, table], [BF16], "inproj_da_k")
    dv, = _segment(u, w_in, 6, _finish_cast, [], [], [BF16], "inproj_da_v")
    return qs, lf, ks, vs, gs, dq, dk, dv


HG_BLOCK = 256
HG_HEADS_PER_STEP = 8


def _hgrn_consts():
    L, C = HG_BLOCK, HG_CHUNK
    r = np.arange(L)[:, None]
    c = np.arange(L)[None, :]
    t_cum = (((r // C) == (c // C)) & (c <= r)).astype(np.float32)
    return jnp.asarray(t_cum, BF16), jnp.asarray(t_cum)


def _bcast_chunk_rows(x, row_in_chunk):
    L, C = HG_BLOCK, HG_CHUNK
    parts = [jnp.broadcast_to(x[c * C + row_in_chunk:c * C + row_in_chunk + 1, :], (C, x.shape[1]))
             for c in range(L // C)]
    return jnp.concatenate(parts, axis=0)


def _hgrn_head(q, lf, k, v, g, nw, tmat, maskf, st):
    L, C = HG_BLOCK, HG_CHUNK

    hi = lf.astype(BF16)
    r1 = lf - hi.astype(F32)
    mid = r1.astype(BF16)
    lo = (r1 - mid.astype(F32)).astype(BF16)
    gg = _dot(tmat, jnp.concatenate([hi, mid, lo], axis=1))
    yield
    g_cum = gg[:, :HG_DK] + gg[:, HG_DK:2 * HG_DK] + gg[:, 2 * HG_DK:]
    g_mid = _bcast_chunk_rows(g_cum, C // 2 - 1)
    g_last = _bcast_chunk_rows(g_cum, C - 1)

    qa = (q * jnp.exp(g_cum - g_mid)).astype(BF16)
    ka = (k * jnp.exp(g_mid - g_cum)).astype(BF16)
    qd = (q * jnp.exp(g_cum)).astype(BF16)
    kd = (k * jnp.exp(g_last - g_cum)).astype(BF16)
    dl = jnp.exp(g_last)

    a = _dot_nt(qa, ka)
    yield
    a = jnp.where(maskf > 0.0, a, 0.0).astype(BF16)
    o_intra = _dot(a, v)
    yield

    zeros = jnp.zeros((C, HG_DK), BF16)
    upds = []
    for p in range(L // (2 * C)):
        r0 = 2 * p * C
        rhs = jnp.concatenate(
            [jnp.concatenate([kd[r0:r0 + C], zeros], axis=1),
             jnp.concatenate([zeros, kd[r0 + C:r0 + 2 * C]], axis=1)], axis=0)
        u2 = _dot_tn(v[r0:r0 + 2 * C], rhs)
        upds += [u2[:, :HG_DK], u2[:, HG_DK:]]
    yield

    starts = []
    for c in range(L // C):
        starts.append(st.astype(BF16))
        st = st * dl[c * C:c * C + 1, :] + upds[c]

    outs = [o_intra[c * C:(c + 1) * C] + _dot_nt(qd[c * C:(c + 1) * C], starts[c])
            for c in range(L // C)]
    yield
    o = jnp.concatenate(outs, axis=0)
    ms = jnp.mean(o * o, -1, keepdims=True)
    y = o * lax.rsqrt(ms + RMS_EPS) * nw * g
    return y.astype(BF16), st


def _hgrn_kernel(q_ref, lf_ref, k_ref, v_ref, g_ref, nw_ref, tmat_ref, mask_ref, *refs):
    n_cast = (len(refs) - 2) // 2
    cast_in, o_ref, cast_out, st_ref = refs[:n_cast], refs[n_cast], refs[n_cast + 1:-1], refs[-1]

    @pl.when(pl.program_id(2) == 0)
    def _():
        st_ref[...] = jnp.zeros_like(st_ref)

    for src_ref, dst_ref in zip(cast_in, cast_out):
        dst_ref[...] = src_ref[...].astype(BF16)

    lanes = [slice(h * HG_DK, (h + 1) * HG_DK) for h in range(HG_HEADS_PER_STEP)]
    heads = [_hgrn_head(q_ref[0, :, ln].astype(F32), lf_ref[0, :, ln], k_ref[0, :, ln].astype(F32),
                        v_ref[0, :, ln], g_ref[0, :, ln].astype(F32), nw_ref[...], tmat_ref[...],
                        mask_ref[...], st_ref[h]) for h, ln in enumerate(lanes)]
    for h, (y, st) in enumerate(_run_interleaved(heads)):
        o_ref[0, :, lanes[h]] = y
        st_ref[h] = st


def _hgrn(qs, lf, ks, vs, gs, norm_w, to_cast):
    bsz, s, _ = qs.shape
    L = HG_BLOCK
    hp = HG_HEADS_PER_STEP
    tmat, mask = _hgrn_consts()
    blk = pl.BlockSpec((1, L, hp * HG_DK), lambda b, h, n: (b, n, h))
    const = lambda shape: pl.BlockSpec(shape, lambda b, h, n: (0, 0))
    heads, blocks = HG_HEADS // hp, s // L
    cast_specs, cast_shapes = _cast_specs(to_cast, bsz * heads * blocks,
                                          lambda b, h, n: (b * heads + h) * blocks + n)
    return pl.pallas_call(
        _hgrn_kernel,
        grid=(bsz, heads, blocks),
        in_specs=[blk, blk, blk, blk, blk, const((1, HG_DK)), const((L, L)), const((L, L))] + cast_specs,
        out_specs=[blk] + cast_specs,
        out_shape=[jax.ShapeDtypeStruct((bsz, s, HG_WIDTH), BF16)] + cast_shapes,
        scratch_shapes=[pltpu.VMEM((hp, HG_DK, HG_DK), F32)],
        compiler_params=pltpu.CompilerParams(
            dimension_semantics=("arbitrary", "arbitrary", "arbitrary"),
            vmem_limit_bytes=VMEM_LIMIT),
        name="hgrn",
    )(qs, lf, ks, vs, gs, norm_w.reshape(1, HG_DK), tmat, mask, *to_cast)


ATT_KEYS = 512
ATT_QUERIES = 1024
ATT_COLS = 256
ATT_ONES = 16


def _attn_scores(k_ref, q_t, n, s_ref, slot, col_blocks):
    TK, C, D = ATT_KEYS, ATT_COLS, DA_HEAD_DIM
    ncol = ATT_QUERIES // C
    kb = k_ref[0, pl.ds(pl.multiple_of(n * TK, TK), TK), :]
    for t in range(2):
        for c in col_blocks:
            s_ref[slot, t * ncol + c] = _dot(kb[:, t * D:(t + 1) * D],
                                             q_t[t * D:(t + 1) * D, c * C:(c + 1) * C])


def _attn_chain(s, v_t, acc_ref, m_ref, cols, first_visible):
    if first_visible is not None:
        key = lax.broadcasted_iota(jnp.int32, s.shape, 0)
        qry = lax.broadcasted_iota(jnp.int32, s.shape, 1) + first_visible
        s = jnp.where(key <= qry, s, jnp.finfo(F32).min)
    m_prev = m_ref[:, cols]
    m_new = jnp.maximum(m_prev, jnp.max(s, axis=0, keepdims=True))
    alpha = jnp.exp2(m_prev - m_new)
    p = jnp.exp2(s - m_new)
    m_ref[:, cols] = m_new
    pv = _dot(v_t, p.astype(BF16))
    yield
    acc_ref[:, cols] = alpha * acc_ref[:, cols] + pv


def _attn_kernel(q_ref, k_ref, v_ref, lq1_ref, lk1_ref, lq2_ref, lk2_ref, sw_ref, *refs):
    n_cast = (len(refs) - 7) // 2
    cast_in, o_ref, cast_out = refs[:n_cast], refs[n_cast], refs[n_cast + 1:2 * n_cast + 1]
    vt_ref, s_ref, acc1_ref, acc2_ref, m1_ref, m2_ref = refs[2 * n_cast + 1:]
    TK, TQ, C = ATT_KEYS, ATT_QUERIES, ATT_COLS
    DV = 2 * DA_HEAD_DIM
    ncol = TQ // C
    nq = q_ref.shape[1] // TQ
    every = tuple(range(ncol))
    plans = [[(c, min(TK, (c + 1) * C - b * TK), c * C - b * TK) for c in every if (c + 1) * C > b * TK]
             for b in range(TQ // TK)]

    for src_ref, dst_ref in zip(cast_in, cast_out):
        dst_ref[...] = src_ref[...].astype(BF16)

    for j in range(vt_ref.shape[0]):
        vt_ref[j, :DV, :] = v_ref[0, j * TK:(j + 1) * TK, :].T
        vt_ref[j, DV:, :] = jnp.ones((ATT_ONES, TK), BF16)

    lam = (jnp.exp(jnp.sum(lq1_ref[...] * lk1_ref[...], axis=-1, keepdims=True))
           - jnp.exp(jnp.sum(lq2_ref[...] * lk2_ref[...], axis=-1, keepdims=True))
           + LAMBDA_INIT)

    def queries_t(i):
        return q_ref[0, pl.ds(pl.multiple_of(i * TQ, TQ), TQ), :].T

    def query_block(i, carry):
        q_t = queries_t(i)
        acc1_ref[...] = jnp.zeros_like(acc1_ref)
        acc2_ref[...] = jnp.zeros_like(acc2_ref)
        m1_ref[...] = jnp.full_like(m1_ref, -jnp.inf)
        m2_ref[...] = jnp.full_like(m2_ref, -jnp.inf)

        def step(n, slot, plan, prefetch_cols):
            if prefetch_cols:
                _attn_scores(k_ref, q_t, n + 1, s_ref, 1 - slot, prefetch_cols)
            vt = vt_ref[n]
            chains = []
            for t, (acc_ref, m_ref) in enumerate(((acc1_ref, m1_ref), (acc2_ref, m2_ref))):
                for c, nk, offset in plan:
                    full = nk == TK and offset is not None and offset >= TK - 1
                    chains.append(_attn_chain(s_ref[slot, t * ncol + c, :nk, :], vt[:, :nk], acc_ref, m_ref,
                                              slice(c * C, (c + 1) * C), None if full else offset))
            _run_interleaved(chains)

        unmasked = [(c, TK, None) for c in every]

        def pair(t, carry):
            step(2 * t, 0, unmasked, every)
            step(2 * t + 1, 1, unmasked, every)
            return carry

        lax.fori_loop(0, i, pair, 0)
        step(2 * i, 0, plans[0], tuple(c for c, _, _ in plans[1]))
        step(2 * i + 1, 1, plans[1], ())

        _attn_scores(k_ref, queries_t(jnp.minimum(i + 1, nq - 1)), 0, s_ref, 0, every)

        inv1 = 1.0 / acc1_ref[DV:DV + 1, :]
        inv2 = lam / acc2_ref[DV:DV + 1, :]
        o_t = acc1_ref[:DV, :] * inv1 - acc2_ref[:DV, :] * inv2
        o = o_t.T
        ms = jnp.mean(o * o, -1, keepdims=True)
        y = o * lax.rsqrt(ms + RMS_EPS) * sw_ref[...] * (1.0 - LAMBDA_INIT)
        o_ref[0, pl.ds(pl.multiple_of(i * TQ, TQ), TQ), :] = y.astype(BF16)
        return carry

    _attn_scores(k_ref, queries_t(0), 0, s_ref, 0, every)
    lax.fori_loop(0, nq, query_block, 0)


def _attn(q, k, v, lq1, lk1, lq2, lk2, subln_w, to_cast):
    bsz, s, _ = q.shape
    TK, TQ = ATT_KEYS, ATT_QUERIES
    hw = 2 * DA_HEAD_DIM
    blk = pl.BlockSpec((1, s, hw), lambda b, h: (b, 0, h))
    vec = lambda n: pl.BlockSpec((1, n), lambda b, h: (0, 0))
    cast_specs, cast_shapes = _cast_specs(to_cast, bsz * DA_HEADS, lambda b, h: b * DA_HEADS + h)
    return pl.pallas_call(
        _attn_kernel,
        grid=(bsz, DA_HEADS),
        in_specs=[blk, blk, blk, vec(DA_HEAD_DIM), vec(DA_HEAD_DIM), vec(DA_HEAD_DIM),
                  vec(DA_HEAD_DIM), vec(hw)] + cast_specs,
        out_specs=[blk] + cast_specs,
        out_shape=[jax.ShapeDtypeStruct((bsz, s, DA_WIDTH), BF16)] + cast_shapes,
        scratch_shapes=[pltpu.VMEM((s // TK, hw + ATT_ONES, TK), BF16),
                        pltpu.VMEM((2, 2 * (TQ // ATT_COLS), TK, ATT_COLS), F32),
                        pltpu.VMEM((hw + ATT_ONES, TQ), F32), pltpu.VMEM((hw + ATT_ONES, TQ), F32),
                        pltpu.VMEM((1, TQ), F32), pltpu.VMEM((1, TQ), F32)],
        compiler_params=pltpu.CompilerParams(
            dimension_semantics=("arbitrary", "arbitrary"),
            vmem_limit_bytes=VMEM_LIMIT),
        name="diff_attn",
    )(q, k, v, lq1.reshape(1, -1), lk1.reshape(1, -1), lq2.reshape(1, -1), lk2.reshape(1, -1),
      subln_w.reshape(1, -1), *to_cast)


def _outproj_kernel(n_tiles, yh_ref, ya_ref, w_ref, x_ref, mod_ref, g_ref, b_ref, h_ref, u_ref, y_ref):
    half = w_ref.shape[1] // 2

    def matmul(slot, zero):
        yh = yh_ref[0]
        ya = ya_ref[0]
        y_ref[slot, :, :half] = _dot(yh, w_ref[:HG_WIDTH, :half]) + _dot(ya, w_ref[HG_WIDTH:, :half])
        if zero is not None:
            ya = ya + jnp.tile(zero, (1, ya.shape[1] // zero.shape[1]))
        y_ref[slot, :, half:] = _dot(yh, w_ref[:HG_WIDTH, half:]) + _dot(ya, w_ref[HG_WIDTH:, half:])

    def finish(slot):
        m = mod_ref[0]
        r = DEEPNORM_ALPHA * x_ref[0] + m[2:3, :] * y_ref[slot]
        h = _ln_plain(r) * g_ref[...] + b_ref[...]
        h_ref[0] = h
        u = (_ln_plain(h) * (1.0 + m[4:5, :]) + m[3:4, :]).astype(BF16)
        u_ref[0] = u
        return _zero_after(u)

    _lagged_steps(matmul, finish, n_tiles)


def _outproj(y_hg, y_da, w_out, x, mod, ln_g, ln_b):
    bsz, s, d = x.shape
    tm = PROJ_ROWS
    n_tiles = bsz * (s // tm)
    cur, lag, lag_batch = _tile_maps(s // tm, n_tiles)
    const = lambda shape: pl.BlockSpec(shape, lambda t: (0, 0))
    return pl.pallas_call(
        functools.partial(_outproj_kernel, n_tiles),
        grid=(n_tiles + 1,),
        in_specs=[pl.BlockSpec((1, tm, HG_WIDTH), cur),
                  pl.BlockSpec((1, tm, DA_WIDTH), cur),
                  const(w_out.shape),
                  pl.BlockSpec((1, tm, d), lag),
                  pl.BlockSpec((1, 6, d), lag_batch),
                  const((1, d)), const((1, d))],
        out_specs=[pl.BlockSpec((1, tm, d), lag), pl.BlockSpec((1, tm, d), lag)],
        out_shape=[jax.ShapeDtypeStruct((bsz, s, d), F32), jax.ShapeDtypeStruct((bsz, s, d), BF16)],
        scratch_shapes=[pltpu.VMEM((2, tm, d), F32)],
        compiler_params=pltpu.CompilerParams(dimension_semantics=("arbitrary",),
                                             vmem_limit_bytes=VMEM_LIMIT),
        name="outproj_ln1",
    )(y_hg, y_da, w_out, x, mod, ln_g.reshape(1, d), ln_b.reshape(1, d))


FFN_ROWS = 512
FFN_COLS = 512


def _ffn_kernel(u_ref, wg_ref, wu_ref, wd_ref, h_ref, mod_ref, g_ref, b_ref, o_ref, acc_ref):
    j = pl.program_id(2)

    @pl.when(j == 0)
    def _():
        acc_ref[...] = jnp.zeros_like(acc_ref)

    u = u_ref[0]
    a = _dot(u, wg_ref[...])
    b = _dot(u, wu_ref[...])
    z = (a * jax.nn.sigmoid(a) * b).astype(BF16)
    acc_ref[...] += _dot(z, wd_ref[...])

    @pl.when(j == pl.num_programs(2) - 1)
    def _():
        m = mod_ref[0]
        r = DEEPNORM_ALPHA * h_ref[0] + m[5:6, :] * acc_ref[...]
        o_ref[0] = _ln_plain(r) * g_ref[...] + b_ref[...]


def _ffn(u, w_gate, w_up, w_down, h, mod, ln_g, ln_b):
    bsz, s, d = h.shape
    f = w_gate.shape[1]
    tm, tf = FFN_ROWS, FFN_COLS
    row = lambda b, i, j: (b, i, 0)
    const3 = lambda shape: pl.BlockSpec(shape, lambda b, i, j: (0, 0))
    return pl.pallas_call(
        _ffn_kernel,
        grid=(bsz, s // tm, f // tf),
        in_specs=[pl.BlockSpec((1, tm, d), row),
                  pl.BlockSpec((d, tf), lambda b, i, j: (0, j)),
                  pl.BlockSpec((d, tf), lambda b, i, j: (0, j)),
                  pl.BlockSpec((tf, d), lambda b, i, j: (j, 0)),
                  pl.BlockSpec((1, tm, d), row),
                  pl.BlockSpec((1, 6, d), lambda b, i, j: (b, 0, 0)),
                  const3((1, d)), const3((1, d))],
        out_specs=pl.BlockSpec((1, tm, d), row),
        out_shape=jax.ShapeDtypeStruct((bsz, s, d), F32),
        scratch_shapes=[pltpu.VMEM((tm, d), F32)],
        compiler_params=pltpu.CompilerParams(
            dimension_semantics=("arbitrary", "arbitrary", "arbitrary"),
            vmem_limit_bytes=VMEM_LIMIT),
        name="ffn_ln2",
    )(u, w_gate, w_up, w_down, h, mod, ln_g.reshape(1, d), ln_b.reshape(1, d))


def kernel(x, c, positions, w_ada, b_ada, w_in, lb_logits, hg_norm_w, lam_q1, lam_k1, lam_q2, lam_k2,
           subln_w, w_out, ln1_g, ln1_b, w_gate, w_up, w_down, ln2_g, ln2_b):
    assert w_ada.shape[0] == DEPTH == 1
    bsz, s, d = x.shape

    mod = _adaln(c, w_ada[0], b_ada[0]).reshape(bsz, 6, d)
    cos, sin, w_in_b = _rope_tables(positions, w_in[0])

    qs, lf, ks, vs, gs, dq, dk, dv = _inproj(x, mod, w_in_b, lb_logits, cos, sin)
    f = w_down.shape[1]
    y_hg, w_out_b, w_gate_b, w_up_b, w_down_b = _hgrn(
        qs, lf, ks, vs, gs, hg_norm_w[0], [w_out[0], w_gate[0], w_up[0], w_down[0].reshape(d, f)])
    y_da, = _attn(dq, dk, dv, lam_q1[0], lam_k1[0], lam_q2[0], lam_k2[0], subln_w[0], [])
    h1, u2 = _outproj(y_hg, y_da, w_out_b, x, mod, ln1_g[0], ln1_b[0])
    return _ffn(u2, w_gate_b, w_up_b, w_down_b.reshape(f, d), h1, mod, ln2_g[0], ln2_b[0])
```

```python
import functools
import math

import numpy as np
import jax
import jax.numpy as jnp
from jax import lax
from jax.experimental import pallas as pl
from jax.experimental.pallas import tpu as pltpu

D_MODEL = 2048
HG_WIDTH = 1024
DA_WIDTH = 1024
HG_HEADS = 8
HG_DK = 128
HG_CHUNK = 32
DA_HEADS = 4
DA_HEAD_DIM = 128
ROPE_THETA = 10000.0
D_FF = 5632
DEPTH = 1
DEEPNORM_ALPHA = (2.0 * DEPTH) ** 0.25
LN_EPS = 1e-5
RMS_EPS = 1e-6
LAMBDA_INIT = 0.8 - 0.6 * math.exp(-0.3 * 0)
SEG = 1024
N_SEG = 7

VMEM_LIMIT = 56 * 1024 * 1024

F32 = jnp.float32
BF16 = jnp.bfloat16


def _dot(a, b):
    return jnp.dot(a, b, preferred_element_type=F32)


def _dot_nt(a, b):
    return lax.dot_general(a, b, (((1,), (1,)), ((), ())), preferred_element_type=F32)


def _dot_tn(a, b):
    return lax.dot_general(a, b, (((0,), (0,)), ((), ())), preferred_element_type=F32)


def _ln_plain(x):
    mu = jnp.mean(x, -1, keepdims=True)
    xc = x - mu
    var = jnp.mean(xc * xc, -1, keepdims=True)
    return xc * lax.rsqrt(var + LN_EPS)


def _run_interleaved(chains):
    pending = list(chains)
    results = {}
    while pending:
        for chain in list(pending):
            try:
                next(chain)
            except StopIteration as done:
                results[id(chain)] = done.value
                pending.remove(chain)
    return [results[id(chain)] for chain in chains]


def _cast_specs(mats, n_steps, step_of):
    specs, shapes = [], []
    for m in mats:
        rows = m.shape[0] // n_steps
        assert rows * n_steps == m.shape[0] and rows % 16 == 0, (m.shape, n_steps)
        specs.append(pl.BlockSpec((rows, m.shape[1]), lambda *idx: (step_of(*idx), 0)))
        shapes.append(jax.ShapeDtypeStruct(m.shape, BF16))
    return specs, shapes


def _adaln_kernel(c_ref, w_ref, b_ref, o_ref):
    cond = c_ref[...]
    cond = cond * jax.nn.sigmoid(cond)
    o_ref[...] = _dot(cond, w_ref[...]) + b_ref[...]


def _adaln(c, w, b):
    bsz, d = c.shape
    n = w.shape[1]
    tn = 1024
    return pl.pallas_call(
        _adaln_kernel,
        grid=(n // tn,),
        in_specs=[pl.BlockSpec((bsz, d), lambda j: (0, 0)),
                  pl.BlockSpec((d, tn), lambda j: (0, j)),
                  pl.BlockSpec((1, tn), lambda j: (0, j))],
        out_specs=pl.BlockSpec((bsz, tn), lambda j: (0, j)),
        out_shape=jax.ShapeDtypeStruct((bsz, n), F32),
        compiler_params=pltpu.CompilerParams(dimension_semantics=("arbitrary",),
                                             vmem_limit_bytes=VMEM_LIMIT),
        name="adaln",
    )(c, w, b.reshape(1, n))


def _rope_kernel(pos_ref, inv_ref, sign_ref, w_ref, cos_ref, sin_ref, wb_ref):
    wb_ref[...] = w_ref[...].astype(BF16)

    hs = pos_ref.shape[1] // 2
    lo = lax.broadcasted_iota(jnp.int32, (hs, DA_HEAD_DIM), 1) < DA_HEAD_DIM // 2
    ang = jnp.where(lo, pos_ref[0, :hs, :], pos_ref[0, hs:, :]) * inv_ref[...]
    for fn, ref, scale in ((jnp.cos, cos_ref, None), (jnp.sin, sin_ref, sign_ref[...])):
        val = fn(ang)
        swapped = pltpu.roll(val, DA_HEAD_DIM // 2, 1)
        top = jnp.where(lo, val, swapped)
        bottom = jnp.where(lo, swapped, val)
        ref[0, :hs, :] = top if scale is None else top * scale
        ref[0, hs:, :] = bottom if scale is None else bottom * scale


def _rope_tables(positions, to_cast):
    bsz, s = positions.shape
    ts = 2048
    half = DA_HEAD_DIM // 2
    inv = 1.0 / (ROPE_THETA ** (jnp.arange(0, DA_HEAD_DIM, 2, dtype=F32) / DA_HEAD_DIM))
    inv2 = jnp.concatenate([inv, inv]).reshape(1, DA_HEAD_DIM)
    sign = jnp.concatenate([-jnp.ones((half,), F32), jnp.ones((half,), F32)]).reshape(1, DA_HEAD_DIM)
    pos = positions.astype(F32).reshape(bsz, s, 1)
    tab = jax.ShapeDtypeStruct((bsz, s, DA_HEAD_DIM), F32)
    cast_specs, cast_shapes = _cast_specs([to_cast], bsz * (s // ts), lambda b, i: b * (s // ts) + i)
    return pl.pallas_call(
        _rope_kernel,
        grid=(bsz, s // ts),
        in_specs=[pl.BlockSpec((1, ts, 1), lambda b, i: (b, i, 0)),
                  pl.BlockSpec((1, DA_HEAD_DIM), lambda b, i: (0, 0)),
                  pl.BlockSpec((1, DA_HEAD_DIM), lambda b, i: (0, 0))] + cast_specs,
        out_specs=[pl.BlockSpec((1, ts, DA_HEAD_DIM), lambda b, i: (b, i, 0)),
                   pl.BlockSpec((1, ts, DA_HEAD_DIM), lambda b, i: (b, i, 0))] + cast_specs,
        out_shape=[tab, tab] + cast_shapes,
        compiler_params=pltpu.CompilerParams(dimension_semantics=("arbitrary", "arbitrary"),
                                             vmem_limit_bytes=VMEM_LIMIT),
        name="rope_tables",
    )(pos, inv2, sign, to_cast)


PROJ_ROWS = 512
SEG_ROWS = 1024


def _zero_after(x):
    bits = pltpu.bitcast(x, jnp.int32)
    acc = bits[0:8, :]
    for r in range(8, bits.shape[0], 8):
        acc = acc | bits[r:r + 8, :]
    word = acc[:, 0:128]
    for c in range(128, acc.shape[1], 128):
        word = word | acc[:, c:c + 128]
    word = lax.shift_right_logical(lax.shift_right_logical(word, 16), 16)
    return pltpu.bitcast(word, F32)[0:1, :].astype(BF16)


def _lagged_steps(matmul, finish, n_tiles):
    t = pl.program_id(0)

    @pl.when(t == 0)
    def _():
        matmul(0, None)

    for parity in (0, 1):
        @pl.when(jnp.logical_and(jnp.logical_and(t > 0, t < n_tiles), t % 2 == parity))
        def _():
            matmul(parity, finish(1 - parity))

    @pl.when(t == n_tiles)
    def _():
        finish((n_tiles - 1) % 2)


def _tile_maps(tiles_per_batch, n_tiles):
    def split(k):
        return k // tiles_per_batch, k % tiles_per_batch
    cur = lambda t: (*split(jnp.minimum(t, n_tiles - 1)), 0)
    lag = lambda t: (*split(jnp.maximum(t - 1, 0)), 0)
    lag_batch = lambda t: (jnp.maximum(t - 1, 0) // tiles_per_batch, 0, 0)
    return cur, lag, lag_batch


def _rope_apply(acc, cos, sin, scale):
    outs = []
    for h in range(SEG // DA_HEAD_DIM):
        xh = acc[:, h * DA_HEAD_DIM:(h + 1) * DA_HEAD_DIM]
        rot = pltpu.roll(xh, DA_HEAD_DIM // 2, 1)
        y = xh * cos + rot * sin
        if scale is not None:
            y = y * scale
        outs.append(y.astype(BF16))
    return jnp.concatenate(outs, axis=-1)


def _sigmoid_tanh(x):
    return 0.5 * jnp.tanh(0.5 * x) + 0.5


def _finish_hg_q(y, side, outs):
    q = (y * _sigmoid_tanh(y) * (HG_DK ** -0.5)).astype(BF16)
    outs[0][0] = q
    return _zero_after(q)


def _finish_hg_f(y, side, outs):
    lbl = side[0][...]
    e = jnp.exp(lbl - jnp.max(lbl, axis=0, keepdims=True))
    lb = e[0:1, :] / jnp.sum(e, axis=0, keepdims=True)
    f = lb + (1.0 - lb) * jax.nn.sigmoid(y)
    log_f = jnp.log(f)
    key = (1.0 - f).astype(BF16)
    outs[0][0] = log_f
    outs[1][0] = key
    return _zero_after(log_f) + _zero_after(key)


def _finish_cast(y, side, outs):
    outs[0][0] = y.astype(BF16)
    return None


def _finish_hg_g(y, side, outs):
    g = _sigmoid_tanh(y).astype(BF16)
    outs[0][0] = g
    return _zero_after(g)


def _finish_da_q(y, side, outs):
    q = _rope_apply(y, side[0][0], side[1][0], DA_HEAD_DIM ** -0.5 * math.log2(math.e))
    outs[0][0] = q
    return _zero_after(q)


def _finish_da_k(y, side, outs):
    k = _rope_apply(y, side[0][0], side[1][0], None)
    outs[0][0] = k
    return _zero_after(k)


def _segment_kernel(finish, n_side, n_tiles, u_ref, w_ref, *refs):
    side, outs, y_ref = refs[:n_side], refs[n_side:-1], refs[-1]

    half = SEG // 2

    def matmul(slot, zero):
        u = u_ref[0]
        y_ref[slot, :, :half] = _dot(u, w_ref[:, :half])
        if zero is not None:
            u = u + jnp.tile(zero, (1, u.shape[1] // zero.shape[1]))
        y_ref[slot, :, half:] = _dot(u, w_ref[:, half:])

    _lagged_steps(matmul, lambda slot: finish(y_ref[slot], side, outs), n_tiles)


def _segment(u, w_in, seg, finish, side, side_specs, out_dtypes, name):
    bsz, s, d = u.shape
    tm = SEG_ROWS
    n_tiles = bsz * (s // tm)
    cur, lag, _ = _tile_maps(s // tm, n_tiles)
    specs = [pl.BlockSpec((1, tm, d), cur), pl.BlockSpec((d, SEG), lambda t: (0, seg))]
    specs += [spec(lag) for spec in side_specs]
    return pl.pallas_call(
        functools.partial(_segment_kernel, finish, len(side), n_tiles),
        grid=(n_tiles + 1,),
        in_specs=specs,
        out_specs=[pl.BlockSpec((1, tm, SEG), lag) for _ in out_dtypes],
        out_shape=[jax.ShapeDtypeStruct((bsz, s, SEG), dt) for dt in out_dtypes],
        scratch_shapes=[pltpu.VMEM((2, tm, SEG), F32)],
        compiler_params=pltpu.CompilerParams(dimension_semantics=("arbitrary",),
                                             vmem_limit_bytes=VMEM_LIMIT),
        name=name,
    )(u, w_in, *side)


LN_CHUNK_ROWS = 256


def _first_segment_kernel(finish, n_tiles, x_ref, mod_ref, w_ref, u_ref, o_ref, y_ref):
    half = SEG // 2

    def matmul(slot, zero):
        m = mod_ref[0]
        for r0 in range(0, x_ref.shape[1], LN_CHUNK_ROWS):
            rows = slice(r0, r0 + LN_CHUNK_ROWS)
            u = (_ln_plain(x_ref[0, rows, :]) * (1.0 + m[1:2, :]) + m[0:1, :]).astype(BF16)
            u_ref[0, rows, :] = u
            y_ref[slot, rows, :half] = _dot(u, w_ref[:, :half])
            if zero is not None:
                u = u + jnp.tile(zero, (1, u.shape[1] // zero.shape[1]))
            y_ref[slot, rows, half:] = _dot(u, w_ref[:, half:])

    _lagged_steps(matmul, lambda slot: finish(y_ref[slot], (), (o_ref,)), n_tiles)


def _first_segment(x, mod, w_in, finish, name):
    bsz, s, d = x.shape
    tm = SEG_ROWS
    n_tiles = bsz * (s // tm)
    cur, lag, _ = _tile_maps(s // tm, n_tiles)
    cur_batch = lambda t: (jnp.minimum(t, n_tiles - 1) // (s // tm), 0, 0)
    return pl.pallas_call(
        functools.partial(_first_segment_kernel, finish, n_tiles),
        grid=(n_tiles + 1,),
        in_specs=[pl.BlockSpec((1, tm, d), cur), pl.BlockSpec((1, 6, d), cur_batch),
                  pl.BlockSpec((d, SEG), lambda t: (0, 0))],
        out_specs=[pl.BlockSpec((1, tm, d), cur), pl.BlockSpec((1, tm, SEG), lag)],
        out_shape=[jax.ShapeDtypeStruct((bsz, s, d), BF16), jax.ShapeDtypeStruct((bsz, s, SEG), BF16)],
        scratch_shapes=[pltpu.VMEM((2, tm, SEG), F32)],
        compiler_params=pltpu.CompilerParams(dimension_semantics=("arbitrary",),
                                             vmem_limit_bytes=VMEM_LIMIT),
        name=name,
    )(x, mod, w_in)


def _inproj(x, mod, w_in, lb_logits, cos, sin):
    tm = SEG_ROWS
    table = lambda lag: pl.BlockSpec((1, tm, DA_HEAD_DIM), lag)
    const = lambda lag: pl.BlockSpec(lb_logits.shape, lambda t: (0, 0))
    u, qs = _first_segment(x, mod, w_in, _finish_hg_q, "inproj_ln_hg_q")
    lf, ks = _segment(u, w_in, 1, _finish_hg_f, [lb_logits], [const], [F32, BF16], "inproj_hg_f")
    vs, = _segment(u, w_in, 2, _finish_cast, [], [], [BF16], "inproj_hg_i")
    gs, = _segment(u, w_in, 3, _finish_hg_g, [], [], [BF16], "inproj_hg_g")
    dq, = _segment(u, w_in, 4, _finish_da_q, [cos, sin], [table, table], [BF16], "inproj_da_q")
    dk, = _segment(u, w_in, 5, _finish_da_k, [cos, sin], [table, table], [BF16], "inproj_da_k")
    dv, = _segment(u, w_in, 6, _finish_cast, [], [], [BF16], "inproj_da_v")
    return qs, lf, ks, vs, gs, dq, dk, dv


HG_BLOCK = 256
HG_HEADS_PER_STEP = 8


def _hgrn_consts():
    L, C = HG_BLOCK, HG_CHUNK
    r = np.arange(L)[:, None]
    c = np.arange(L)[None, :]
    t_cum = (((r // C) == (c // C)) & (c <= r)).astype(np.float32)
    return jnp.asarray(t_cum, BF16), jnp.asarray(t_cum)


def _bcast_chunk_rows(x, row_in_chunk):
    L, C = HG_BLOCK, HG_CHUNK
    parts = [jnp.broadcast_to(x[c * C + row_in_chunk:c * C + row_in_chunk + 1, :], (C, x.shape[1]))
             for c in range(L // C)]
    return jnp.concatenate(parts, axis=0)


def _hgrn_head(q, lf, k, v, g, nw, tmat, maskf, st):
    L, C = HG_BLOCK, HG_CHUNK

    hi = lf.astype(BF16)
    r1 = lf - hi.astype(F32)
    mid = r1.astype(BF16)
    lo = (r1 - mid.astype(F32)).astype(BF16)
    gg = _dot(tmat, jnp.concatenate([hi, mid, lo], axis=1))
    yield
    g_cum = gg[:, :HG_DK] + gg[:, HG_DK:2 * HG_DK] + gg[:, 2 * HG_DK:]
    g_mid = _bcast_chunk_rows(g_cum, C // 2 - 1)
    g_last = _bcast_chunk_rows(g_cum, C - 1)

    qa = (q * jnp.exp(g_cum - g_mid)).astype(BF16)
    ka = (k * jnp.exp(g_mid - g_cum)).astype(BF16)
    qd = (q * jnp.exp(g_cum)).astype(BF16)
    kd = (k * jnp.exp(g_last - g_cum)).astype(BF16)
    dl = jnp.exp(g_last)

    a = _dot_nt(qa, ka)
    yield
    a = jnp.where(maskf > 0.0, a, 0.0).astype(BF16)
    o_intra = _dot(a, v)
    yield

    zeros = jnp.zeros((C, HG_DK), BF16)
    upds = []
    for p in range(L // (2 * C)):
        r0 = 2 * p * C
        rhs = jnp.concatenate(
            [jnp.concatenate([kd[r0:r0 + C], zeros], axis=1),
             jnp.concatenate([zeros, kd[r0 + C:r0 + 2 * C]], axis=1)], axis=0)
        u2 = _dot_tn(v[r0:r0 + 2 * C], rhs)
        upds += [u2[:, :HG_DK], u2[:, HG_DK:]]
    yield

    starts = []
    for c in range(L // C):
        starts.append(st.astype(BF16))
        st = st * dl[c * C:c * C + 1, :] + upds[c]

    outs = [o_intra[c * C:(c + 1) * C] + _dot_nt(qd[c * C:(c + 1) * C], starts[c])
            for c in range(L // C)]
    yield
    o = jnp.concatenate(outs, axis=0)
    ms = jnp.mean(o * o, -1, keepdims=True)
    y = o * lax.rsqrt(ms + RMS_EPS) * nw * g
    return y.astype(BF16), st


def _hgrn_kernel(q_ref, lf_ref, k_ref, v_ref, g_ref, nw_ref, tmat_ref, mask_ref, *refs):
    n_cast = (len(refs) - 2) // 2
    cast_in, o_ref, cast_out, st_ref = refs[:n_cast], refs[n_cast], refs[n_cast + 1:-1], refs[-1]

    @pl.when(pl.program_id(2) == 0)
    def _():
        st_ref[...] = jnp.zeros_like(st_ref)

    for src_ref, dst_ref in zip(cast_in, cast_out):
        dst_ref[...] = src_ref[...].astype(BF16)

    lanes = [slice(h * HG_DK, (h + 1) * HG_DK) for h in range(HG_HEADS_PER_STEP)]
    heads = [_hgrn_head(q_ref[0, :, ln].astype(F32), lf_ref[0, :, ln], k_ref[0, :, ln].astype(F32),
                        v_ref[0, :, ln], g_ref[0, :, ln].astype(F32), nw_ref[...], tmat_ref[...],
                        mask_ref[...], st_ref[h]) for h, ln in enumerate(lanes)]
    for h, (y, st) in enumerate(_run_interleaved(heads)):
        o_ref[0, :, lanes[h]] = y
        st_ref[h] = st


def _hgrn(qs, lf, ks, vs, gs, norm_w, to_cast):
    bsz, s, _ = qs.shape
    L = HG_BLOCK
    hp = HG_HEADS_PER_STEP
    tmat, mask = _hgrn_consts()
    blk = pl.BlockSpec((1, L, hp * HG_DK), lambda b, h, n: (b, n, h))
    const = lambda shape: pl.BlockSpec(shape, lambda b, h, n: (0, 0))
    heads, blocks = HG_HEADS // hp, s // L
    cast_specs, cast_shapes = _cast_specs(to_cast, bsz * heads * blocks,
                                          lambda b, h, n: (b * heads + h) * blocks + n)
    return pl.pallas_call(
        _hgrn_kernel,
        grid=(bsz, heads, blocks),
        in_specs=[blk, blk, blk, blk, blk, const((1, HG_DK)), const((L, L)), const((L, L))] + cast_specs,
        out_specs=[blk] + cast_specs,
        out_shape=[jax.ShapeDtypeStruct((bsz, s, HG_WIDTH), BF16)] + cast_shapes,
        scratch_shapes=[pltpu.VMEM((hp, HG_DK, HG_DK), F32)],
        compiler_params=pltpu.CompilerParams(
            dimension_semantics=("arbitrary", "arbitrary", "arbitrary"),
            vmem_limit_bytes=VMEM_LIMIT),
        name="hgrn",
    )(qs, lf, ks, vs, gs, norm_w.reshape(1, HG_DK), tmat, mask, *to_cast)


ATT_KEYS = 512
ATT_QUERIES = 1024
ATT_COLS = 256
ATT_ONES = 16


def _attn_scores(k_ref, q_t, n, s_ref, slot, col_blocks):
    TK, C, D = ATT_KEYS, ATT_COLS, DA_HEAD_DIM
    ncol = ATT_QUERIES // C
    kb = k_ref[0, pl.ds(pl.multiple_of(n * TK, TK), TK), :]
    for t in range(2):
        for c in col_blocks:
            s_ref[slot, t * ncol + c] = _dot(kb[:, t * D:(t + 1) * D],
                                             q_t[t * D:(t + 1) * D, c * C:(c + 1) * C])


def _attn_chain(s, v_t, acc_ref, m_ref, cols, first_visible):
    if first_visible is not None:
        key = lax.broadcasted_iota(jnp.int32, s.shape, 0)
        qry = lax.broadcasted_iota(jnp.int32, s.shape, 1) + first_visible
        s = jnp.where(key <= qry, s, jnp.finfo(F32).min)
    m_prev = m_ref[:, cols]
    m_new = jnp.maximum(m_prev, jnp.max(s, axis=0, keepdims=True))
    alpha = jnp.exp2(m_prev - m_new)
    p = jnp.exp2(s - m_new)
    m_ref[:, cols] = m_new
    pv = _dot(v_t, p.astype(BF16))
    yield
    acc_ref[:, cols] = alpha * acc_ref[:, cols] + pv


def _attn_kernel(q_ref, k_ref, v_ref, lq1_ref, lk1_ref, lq2_ref, lk2_ref, sw_ref, *refs):
    n_cast = (len(refs) - 7) // 2
    cast_in, o_ref, cast_out = refs[:n_cast], refs[n_cast], refs[n_cast + 1:2 * n_cast + 1]
    vt_ref, s_ref, acc1_ref, acc2_ref, m1_ref, m2_ref = refs[2 * n_cast + 1:]
    TK, TQ, C = ATT_KEYS, ATT_QUERIES, ATT_COLS
    DV = 2 * DA_HEAD_DIM
    ncol = TQ // C
    nq = q_ref.shape[1] // TQ
    every = tuple(range(ncol))
    plans = [[(c, min(TK, (c + 1) * C - b * TK), c * C - b * TK) for c in every if (c + 1) * C > b * TK]
             for b in range(TQ // TK)]

    for src_ref, dst_ref in zip(cast_in, cast_out):
        dst_ref[...] = src_ref[...].astype(BF16)

    for j in range(vt_ref.shape[0]):
        vt_ref[j, :DV, :] = v_ref[0, j * TK:(j + 1) * TK, :].T
        vt_ref[j, DV:, :] = jnp.ones((ATT_ONES, TK), BF16)

    lam = (jnp.exp(jnp.sum(lq1_ref[...] * lk1_ref[...], axis=-1, keepdims=True))
           - jnp.exp(jnp.sum(lq2_ref[...] * lk2_ref[...], axis=-1, keepdims=True))
           + LAMBDA_INIT)

    def queries_t(i):
        return q_ref[0, pl.ds(pl.multiple_of(i * TQ, TQ), TQ), :].T

    def query_block(i, carry):
        q_t = queries_t(i)
        acc1_ref[...] = jnp.zeros_like(acc1_ref)
        acc2_ref[...] = jnp.zeros_like(acc2_ref)
        m1_ref[...] = jnp.full_like(m1_ref, -jnp.inf)
        m2_ref[...] = jnp.full_like(m2_ref, -jnp.inf)

        def step(n, slot, plan, prefetch_cols):
            if prefetch_cols:
                _attn_scores(k_ref, q_t, n + 1, s_ref, 1 - slot, prefetch_cols)
            vt = vt_ref[n]
            chains = []
            for t, (acc_ref, m_ref) in enumerate(((acc1_ref, m1_ref), (acc2_ref, m2_ref))):
                for c, nk, offset in plan:
                    full = nk == TK and offset is not None and offset >= TK - 1
                    chains.append(_attn_chain(s_ref[slot, t * ncol + c, :nk, :], vt[:, :nk], acc_ref, m_ref,
                                              slice(c * C, (c + 1) * C), None if full else offset))
            _run_interleaved(chains)

        unmasked = [(c, TK, None) for c in every]

        def pair(t, carry):
            step(2 * t, 0, unmasked, every)
            step(2 * t + 1, 1, unmasked, every)
            return carry

        lax.fori_loop(0, i, pair, 0)
        step(2 * i, 0, plans[0], tuple(c for c, _, _ in plans[1]))
        step(2 * i + 1, 1, plans[1], ())

        _attn_scores(k_ref, queries_t(jnp.minimum(i + 1, nq - 1)), 0, s_ref, 0, every)

        inv1 = 1.0 / acc1_ref[DV:DV + 1, :]
        inv2 = lam / acc2_ref[DV:DV + 1, :]
        o_t = acc1_ref[:DV, :] * inv1 - acc2_ref[:DV, :] * inv2
        o = o_t.T
        ms = jnp.mean(o * o, -1, keepdims=True)
        y = o * lax.rsqrt(ms + RMS_EPS) * sw_ref[...] * (1.0 - LAMBDA_INIT)
        o_ref[0, pl.ds(pl.multiple_of(i * TQ, TQ), TQ), :] = y.astype(BF16)
        return carry

    _attn_scores(k_ref, queries_t(0), 0, s_ref, 0, every)
    lax.fori_loop(0, nq, query_block, 0)


def _attn(q, k, v, lq1, lk1, lq2, lk2, subln_w, to_cast):
    bsz, s, _ = q.shape
    TK, TQ = ATT_KEYS, ATT_QUERIES
    hw = 2 * DA_HEAD_DIM
    blk = pl.BlockSpec((1, s, hw), lambda b, h: (b, 0, h))
    vec = lambda n: pl.BlockSpec((1, n), lambda b, h: (0, 0))
    cast_specs, cast_shapes = _cast_specs(to_cast, bsz * DA_HEADS, lambda b, h: b * DA_HEADS + h)
    return pl.pallas_call(
        _attn_kernel,
        grid=(bsz, DA_HEADS),
        in_specs=[blk, blk, blk, vec(DA_HEAD_DIM), vec(DA_HEAD_DIM), vec(DA_HEAD_DIM),
                  vec(DA_HEAD_DIM), vec(hw)] + cast_specs,
        out_specs=[blk] + cast_specs,
        out_shape=[jax.ShapeDtypeStruct((bsz, s, DA_WIDTH), BF16)] + cast_shapes,
        scratch_shapes=[pltpu.VMEM((s // TK, hw + ATT_ONES, TK), BF16),
                        pltpu.VMEM((2, 2 * (TQ // ATT_COLS), TK, ATT_COLS), F32),
                        pltpu.VMEM((hw + ATT_ONES, TQ), F32), pltpu.VMEM((hw + ATT_ONES, TQ), F32),
                        pltpu.VMEM((1, TQ), F32), pltpu.VMEM((1, TQ), F32)],
        compiler_params=pltpu.CompilerParams(
            dimension_semantics=("arbitrary", "arbitrary"),
            vmem_limit_bytes=VMEM_LIMIT),
        name="diff_attn",
    )(q, k, v, lq1.reshape(1, -1), lk1.reshape(1, -1), lq2.reshape(1, -1), lk2.reshape(1, -1),
      subln_w.reshape(1, -1), *to_cast)


def _outproj_kernel(n_tiles, yh_ref, ya_ref, w_ref, x_ref, mod_ref, g_ref, b_ref, h_ref, u_ref, y_ref):
    half = w_ref.shape[1] // 2

    def matmul(slot, zero):
        yh = yh_ref[0]
        ya = ya_ref[0]
        y_ref[slot, :, :half] = _dot(yh, w_ref[:HG_WIDTH, :half]) + _dot(ya, w_ref[HG_WIDTH:, :half])
        if zero is not None:
            ya = ya + jnp.tile(zero, (1, ya.shape[1] // zero.shape[1]))
        y_ref[slot, :, half:] = _dot(yh, w_ref[:HG_WIDTH, half:]) + _dot(ya, w_ref[HG_WIDTH:, half:])

    def finish(slot):
        m = mod_ref[0]
        r = DEEPNORM_ALPHA * x_ref[0] + m[2:3, :] * y_ref[slot]
        h = _ln_plain(r) * g_ref[...] + b_ref[...]
        h_ref[0] = h
        u = (_ln_plain(h) * (1.0 + m[4:5, :]) + m[3:4, :]).astype(BF16)
        u_ref[0] = u
        return _zero_after(u)

    _lagged_steps(matmul, finish, n_tiles)


def _outproj(y_hg, y_da, w_out, x, mod, ln_g, ln_b):
    bsz, s, d = x.shape
    tm = PROJ_ROWS
    n_tiles = bsz * (s // tm)
    cur, lag, lag_batch = _tile_maps(s // tm, n_tiles)
    const = lambda shape: pl.BlockSpec(shape, lambda t: (0, 0))
    return pl.pallas_call(
        functools.partial(_outproj_kernel, n_tiles),
        grid=(n_tiles + 1,),
        in_specs=[pl.BlockSpec((1, tm, HG_WIDTH), cur),
                  pl.BlockSpec((1, tm, DA_WIDTH), cur),
                  const(w_out.shape),
                  pl.BlockSpec((1, tm, d), lag),
                  pl.BlockSpec((1, 6, d), lag_batch),
                  const((1, d)), const((1, d))],
        out_specs=[pl.BlockSpec((1, tm, d), lag), pl.BlockSpec((1, tm, d), lag)],
        out_shape=[jax.ShapeDtypeStruct((bsz, s, d), F32), jax.ShapeDtypeStruct((bsz, s, d), BF16)],
        scratch_shapes=[pltpu.VMEM((2, tm, d), F32)],
        compiler_params=pltpu.CompilerParams(dimension_semantics=("arbitrary",),
                                             vmem_limit_bytes=VMEM_LIMIT),
        name="outproj_ln1",
    )(y_hg, y_da, w_out, x, mod, ln_g.reshape(1, d), ln_b.reshape(1, d))


FFN_ROWS = 512
FFN_COLS = 512


def _ffn_kernel(u_ref, wg_ref, wu_ref, wd_ref, h_ref, mod_ref, g_ref, b_ref, o_ref, acc_ref):
    j = pl.program_id(2)

    @pl.when(j == 0)
    def _():
        acc_ref[...] = jnp.zeros_like(acc_ref)

    u = u_ref[0]
    a = _dot(u, wg_ref[...])
    b = _dot(u, wu_ref[...])
    z = (a * jax.nn.sigmoid(a) * b).astype(BF16)
    acc_ref[...] += _dot(z, wd_ref[...])

    @pl.when(j == pl.num_programs(2) - 1)
    def _():
        m = mod_ref[0]
        r = DEEPNORM_ALPHA * h_ref[0] + m[5:6, :] * acc_ref[...]
        o_ref[0] = _ln_plain(r) * g_ref[...] + b_ref[...]


def _ffn(u, w_gate, w_up, w_down, h, mod, ln_g, ln_b):
    bsz, s, d = h.shape
    f = w_gate.shape[1]
    tm, tf = FFN_ROWS, FFN_COLS
    row = lambda b, i, j: (b, i, 0)
    const3 = lambda shape: pl.BlockSpec(shape, lambda b, i, j: (0, 0))
    return pl.pallas_call(
        _ffn_kernel,
        grid=(bsz, s // tm, f // tf),
        in_specs=[pl.BlockSpec((1, tm, d), row),
                  pl.BlockSpec((d, tf), lambda b, i, j: (0, j)),
                  pl.BlockSpec((d, tf), lambda b, i, j: (0, j)),
                  pl.BlockSpec((tf, d), lambda b, i, j: (j, 0)),
                  pl.BlockSpec((1, tm, d), row),
                  pl.BlockSpec((1, 6, d), lambda b, i, j: (b, 0, 0)),
                  const3((1, d)), const3((1, d))],
        out_specs=pl.BlockSpec((1, tm, d), row),
        out_shape=jax.ShapeDtypeStruct((bsz, s, d), F32),
        scratch_shapes=[pltpu.VMEM((tm, d), F32)],
        compiler_params=pltpu.CompilerParams(
            dimension_semantics=("arbitrary", "arbitrary", "arbitrary"),
            vmem_limit_bytes=VMEM_LIMIT),
        name="ffn_ln2",
    )(u, w_gate, w_up, w_down, h, mod, ln_g.reshape(1, d), ln_b.reshape(1, d))


def kernel(x, c, positions, w_ada, b_ada, w_in, lb_logits, hg_norm_w, lam_q1, lam_k1, lam_q2, lam_k2,
           subln_w, w_out, ln1_g, ln1_b, w_gate, w_up, w_down, ln2_g, ln2_b):
    assert w_ada.shape[0] == DEPTH == 1
    bsz, s, d = x.shape

    mod = _adaln(c, w_ada[0], b_ada[0]).reshape(bsz, 6, d)
    cos, sin, w_in_b = _rope_tables(positions, w_in[0])

    qs, lf, ks, vs, gs, dq, dk, dv = _inproj(x, mod, w_in_b, lb_logits, cos, sin)
    y_hg, w_out_b, w_gate_b, w_up_b = _hgrn(qs, lf, ks, vs, gs, hg_norm_w[0],
                                            [w_out[0], w_gate[0], w_up[0]])
    y_da, w_down_b = _attn(dq, dk, dv, lam_q1[0], lam_k1[0], lam_q2[0], lam_k2[0], subln_w[0],
                           [w_down[0]])
    h1, u2 = _outproj(y_hg, y_da, w_out_b, x, mod, ln1_g[0], ln1_b[0])
    return _ffn(u2, w_gate_b, w_up_b, w_down_b, h1, mod, ln2_g[0], ln2_b[0])
```

```python
import functools
import math

import numpy as np
import jax
import jax.numpy as jnp
from jax import lax
from jax.experimental import pallas as pl
from jax.experimental.pallas import tpu as pltpu

HG_WIDTH = 1024
DA_WIDTH = 1024
HG_HEADS = 8
HG_DK = 128
HG_CHUNK = 32
DA_HEADS = 4
DA_HEAD_DIM = 128
ROPE_THETA = 10000.0
DEPTH = 1
DEEPNORM_ALPHA = (2.0 * DEPTH) ** 0.25
LN_EPS = 1e-5
RMS_EPS = 1e-6
LAMBDA_INIT = 0.8 - 0.6 * math.exp(-0.3 * 0)
SEG = 1024

VMEM_LIMIT = 56 * 1024 * 1024

F32 = jnp.float32
BF16 = jnp.bfloat16


def _dot(a, b):
    return jnp.dot(a, b, preferred_element_type=F32)


def _dot_nt(a, b):
    return lax.dot_general(a, b, (((1,), (1,)), ((), ())), preferred_element_type=F32)


def _dot_tn(a, b):
    return lax.dot_general(a, b, (((0,), (0,)), ((), ())), preferred_element_type=F32)


def _ln_plain(x):
    mu = jnp.mean(x, -1, keepdims=True)
    xc = x - mu
    var = jnp.mean(xc * xc, -1, keepdims=True)
    return xc * lax.rsqrt(var + LN_EPS)


def _run_interleaved(chains):
    pending = list(chains)
    results = {}
    while pending:
        for chain in list(pending):
            try:
                next(chain)
            except StopIteration as done:
                results[id(chain)] = done.value
                pending.remove(chain)
    return [results[id(chain)] for chain in chains]


def _cast_specs(mats, n_steps, step_of):
    specs, shapes = [], []
    for m in mats:
        rows = m.shape[0] // n_steps
        assert rows * n_steps == m.shape[0] and rows % 16 == 0, (m.shape, n_steps)
        specs.append(pl.BlockSpec((rows, m.shape[1]), lambda *idx: (step_of(*idx), 0)))
        shapes.append(jax.ShapeDtypeStruct(m.shape, BF16))
    return specs, shapes


def _adaln_kernel(c_ref, w_ref, b_ref, o_ref):
    cond = c_ref[...]
    cond = cond * jax.nn.sigmoid(cond)
    o_ref[...] = _dot(cond, w_ref[...]) + b_ref[...]


def _adaln(c, w, b):
    bsz, d = c.shape
    n = w.shape[1]
    tn = 1024
    return pl.pallas_call(
        _adaln_kernel,
        grid=(n // tn,),
        in_specs=[pl.BlockSpec((bsz, d), lambda j: (0, 0)),
                  pl.BlockSpec((d, tn), lambda j: (0, j)),
                  pl.BlockSpec((1, tn), lambda j: (0, j))],
        out_specs=pl.BlockSpec((bsz, tn), lambda j: (0, j)),
        out_shape=jax.ShapeDtypeStruct((bsz, n), F32),
        compiler_params=pltpu.CompilerParams(dimension_semantics=("arbitrary",),
                                             vmem_limit_bytes=VMEM_LIMIT),
        name="adaln",
    )(c, w, b.reshape(1, n))


def _rope_kernel(pos_ref, inv_ref, sign_ref, w_ref, cos_ref, sin_ref, wb_ref):
    wb_ref[...] = w_ref[...].astype(BF16)

    hs = pos_ref.shape[1] // 2
    lo = lax.broadcasted_iota(jnp.int32, (hs, DA_HEAD_DIM), 1) < DA_HEAD_DIM // 2
    ang = jnp.where(lo, pos_ref[0, :hs, :], pos_ref[0, hs:, :]) * inv_ref[...]
    for fn, ref, scale in ((jnp.cos, cos_ref, None), (jnp.sin, sin_ref, sign_ref[...])):
        val = fn(ang)
        swapped = pltpu.roll(val, DA_HEAD_DIM // 2, 1)
        top = jnp.where(lo, val, swapped)
        bottom = jnp.where(lo, swapped, val)
        ref[0, :hs, :] = top if scale is None else top * scale
        ref[0, hs:, :] = bottom if scale is None else bottom * scale


def _rope_tables(positions, to_cast):
    bsz, s = positions.shape
    ts = 2048
    half = DA_HEAD_DIM // 2
    inv = 1.0 / (ROPE_THETA ** (jnp.arange(0, DA_HEAD_DIM, 2, dtype=F32) / DA_HEAD_DIM))
    inv2 = jnp.concatenate([inv, inv]).reshape(1, DA_HEAD_DIM)
    sign = jnp.concatenate([-jnp.ones((half,), F32), jnp.ones((half,), F32)]).reshape(1, DA_HEAD_DIM)
    pos = positions.astype(F32).reshape(bsz, s, 1)
    tab = jax.ShapeDtypeStruct((bsz, s, DA_HEAD_DIM), F32)
    cast_specs, cast_shapes = _cast_specs([to_cast], bsz * (s // ts), lambda b, i: b * (s // ts) + i)
    return pl.pallas_call(
        _rope_kernel,
        grid=(bsz, s // ts),
        in_specs=[pl.BlockSpec((1, ts, 1), lambda b, i: (b, i, 0)),
                  pl.BlockSpec((1, DA_HEAD_DIM), lambda b, i: (0, 0)),
                  pl.BlockSpec((1, DA_HEAD_DIM), lambda b, i: (0, 0))] + cast_specs,
        out_specs=[pl.BlockSpec((1, ts, DA_HEAD_DIM), lambda b, i: (b, i, 0)),
                   pl.BlockSpec((1, ts, DA_HEAD_DIM), lambda b, i: (b, i, 0))] + cast_specs,
        out_shape=[tab, tab] + cast_shapes,
        compiler_params=pltpu.CompilerParams(dimension_semantics=("arbitrary", "arbitrary"),
                                             vmem_limit_bytes=VMEM_LIMIT),
        name="rope_tables",
    )(pos, inv2, sign, to_cast)


PROJ_ROWS = 512
SEG_ROWS = 1024


def _zero_after(x):
    bits = pltpu.bitcast(x, jnp.int32)
    acc = bits[0:8, :]
    for r in range(8, bits.shape[0], 8):
        acc = acc | bits[r:r + 8, :]
    word = acc[:, 0:128]
    for c in range(128, acc.shape[1], 128):
        word = word | acc[:, c:c + 128]
    word = lax.shift_right_logical(lax.shift_right_logical(word, 16), 16)
    return pltpu.bitcast(word, F32)[0:1, :].astype(BF16)


def _lagged_steps(matmul, finish, n_tiles):
    t = pl.program_id(0)

    @pl.when(t == 0)
    def _():
        matmul(0, None)

    for parity in (0, 1):
        @pl.when(jnp.logical_and(jnp.logical_and(t > 0, t < n_tiles), t % 2 == parity))
        def _():
            matmul(parity, finish(1 - parity))

    @pl.when(t == n_tiles)
    def _():
        finish((n_tiles - 1) % 2)


def _tile_maps(tiles_per_batch, n_tiles):
    def split(k):
        return k // tiles_per_batch, k % tiles_per_batch
    cur = lambda t: (*split(jnp.minimum(t, n_tiles - 1)), 0)
    lag = lambda t: (*split(jnp.maximum(t - 1, 0)), 0)
    lag_batch = lambda t: (jnp.maximum(t - 1, 0) // tiles_per_batch, 0, 0)
    return cur, lag, lag_batch


def _rope_apply(acc, cos, sin, scale):
    outs = []
    for h in range(SEG // DA_HEAD_DIM):
        xh = acc[:, h * DA_HEAD_DIM:(h + 1) * DA_HEAD_DIM]
        rot = pltpu.roll(xh, DA_HEAD_DIM // 2, 1)
        y = xh * cos + rot * sin
        if scale is not None:
            y = y * scale
        outs.append(y.astype(BF16))
    return jnp.concatenate(outs, axis=-1)


def _sigmoid_tanh(x):
    return 0.5 * jnp.tanh(0.5 * x) + 0.5


def _finish_hg_q(y, side, outs):
    q = (y * _sigmoid_tanh(y) * (HG_DK ** -0.5)).astype(BF16)
    outs[0][0] = q
    return _zero_after(q)


def _finish_hg_f(y, side, outs):
    lbl = side[0][...]
    e = jnp.exp(lbl - jnp.max(lbl, axis=0, keepdims=True))
    lb = e[0:1, :] / jnp.sum(e, axis=0, keepdims=True)
    f = lb + (1.0 - lb) * jax.nn.sigmoid(y)
    log_f = jnp.log2(f)
    key = (1.0 - f).astype(BF16)
    outs[0][0] = log_f
    outs[1][0] = key
    return _zero_after(log_f) + _zero_after(key)


def _finish_cast(y, side, outs):
    outs[0][0] = y.astype(BF16)
    return None


def _finish_hg_g(y, side, outs):
    g = _sigmoid_tanh(y).astype(BF16)
    outs[0][0] = g
    return _zero_after(g)


def _finish_da_q(y, side, outs):
    q = _rope_apply(y, side[0][0], side[1][0], DA_HEAD_DIM ** -0.5 * math.log2(math.e))
    outs[0][0] = q
    return _zero_after(q)


def _finish_da_k(y, side, outs):
    k = _rope_apply(y, side[0][0], side[1][0], None)
    outs[0][0] = k
    return _zero_after(k)


def _segment_kernel(finish, n_side, n_tiles, u_ref, w_ref, *refs):
    side, outs, y_ref = refs[:n_side], refs[n_side:-1], refs[-1]

    half = SEG // 2

    def matmul(slot, zero):
        u = u_ref[0]
        y_ref[slot, :, :half] = _dot(u, w_ref[:, :half])
        if zero is not None:
            u = u + jnp.tile(zero, (1, u.shape[1] // zero.shape[1]))
        y_ref[slot, :, half:] = _dot(u, w_ref[:, half:])

    _lagged_steps(matmul, lambda slot: finish(y_ref[slot], side, outs), n_tiles)


def _segment(u, w_in, seg, finish, side, side_specs, out_dtypes, name):
    bsz, s, d = u.shape
    tm = SEG_ROWS
    n_tiles = bsz * (s // tm)
    cur, lag, _ = _tile_maps(s // tm, n_tiles)
    specs = [pl.BlockSpec((1, tm, d), cur), pl.BlockSpec((d, SEG), lambda t: (0, seg))]
    specs += [spec(lag) for spec in side_specs]
    return pl.pallas_call(
        functools.partial(_segment_kernel, finish, len(side), n_tiles),
        grid=(n_tiles + 1,),
        in_specs=specs,
        out_specs=[pl.BlockSpec((1, tm, SEG), lag) for _ in out_dtypes],
        out_shape=[jax.ShapeDtypeStruct((bsz, s, SEG), dt) for dt in out_dtypes],
        scratch_shapes=[pltpu.VMEM((2, tm, SEG), F32)],
        compiler_params=pltpu.CompilerParams(dimension_semantics=("arbitrary",),
                                             vmem_limit_bytes=VMEM_LIMIT),
        name=name,
    )(u, w_in, *side)


LN_CHUNK_ROWS = 256


def _first_segment_kernel(finish, n_tiles, x_ref, mod_ref, w_ref, u_ref, o_ref, y_ref):
    half = SEG // 2

    def matmul(slot, zero):
        m = mod_ref[0]
        for r0 in range(0, x_ref.shape[1], LN_CHUNK_ROWS):
            rows = slice(r0, r0 + LN_CHUNK_ROWS)
            u = (_ln_plain(x_ref[0, rows, :]) * (1.0 + m[1:2, :]) + m[0:1, :]).astype(BF16)
            u_ref[0, rows, :] = u
            y_ref[slot, rows, :half] = _dot(u, w_ref[:, :half])
            if zero is not None:
                u = u + jnp.tile(zero, (1, u.shape[1] // zero.shape[1]))
            y_ref[slot, rows, half:] = _dot(u, w_ref[:, half:])

    _lagged_steps(matmul, lambda slot: finish(y_ref[slot], (), (o_ref,)), n_tiles)


def _first_segment(x, mod, w_in, finish, name):
    bsz, s, d = x.shape
    tm = SEG_ROWS
    n_tiles = bsz * (s // tm)
    cur, lag, _ = _tile_maps(s // tm, n_tiles)
    cur_batch = lambda t: (jnp.minimum(t, n_tiles - 1) // (s // tm), 0, 0)
    return pl.pallas_call(
        functools.partial(_first_segment_kernel, finish, n_tiles),
        grid=(n_tiles + 1,),
        in_specs=[pl.BlockSpec((1, tm, d), cur), pl.BlockSpec((1, 6, d), cur_batch),
                  pl.BlockSpec((d, SEG), lambda t: (0, 0))],
        out_specs=[pl.BlockSpec((1, tm, d), cur), pl.BlockSpec((1, tm, SEG), lag)],
        out_shape=[jax.ShapeDtypeStruct((bsz, s, d), BF16), jax.ShapeDtypeStruct((bsz, s, SEG), BF16)],
        scratch_shapes=[pltpu.VMEM((2, tm, SEG), F32)],
        compiler_params=pltpu.CompilerParams(dimension_semantics=("arbitrary",),
                                             vmem_limit_bytes=VMEM_LIMIT),
        name=name,
    )(x, mod, w_in)


def _inproj(x, mod, w_in, lb_logits, cos, sin):
    tm = SEG_ROWS
    table = lambda lag: pl.BlockSpec((1, tm, DA_HEAD_DIM), lag)
    const = lambda lag: pl.BlockSpec(lb_logits.shape, lambda t: (0, 0))
    u, qs = _first_segment(x, mod, w_in, _finish_hg_q, "inproj_ln_hg_q")
    lf, ks = _segment(u, w_in, 1, _finish_hg_f, [lb_logits], [const], [F32, BF16], "inproj_hg_f")
    vs, = _segment(u, w_in, 2, _finish_cast, [], [], [BF16], "inproj_hg_i")
    gs, = _segment(u, w_in, 3, _finish_hg_g, [], [], [BF16], "inproj_hg_g")
    dq, = _segment(u, w_in, 4, _finish_da_q, [cos, sin], [table, table], [BF16], "inproj_da_q")
    dk, = _segment(u, w_in, 5, _finish_da_k, [cos, sin], [table, table], [BF16], "inproj_da_k")
    dv, = _segment(u, w_in, 6, _finish_cast, [], [], [BF16], "inproj_da_v")
    return qs, lf, ks, vs, gs, dq, dk, dv


HG_BLOCK = 256
HG_HEADS_PER_STEP = 8


def _hgrn_consts():
    L, C = HG_BLOCK, HG_CHUNK
    r = np.arange(L)[:, None]
    c = np.arange(L)[None, :]
    t_cum = (((r // C) == (c // C)) & (c <= r)).astype(np.float32)
    return jnp.asarray(t_cum, BF16)


def _bcast_chunk_rows(x, row_in_chunk):
    L, C = HG_BLOCK, HG_CHUNK
    parts = [jnp.broadcast_to(x[c * C + row_in_chunk:c * C + row_in_chunk + 1, :], (C, x.shape[1]))
             for c in range(L // C)]
    return jnp.concatenate(parts, axis=0)


def _hgrn_head(q, lf, k, v, g, nw, tmat, st):
    L, C = HG_BLOCK, HG_CHUNK

    hi = lf.astype(BF16)
    r1 = lf - hi.astype(F32)
    mid = r1.astype(BF16)
    lo = (r1 - mid.astype(F32)).astype(BF16)
    gg = _dot(tmat, jnp.concatenate([hi, mid, lo], axis=1))
    yield
    g_cum = gg[:, :HG_DK] + gg[:, HG_DK:2 * HG_DK] + gg[:, 2 * HG_DK:]
    g_mid = _bcast_chunk_rows(g_cum, C // 2 - 1)
    g_last = _bcast_chunk_rows(g_cum, C - 1)

    qa = (q * jnp.exp2(g_cum - g_mid)).astype(BF16)
    ka = (k * jnp.exp2(g_mid - g_cum)).astype(BF16)
    qd = (q * jnp.exp2(g_cum)).astype(BF16)
    kd = (k * jnp.exp2(g_last - g_cum)).astype(BF16)
    dl = jnp.exp2(g_last)

    a = _dot_nt(qa, ka)
    yield
    a = a.astype(BF16)
    a = jnp.where(tmat > 0, a, jnp.zeros_like(a))
    o_intra = _dot(a, v)
    yield

    zeros = jnp.zeros((C, HG_DK), BF16)
    upds = []
    for p in range(L // (2 * C)):
        r0 = 2 * p * C
        rhs = jnp.concatenate(
            [jnp.concatenate([kd[r0:r0 + C], zeros], axis=1),
             jnp.concatenate([zeros, kd[r0 + C:r0 + 2 * C]], axis=1)], axis=0)
        u2 = _dot_tn(v[r0:r0 + 2 * C], rhs)
        upds += [u2[:, :HG_DK], u2[:, HG_DK:]]
    yield

    starts = []
    for c in range(L // C):
        starts.append(st.astype(BF16))
        st = st * dl[c * C:c * C + 1, :] + upds[c]

    outs = [o_intra[c * C:(c + 1) * C] + _dot_nt(qd[c * C:(c + 1) * C], starts[c])
            for c in range(L // C)]
    yield
    o = jnp.concatenate(outs, axis=0)
    ms = jnp.mean(o * o, -1, keepdims=True)
    y = o * lax.rsqrt(ms + RMS_EPS) * nw * g
    return y.astype(BF16), st


def _hgrn_kernel(q_ref, lf_ref, k_ref, v_ref, g_ref, nw_ref, tmat_ref, *refs):
    n_cast = (len(refs) - 2) // 2
    cast_in, o_ref, cast_out, st_ref = refs[:n_cast], refs[n_cast], refs[n_cast + 1:-1], refs[-1]

    @pl.when(pl.program_id(2) == 0)
    def _():
        st_ref[...] = jnp.zeros_like(st_ref)

    for src_ref, dst_ref in zip(cast_in, cast_out):
        dst_ref[...] = src_ref[...].astype(BF16)

    lanes = [slice(h * HG_DK, (h + 1) * HG_DK) for h in range(HG_HEADS_PER_STEP)]
    heads = [_hgrn_head(q_ref[0, :, ln].astype(F32), lf_ref[0, :, ln], k_ref[0, :, ln].astype(F32),
                        v_ref[0, :, ln], g_ref[0, :, ln].astype(F32), nw_ref[...], tmat_ref[...],
                        st_ref[h]) for h, ln in enumerate(lanes)]
    for h, (y, st) in enumerate(_run_interleaved(heads)):
        o_ref[0, :, lanes[h]] = y
        st_ref[h] = st


def _hgrn(qs, lf, ks, vs, gs, norm_w, to_cast):
    bsz, s, _ = qs.shape
    L = HG_BLOCK
    hp = HG_HEADS_PER_STEP
    tmat = _hgrn_consts()
    blk = pl.BlockSpec((1, L, hp * HG_DK), lambda b, h, n: (b, n, h))
    const = lambda shape: pl.BlockSpec(shape, lambda b, h, n: (0, 0))
    heads, blocks = HG_HEADS // hp, s // L
    cast_specs, cast_shapes = _cast_specs(to_cast, bsz * heads * blocks,
                                          lambda b, h, n: (b * heads + h) * blocks + n)
    return pl.pallas_call(
        _hgrn_kernel,
        grid=(bsz, heads, blocks),
        in_specs=[blk, blk, blk, blk, blk, const((1, HG_DK)), const((L, L))] + cast_specs,
        out_specs=[blk] + cast_specs,
        out_shape=[jax.ShapeDtypeStruct((bsz, s, HG_WIDTH), BF16)] + cast_shapes,
        scratch_shapes=[pltpu.VMEM((hp, HG_DK, HG_DK), F32)],
        compiler_params=pltpu.CompilerParams(
            dimension_semantics=("arbitrary", "arbitrary", "arbitrary"),
            vmem_limit_bytes=VMEM_LIMIT),
        name="hgrn",
    )(qs, lf, ks, vs, gs, norm_w.reshape(1, HG_DK), tmat, *to_cast)


ATT_KEYS = 512
ATT_QUERIES = 1024
ATT_COLS = 256
ATT_ONES = 16

def _attn_scores(k_ref, q_t, n, s_ref, slot, col_blocks):
    TK, C, D = ATT_KEYS, ATT_COLS, DA_HEAD_DIM
    ncol = ATT_QUERIES // C
    kb = k_ref[0, pl.ds(pl.multiple_of(n * TK, TK), TK), :]
    for t in range(2):
        for c in col_blocks:
            s_ref[slot, t * ncol + c] = _dot(kb[:, t * D:(t + 1) * D],
                                             q_t[t * D:(t + 1) * D, c * C:(c + 1) * C])


def _attn_chain(s, v_t, acc_ref, m_ref, cols, first_visible):
    if first_visible is not None:
        key = lax.broadcasted_iota(jnp.int32, s.shape, 0)
        qry = lax.broadcasted_iota(jnp.int32, s.shape, 1) + first_visible
        s = jnp.where(key <= qry, s, jnp.finfo(F32).min)
    m_prev = m_ref[:, cols]
    m_new = jnp.maximum(m_prev, jnp.max(s, axis=0, keepdims=True))
    alpha = jnp.exp2(m_prev - m_new)
    p = jnp.exp2(s - m_new)
    m_ref[:, cols] = m_new
    pv = _dot(v_t, p.astype(BF16))
    yield
    acc_ref[:, cols] = alpha * acc_ref[:, cols] + pv


def _attn_kernel(q_ref, k_ref, v_ref, lq1_ref, lk1_ref, lq2_ref, lk2_ref, sw_ref, *refs):
    n_cast = (len(refs) - 7) // 2
    cast_in, o_ref, cast_out = refs[:n_cast], refs[n_cast], refs[n_cast + 1:2 * n_cast + 1]
    vt_ref, s_ref, acc1_ref, acc2_ref, m1_ref, m2_ref = refs[2 * n_cast + 1:]
    TK, TQ, C = ATT_KEYS, ATT_QUERIES, ATT_COLS
    DV = 2 * DA_HEAD_DIM
    ncol = TQ // C
    nq = q_ref.shape[1] // TQ
    every = tuple(range(ncol))
    plans = [[(c, min(TK, (c + 1) * C - b * TK), c * C - b * TK) for c in every if (c + 1) * C > b * TK]
             for b in range(TQ // TK)]

    for src_ref, dst_ref in zip(cast_in, cast_out):
        dst_ref[...] = src_ref[...].astype(BF16)

    for j in range(vt_ref.shape[0]):
        vt_ref[j, :DV, :] = v_ref[0, j * TK:(j + 1) * TK, :].T
        vt_ref[j, DV:, :] = jnp.ones((ATT_ONES, TK), BF16)

    lam = (jnp.exp(jnp.sum(lq1_ref[...] * lk1_ref[...], axis=-1, keepdims=True))
           - jnp.exp(jnp.sum(lq2_ref[...] * lk2_ref[...], axis=-1, keepdims=True))
           + LAMBDA_INIT)

    def queries_t(i):
        return q_ref[0, pl.ds(pl.multiple_of(i * TQ, TQ), TQ), :].T

    def query_block(i, carry):
        q_t = queries_t(i)
        acc1_ref[...] = jnp.zeros_like(acc1_ref)
        acc2_ref[...] = jnp.zeros_like(acc2_ref)
        m1_ref[...] = jnp.full_like(m1_ref, -jnp.inf)
        m2_ref[...] = jnp.full_like(m2_ref, -jnp.inf)

        def step(n, slot, plan, prefetch_cols):
            if prefetch_cols:
                _attn_scores(k_ref, q_t, n + 1, s_ref, 1 - slot, prefetch_cols)
            vt = vt_ref[n]
            chains = []
            for t, (acc_ref, m_ref) in enumerate(((acc1_ref, m1_ref), (acc2_ref, m2_ref))):
                for c, nk, offset in plan:
                    full = nk == TK and offset is not None and offset >= TK - 1
                    chains.append(_attn_chain(s_ref[slot, t * ncol + c, :nk, :], vt[:, :nk], acc_ref, m_ref,
                                              slice(c * C, (c + 1) * C), None if full else offset))
            _run_interleaved(chains)

        unmasked = [(c, TK, None) for c in every]

        def pair(t, carry):
            step(2 * t, 0, unmasked, every)
            step(2 * t + 1, 1, unmasked, every)
            return carry

        ndiag = len(plans)
        lax.fori_loop(0, (ndiag // 2) * i, pair, 0)
        for b, plan in enumerate(plans):
            nxt = tuple(c for c, _, _ in plans[b + 1]) if b + 1 < ndiag else ()
            step(ndiag * i + b, b % 2, plan, nxt)

        _attn_scores(k_ref, queries_t(jnp.minimum(i + 1, nq - 1)), 0, s_ref, 0, every)

        inv1 = 1.0 / acc1_ref[DV:DV + 1, :]
        inv2 = lam / acc2_ref[DV:DV + 1, :]
        o_t = acc1_ref[:DV, :] * inv1 - acc2_ref[:DV, :] * inv2
        o = o_t.T
        ms = jnp.mean(o * o, -1, keepdims=True)
        y = o * lax.rsqrt(ms + RMS_EPS) * sw_ref[...] * (1.0 - LAMBDA_INIT)
        o_ref[0, pl.ds(pl.multiple_of(i * TQ, TQ), TQ), :] = y.astype(BF16)
        return carry

    _attn_scores(k_ref, queries_t(0), 0, s_ref, 0, every)
    lax.fori_loop(0, nq, query_block, 0)


def _attn(q, k, v, lq1, lk1, lq2, lk2, subln_w, to_cast):
    bsz, s, _ = q.shape
    TK, TQ = ATT_KEYS, ATT_QUERIES
    hw = 2 * DA_HEAD_DIM
    blk = pl.BlockSpec((1, s, hw), lambda b, h: (b, 0, h))
    vec = lambda n: pl.BlockSpec((1, n), lambda b, h: (0, 0))
    cast_specs, cast_shapes = _cast_specs(to_cast, bsz * DA_HEADS, lambda b, h: b * DA_HEADS + h)
    return pl.pallas_call(
        _attn_kernel,
        grid=(bsz, DA_HEADS),
        in_specs=[blk, blk, blk, vec(DA_HEAD_DIM), vec(DA_HEAD_DIM), vec(DA_HEAD_DIM),
                  vec(DA_HEAD_DIM), vec(hw)] + cast_specs,
        out_specs=[blk] + cast_specs,
        out_shape=[jax.ShapeDtypeStruct((bsz, s, DA_WIDTH), BF16)] + cast_shapes,
        scratch_shapes=[pltpu.VMEM((s // TK, hw + ATT_ONES, TK), BF16),
                        pltpu.VMEM((2, 2 * (TQ // ATT_COLS), TK, ATT_COLS), F32),
                        pltpu.VMEM((hw + ATT_ONES, TQ), F32), pltpu.VMEM((hw + ATT_ONES, TQ), F32),
                        pltpu.VMEM((1, TQ), F32), pltpu.VMEM((1, TQ), F32)],
        compiler_params=pltpu.CompilerParams(
            dimension_semantics=("arbitrary", "arbitrary"),
            vmem_limit_bytes=VMEM_LIMIT),
        name="diff_attn",
    )(q, k, v, lq1.reshape(1, -1), lk1.reshape(1, -1), lq2.reshape(1, -1), lk2.reshape(1, -1),
      subln_w.reshape(1, -1), *to_cast)


def _outproj_kernel(n_tiles, yh_ref, ya_ref, w_ref, x_ref, mod_ref, g_ref, b_ref, h_ref, u_ref, y_ref):
    half = w_ref.shape[1] // 2

    def matmul(slot, zero):
        yh = yh_ref[0]
        ya = ya_ref[0]
        y_ref[slot, :, :half] = _dot(yh, w_ref[:HG_WIDTH, :half]) + _dot(ya, w_ref[HG_WIDTH:, :half])
        if zero is not None:
            ya = ya + jnp.tile(zero, (1, ya.shape[1] // zero.shape[1]))
        y_ref[slot, :, half:] = _dot(yh, w_ref[:HG_WIDTH, half:]) + _dot(ya, w_ref[HG_WIDTH:, half:])

    def finish(slot):
        m = mod_ref[0]
        r = DEEPNORM_ALPHA * x_ref[0] + m[2:3, :] * y_ref[slot]
        h = _ln_plain(r) * g_ref[...] + b_ref[...]
        h_ref[0] = h
        u = (_ln_plain(h) * (1.0 + m[4:5, :]) + m[3:4, :]).astype(BF16)
        u_ref[0] = u
        return _zero_after(u)

    _lagged_steps(matmul, finish, n_tiles)


def _outproj(y_hg, y_da, w_out, x, mod, ln_g, ln_b):
    bsz, s, d = x.shape
    tm = PROJ_ROWS
    n_tiles = bsz * (s // tm)
    cur, lag, lag_batch = _tile_maps(s // tm, n_tiles)
    const = lambda shape: pl.BlockSpec(shape, lambda t: (0, 0))
    return pl.pallas_call(
        functools.partial(_outproj_kernel, n_tiles),
        grid=(n_tiles + 1,),
        in_specs=[pl.BlockSpec((1, tm, HG_WIDTH), cur),
                  pl.BlockSpec((1, tm, DA_WIDTH), cur),
                  const(w_out.shape),
                  pl.BlockSpec((1, tm, d), lag),
                  pl.BlockSpec((1, 6, d), lag_batch),
                  const((1, d)), const((1, d))],
        out_specs=[pl.BlockSpec((1, tm, d), lag), pl.BlockSpec((1, tm, d), lag)],
        out_shape=[jax.ShapeDtypeStruct((bsz, s, d), F32), jax.ShapeDtypeStruct((bsz, s, d), BF16)],
        scratch_shapes=[pltpu.VMEM((2, tm, d), F32)],
        compiler_params=pltpu.CompilerParams(dimension_semantics=("arbitrary",),
                                             vmem_limit_bytes=VMEM_LIMIT),
        name="outproj_ln1",
    )(y_hg, y_da, w_out, x, mod, ln_g.reshape(1, d), ln_b.reshape(1, d))


FFN_ROWS = 512
FFN_COLS = 512


def _ffn_kernel(u_ref, wg_ref, wu_ref, wd_ref, h_ref, mod_ref, g_ref, b_ref, o_ref, acc_ref):
    j = pl.program_id(2)

    @pl.when(j == 0)
    def _():
        acc_ref[...] = jnp.zeros_like(acc_ref)

    u = u_ref[0]
    a = _dot(u, wg_ref[...])
    b = _dot(u, wu_ref[...])
    z = (a * jax.nn.sigmoid(a) * b).astype(BF16)
    acc_ref[...] += _dot(z, wd_ref[...])

    @pl.when(j == pl.num_programs(2) - 1)
    def _():
        m = mod_ref[0]
        r = DEEPNORM_ALPHA * h_ref[0] + m[5:6, :] * acc_ref[...]
        o_ref[0] = _ln_plain(r) * g_ref[...] + b_ref[...]


def _ffn(u, w_gate, w_up, w_down, h, mod, ln_g, ln_b):
    bsz, s, d = h.shape
    f = w_gate.shape[1]
    tm, tf = FFN_ROWS, FFN_COLS
    row = lambda b, i, j: (b, i, 0)
    const3 = lambda shape: pl.BlockSpec(shape, lambda b, i, j: (0, 0))
    return pl.pallas_call(
        _ffn_kernel,
        grid=(bsz, s // tm, f // tf),
        in_specs=[pl.BlockSpec((1, tm, d), row),
                  pl.BlockSpec((d, tf), lambda b, i, j: (0, j)),
                  pl.BlockSpec((d, tf), lambda b, i, j: (0, j)),
                  pl.BlockSpec((tf, d), lambda b, i, j: (j, 0)),
                  pl.BlockSpec((1, tm, d), row),
                  pl.BlockSpec((1, 6, d), lambda b, i, j: (b, 0, 0)),
                  const3((1, d)), const3((1, d))],
        out_specs=pl.BlockSpec((1, tm, d), row),
        out_shape=jax.ShapeDtypeStruct((bsz, s, d), F32),
        scratch_shapes=[pltpu.VMEM((tm, d), F32)],
        compiler_params=pltpu.CompilerParams(
            dimension_semantics=("arbitrary", "arbitrary", "arbitrary"),
            vmem_limit_bytes=VMEM_LIMIT),
        name="ffn_ln2",
    )(u, w_gate, w_up, w_down, h, mod, ln_g.reshape(1, d), ln_b.reshape(1, d))


def kernel(x, c, positions, w_ada, b_ada, w_in, lb_logits, hg_norm_w, lam_q1, lam_k1, lam_q2, lam_k2,
           subln_w, w_out, ln1_g, ln1_b, w_gate, w_up, w_down, ln2_g, ln2_b):
    assert w_ada.shape[0] == DEPTH == 1
    bsz, s, d = x.shape

    mod = _adaln(c, w_ada[0], b_ada[0]).reshape(bsz, 6, d)
    cos, sin, w_in_b = _rope_tables(positions, w_in[0])

    qs, lf, ks, vs, gs, dq, dk, dv = _inproj(x, mod, w_in_b, lb_logits, cos, sin)
    y_hg, w_out_b, w_gate_b, w_up_b = _hgrn(qs, lf, ks, vs, gs, hg_norm_w[0],
                                            [w_out[0], w_gate[0], w_up[0]])
    y_da, w_down_b = _attn(dq, dk, dv, lam_q1[0], lam_k1[0], lam_q2[0], lam_k2[0], subln_w[0],
                           [w_down[0]])
    h1, u2 = _outproj(y_hg, y_da, w_out_b, x, mod, ln1_g[0], ln1_b[0])
    return _ffn(u2, w_gate_b, w_up_b, w_down_b, h1, mod, ln2_g[0], ln2_b[0])
```

```python
import functools
import math

import numpy as np
import jax
import jax.numpy as jnp
from jax import lax
from jax.experimental import pallas as pl
from jax.experimental.pallas import tpu as pltpu

HG_WIDTH = 1024
DA_WIDTH = 1024
HG_HEADS = 8
HG_DK = 128
HG_CHUNK = 32
DA_HEADS = 4
DA_HEAD_DIM = 128
ROPE_THETA = 10000.0
DEPTH = 1
DEEPNORM_ALPHA = (2.0 * DEPTH) ** 0.25
LN_EPS = 1e-5
RMS_EPS = 1e-6
LAMBDA_INIT = 0.8 - 0.6 * math.exp(-0.3 * 0)
SEG = 1024

VMEM_LIMIT = 56 * 1024 * 1024

F32 = jnp.float32
BF16 = jnp.bfloat16


def _dot(a, b):
    return jnp.dot(a, b, preferred_element_type=F32)


def _dot_nt(a, b):
    return lax.dot_general(a, b, (((1,), (1,)), ((), ())), preferred_element_type=F32)


def _dot_tn(a, b):
    return lax.dot_general(a, b, (((0,), (0,)), ((), ())), preferred_element_type=F32)


def _ln_plain(x):
    mu = jnp.mean(x, -1, keepdims=True)
    xc = x - mu
    var = jnp.mean(xc * xc, -1, keepdims=True)
    return xc * lax.rsqrt(var + LN_EPS)


def _run_interleaved(chains):
    pending = list(chains)
    results = {}
    while pending:
        for chain in list(pending):
            try:
                next(chain)
            except StopIteration as done:
                results[id(chain)] = done.value
                pending.remove(chain)
    return [results[id(chain)] for chain in chains]


def _cast_specs(mats, n_steps, step_of):
    specs, shapes = [], []
    for m in mats:
        rows = m.shape[0] // n_steps
        assert rows * n_steps == m.shape[0] and rows % 16 == 0, (m.shape, n_steps)
        specs.append(pl.BlockSpec((rows, m.shape[1]), lambda *idx: (step_of(*idx), 0)))
        shapes.append(jax.ShapeDtypeStruct(m.shape, BF16))
    return specs, shapes


def _adaln_kernel(c_ref, w_ref, b_ref, o_ref):
    cond = c_ref[...]
    cond = cond * jax.nn.sigmoid(cond)
    o_ref[...] = _dot(cond, w_ref[...]) + b_ref[...]


def _adaln(c, w, b):
    bsz, d = c.shape
    n = w.shape[1]
    tn = 1024
    return pl.pallas_call(
        _adaln_kernel,
        grid=(n // tn,),
        in_specs=[pl.BlockSpec((bsz, d), lambda j: (0, 0)),
                  pl.BlockSpec((d, tn), lambda j: (0, j)),
                  pl.BlockSpec((1, tn), lambda j: (0, j))],
        out_specs=pl.BlockSpec((bsz, tn), lambda j: (0, j)),
        out_shape=jax.ShapeDtypeStruct((bsz, n), F32),
        compiler_params=pltpu.CompilerParams(dimension_semantics=("arbitrary",),
                                             vmem_limit_bytes=VMEM_LIMIT),
        name="adaln",
    )(c, w, b.reshape(1, n))


def _rope_kernel(pos_ref, inv_ref, sign_ref, w_ref, cos_ref, sin_ref, wb_ref):
    wb_ref[...] = w_ref[...].astype(BF16)

    hs = pos_ref.shape[1] // 2
    lo = lax.broadcasted_iota(jnp.int32, (hs, DA_HEAD_DIM), 1) < DA_HEAD_DIM // 2
    ang = jnp.where(lo, pos_ref[0, :hs, :], pos_ref[0, hs:, :]) * inv_ref[...]
    for fn, ref, scale in ((jnp.cos, cos_ref, None), (jnp.sin, sin_ref, sign_ref[...])):
        val = fn(ang)
        swapped = pltpu.roll(val, DA_HEAD_DIM // 2, 1)
        top = jnp.where(lo, val, swapped)
        bottom = jnp.where(lo, swapped, val)
        ref[0, :hs, :] = top if scale is None else top * scale
        ref[0, hs:, :] = bottom if scale is None else bottom * scale


def _rope_tables(positions, to_cast):
    bsz, s = positions.shape
    ts = 2048
    half = DA_HEAD_DIM // 2
    inv = 1.0 / (ROPE_THETA ** (jnp.arange(0, DA_HEAD_DIM, 2, dtype=F32) / DA_HEAD_DIM))
    inv2 = jnp.concatenate([inv, inv]).reshape(1, DA_HEAD_DIM)
    sign = jnp.concatenate([-jnp.ones((half,), F32), jnp.ones((half,), F32)]).reshape(1, DA_HEAD_DIM)
    pos = positions.astype(F32).reshape(bsz, s, 1)
    tab = jax.ShapeDtypeStruct((bsz, s, DA_HEAD_DIM), F32)
    cast_specs, cast_shapes = _cast_specs([to_cast], bsz * (s // ts), lambda b, i: b * (s // ts) + i)
    return pl.pallas_call(
        _rope_kernel,
        grid=(bsz, s // ts),
        in_specs=[pl.BlockSpec((1, ts, 1), lambda b, i: (b, i, 0)),
                  pl.BlockSpec((1, DA_HEAD_DIM), lambda b, i: (0, 0)),
                  pl.BlockSpec((1, DA_HEAD_DIM), lambda b, i: (0, 0))] + cast_specs,
        out_specs=[pl.BlockSpec((1, ts, DA_HEAD_DIM), lambda b, i: (b, i, 0)),
                   pl.BlockSpec((1, ts, DA_HEAD_DIM), lambda b, i: (b, i, 0))] + cast_specs,
        out_shape=[tab, tab] + cast_shapes,
        compiler_params=pltpu.CompilerParams(dimension_semantics=("arbitrary", "arbitrary"),
                                             vmem_limit_bytes=VMEM_LIMIT),
        name="rope_tables",
    )(pos, inv2, sign, to_cast)


PROJ_ROWS = 512
SEG_ROWS = 1024


def _zero_after(x):
    bits = pltpu.bitcast(x, jnp.int32)
    acc = bits[0:8, :]
    for r in range(8, bits.shape[0], 8):
        acc = acc | bits[r:r + 8, :]
    word = acc[:, 0:128]
    for c in range(128, acc.shape[1], 128):
        word = word | acc[:, c:c + 128]
    word = lax.shift_right_logical(lax.shift_right_logical(word, 16), 16)
    return pltpu.bitcast(word, F32)[0:1, :].astype(BF16)


def _lagged_steps(matmul, finish, n_tiles):
    t = pl.program_id(0)

    @pl.when(t == 0)
    def _():
        matmul(0, None)

    for parity in (0, 1):
        @pl.when(jnp.logical_and(jnp.logical_and(t > 0, t < n_tiles), t % 2 == parity))
        def _():
            matmul(parity, finish(1 - parity))

    @pl.when(t == n_tiles)
    def _():
        finish((n_tiles - 1) % 2)


def _tile_maps(tiles_per_batch, n_tiles):
    def split(k):
        return k // tiles_per_batch, k % tiles_per_batch
    cur = lambda t: (*split(jnp.minimum(t, n_tiles - 1)), 0)
    lag = lambda t: (*split(jnp.maximum(t - 1, 0)), 0)
    lag_batch = lambda t: (jnp.maximum(t - 1, 0) // tiles_per_batch, 0, 0)
    return cur, lag, lag_batch


def _rope_apply(acc, cos, sin, scale):
    outs = []
    for h in range(SEG // DA_HEAD_DIM):
        xh = acc[:, h * DA_HEAD_DIM:(h + 1) * DA_HEAD_DIM]
        rot = pltpu.roll(xh, DA_HEAD_DIM // 2, 1)
        y = xh * cos + rot * sin
        if scale is not None:
            y = y * scale
        outs.append(y.astype(BF16))
    return jnp.concatenate(outs, axis=-1)


def _sigmoid_tanh(x):
    return 0.5 * jnp.tanh(0.5 * x) + 0.5


def _finish_hg_q(y, side, outs):
    q = (y * _sigmoid_tanh(y) * (HG_DK ** -0.5)).astype(BF16)
    outs[0][0] = q
    return _zero_after(q)


def _finish_hg_f(y, side, outs):
    lbl = side[0][...]
    e = jnp.exp(lbl - jnp.max(lbl, axis=0, keepdims=True))
    lb = e[0:1, :] / jnp.sum(e, axis=0, keepdims=True)
    f = lb + (1.0 - lb) * jax.nn.sigmoid(y)
    log_f = jnp.log2(f)
    key = (1.0 - f).astype(BF16)
    outs[0][0] = log_f
    outs[1][0] = key
    return _zero_after(log_f) + _zero_after(key)


def _finish_cast(y, side, outs):
    outs[0][0] = y.astype(BF16)
    return None


def _finish_hg_g(y, side, outs):
    g = _sigmoid_tanh(y).astype(BF16)
    outs[0][0] = g
    return _zero_after(g)


def _finish_da_q(y, side, outs):
    q = _rope_apply(y, side[0][0], side[1][0], DA_HEAD_DIM ** -0.5 * math.log2(math.e))
    outs[0][0] = q
    return _zero_after(q)


def _finish_da_k(y, side, outs):
    k = _rope_apply(y, side[0][0], side[1][0], None)
    outs[0][0] = k
    return _zero_after(k)


def _segment_kernel(finish, n_side, n_tiles, u_ref, w_ref, *refs):
    side, outs, y_ref = refs[:n_side], refs[n_side:-1], refs[-1]

    half = SEG // 2

    def matmul(slot, zero):
        u = u_ref[0]
        y_ref[slot, :, :half] = _dot(u, w_ref[:, :half])
        if zero is not None:
            u = u + jnp.tile(zero, (1, u.shape[1] // zero.shape[1]))
        y_ref[slot, :, half:] = _dot(u, w_ref[:, half:])

    _lagged_steps(matmul, lambda slot: finish(y_ref[slot], side, outs), n_tiles)


def _segment(u, w_in, seg, finish, side, side_specs, out_dtypes, name):
    bsz, s, d = u.shape
    tm = SEG_ROWS
    n_tiles = bsz * (s // tm)
    cur, lag, _ = _tile_maps(s // tm, n_tiles)
    specs = [pl.BlockSpec((1, tm, d), cur), pl.BlockSpec((d, SEG), lambda t: (0, seg))]
    specs += [spec(lag) for spec in side_specs]
    return pl.pallas_call(
        functools.partial(_segment_kernel, finish, len(side), n_tiles),
        grid=(n_tiles + 1,),
        in_specs=specs,
        out_specs=[pl.BlockSpec((1, tm, SEG), lag) for _ in out_dtypes],
        out_shape=[jax.ShapeDtypeStruct((bsz, s, SEG), dt) for dt in out_dtypes],
        scratch_shapes=[pltpu.VMEM((2, tm, SEG), F32)],
        compiler_params=pltpu.CompilerParams(dimension_semantics=("arbitrary",),
                                             vmem_limit_bytes=VMEM_LIMIT),
        name=name,
    )(u, w_in, *side)


LN_CHUNK_ROWS = 256


def _first_segment_kernel(finish, n_tiles, x_ref, mod_ref, w_ref, u_ref, o_ref, y_ref):
    half = SEG // 2

    def matmul(slot, zero):
        m = mod_ref[0]
        for r0 in range(0, x_ref.shape[1], LN_CHUNK_ROWS):
            rows = slice(r0, r0 + LN_CHUNK_ROWS)
            u = (_ln_plain(x_ref[0, rows, :]) * (1.0 + m[1:2, :]) + m[0:1, :]).astype(BF16)
            u_ref[0, rows, :] = u
            y_ref[slot, rows, :half] = _dot(u, w_ref[:, :half])
            if zero is not None:
                u = u + jnp.tile(zero, (1, u.shape[1] // zero.shape[1]))
            y_ref[slot, rows, half:] = _dot(u, w_ref[:, half:])

    _lagged_steps(matmul, lambda slot: finish(y_ref[slot], (), (o_ref,)), n_tiles)


def _first_segment(x, mod, w_in, finish, name):
    bsz, s, d = x.shape
    tm = SEG_ROWS
    n_tiles = bsz * (s // tm)
    cur, lag, _ = _tile_maps(s // tm, n_tiles)
    cur_batch = lambda t: (jnp.minimum(t, n_tiles - 1) // (s // tm), 0, 0)
    return pl.pallas_call(
        functools.partial(_first_segment_kernel, finish, n_tiles),
        grid=(n_tiles + 1,),
        in_specs=[pl.BlockSpec((1, tm, d), cur), pl.BlockSpec((1, 6, d), cur_batch),
                  pl.BlockSpec((d, SEG), lambda t: (0, 0))],
        out_specs=[pl.BlockSpec((1, tm, d), cur), pl.BlockSpec((1, tm, SEG), lag)],
        out_shape=[jax.ShapeDtypeStruct((bsz, s, d), BF16), jax.ShapeDtypeStruct((bsz, s, SEG), BF16)],
        scratch_shapes=[pltpu.VMEM((2, tm, SEG), F32)],
        compiler_params=pltpu.CompilerParams(dimension_semantics=("arbitrary",),
                                             vmem_limit_bytes=VMEM_LIMIT),
        name=name,
    )(x, mod, w_in)


def _inproj(x, mod, w_in, lb_logits, cos, sin):
    tm = SEG_ROWS
    table = lambda lag: pl.BlockSpec((1, tm, DA_HEAD_DIM), lag)
    const = lambda lag: pl.BlockSpec(lb_logits.shape, lambda t: (0, 0))
    u, qs = _first_segment(x, mod, w_in, _finish_hg_q, "inproj_ln_hg_q")
    lf, ks = _segment(u, w_in, 1, _finish_hg_f, [lb_logits], [const], [F32, BF16], "inproj_hg_f")
    vs, = _segment(u, w_in, 2, _finish_cast, [], [], [BF16], "inproj_hg_i")
    gs, = _segment(u, w_in, 3, _finish_hg_g, [], [], [BF16], "inproj_hg_g")
    dq, = _segment(u, w_in, 4, _finish_da_q, [cos, sin], [table, table], [BF16], "inproj_da_q")
    dk, = _segment(u, w_in, 5, _finish_da_k, [cos, sin], [table, table], [BF16], "inproj_da_k")
    dv, = _segment(u, w_in, 6, _finish_cast, [], [], [BF16], "inproj_da_v")
    return qs, lf, ks, vs, gs, dq, dk, dv


HG_BLOCK = 256
HG_HEADS_PER_STEP = 8


def _hgrn_consts():
    L, C = HG_BLOCK, HG_CHUNK
    r = np.arange(L)[:, None]
    c = np.arange(L)[None, :]
    t_cum = (((r // C) == (c // C)) & (c <= r)).astype(np.float32)
    return jnp.asarray(t_cum, BF16)


def _bcast_chunk_rows(x, row_in_chunk):
    L, C = HG_BLOCK, HG_CHUNK
    parts = [jnp.broadcast_to(x[c * C + row_in_chunk:c * C + row_in_chunk + 1, :], (C, x.shape[1]))
             for c in range(L // C)]
    return jnp.concatenate(parts, axis=0)


def _hgrn_head(q, lf, k, v, g, nw, tmat, st):
    L, C = HG_BLOCK, HG_CHUNK

    hi = lf.astype(BF16)
    r1 = lf - hi.astype(F32)
    mid = r1.astype(BF16)
    lo = (r1 - mid.astype(F32)).astype(BF16)
    gg = _dot(tmat, jnp.concatenate([hi, mid, lo], axis=1))
    yield
    g_cum = gg[:, :HG_DK] + gg[:, HG_DK:2 * HG_DK] + gg[:, 2 * HG_DK:]
    g_mid = _bcast_chunk_rows(g_cum, C // 2 - 1)
    g_last = _bcast_chunk_rows(g_cum, C - 1)

    qa = (q * jnp.exp2(g_cum - g_mid)).astype(BF16)
    ka = (k * jnp.exp2(g_mid - g_cum)).astype(BF16)
    qd = (q * jnp.exp2(g_cum)).astype(BF16)
    kd = (k * jnp.exp2(g_last - g_cum)).astype(BF16)
    dl = jnp.exp2(g_last)

    a = _dot_nt(qa, ka)
    yield
    a = a.astype(BF16)
    a = jnp.where(tmat > 0, a, jnp.zeros_like(a))
    o_intra = _dot(a, v)
    yield

    zeros = jnp.zeros((C, HG_DK), BF16)
    upds = []
    for p in range(L // (2 * C)):
        r0 = 2 * p * C
        rhs = jnp.concatenate(
            [jnp.concatenate([kd[r0:r0 + C], zeros], axis=1),
             jnp.concatenate([zeros, kd[r0 + C:r0 + 2 * C]], axis=1)], axis=0)
        u2 = _dot_tn(v[r0:r0 + 2 * C], rhs)
        upds += [u2[:, :HG_DK], u2[:, HG_DK:]]
    yield

    starts = []
    for c in range(L // C):
        starts.append(st.T.astype(BF16))
        st = st * dl[c * C:c * C + 1, :] + upds[c]

    outs = [o_intra[c * C:(c + 1) * C] + _dot(qd[c * C:(c + 1) * C], starts[c])
            for c in range(L // C)]
    yield
    o = jnp.concatenate(outs, axis=0)
    ms = jnp.mean(o * o, -1, keepdims=True)
    y = o * lax.rsqrt(ms + RMS_EPS) * nw * g
    return y.astype(BF16), st


def _hgrn_kernel(q_ref, lf_ref, k_ref, v_ref, g_ref, nw_ref, tmat_ref, *refs):
    n_cast = (len(refs) - 2) // 2
    cast_in, o_ref, cast_out, st_ref = refs[:n_cast], refs[n_cast], refs[n_cast + 1:-1], refs[-1]

    @pl.when(pl.program_id(2) == 0)
    def _():
        st_ref[...] = jnp.zeros_like(st_ref)

    for src_ref, dst_ref in zip(cast_in, cast_out):
        dst_ref[...] = src_ref[...].astype(BF16)

    lanes = [slice(h * HG_DK, (h + 1) * HG_DK) for h in range(HG_HEADS_PER_STEP)]
    heads = [_hgrn_head(q_ref[0, :, ln].astype(F32), lf_ref[0, :, ln], k_ref[0, :, ln].astype(F32),
                        v_ref[0, :, ln], g_ref[0, :, ln].astype(F32), nw_ref[...], tmat_ref[...],
                        st_ref[h]) for h, ln in enumerate(lanes)]
    for h, (y, st) in enumerate(_run_interleaved(heads)):
        o_ref[0, :, lanes[h]] = y
        st_ref[h] = st


def _hgrn(qs, lf, ks, vs, gs, norm_w, to_cast):
    bsz, s, _ = qs.shape
    L = HG_BLOCK
    hp = HG_HEADS_PER_STEP
    tmat = _hgrn_consts()
    blk = pl.BlockSpec((1, L, hp * HG_DK), lambda b, h, n: (b, n, h))
    const = lambda shape: pl.BlockSpec(shape, lambda b, h, n: (0, 0))
    heads, blocks = HG_HEADS // hp, s // L
    cast_specs, cast_shapes = _cast_specs(to_cast, bsz * heads * blocks,
                                          lambda b, h, n: (b * heads + h) * blocks + n)
    return pl.pallas_call(
        _hgrn_kernel,
        grid=(bsz, heads, blocks),
        in_specs=[blk, blk, blk, blk, blk, const((1, HG_DK)), const((L, L))] + cast_specs,
        out_specs=[blk] + cast_specs,
        out_shape=[jax.ShapeDtypeStruct((bsz, s, HG_WIDTH), BF16)] + cast_shapes,
        scratch_shapes=[pltpu.VMEM((hp, HG_DK, HG_DK), F32)],
        compiler_params=pltpu.CompilerParams(
            dimension_semantics=("arbitrary", "arbitrary", "arbitrary"),
            vmem_limit_bytes=VMEM_LIMIT),
        name="hgrn",
    )(qs, lf, ks, vs, gs, norm_w.reshape(1, HG_DK), tmat, *to_cast)


ATT_KEYS = 512
ATT_QUERIES = 1024
ATT_COLS = 256
ATT_ONES = 16

def _attn_scores(k_ref, q_t, n, s_ref, slot, col_blocks):
    TK, C, D = ATT_KEYS, ATT_COLS, DA_HEAD_DIM
    ncol = ATT_QUERIES // C
    kb = k_ref[0, pl.ds(pl.multiple_of(n * TK, TK), TK), :]
    for t in range(2):
        for c in col_blocks:
            s_ref[slot, t * ncol + c] = _dot(kb[:, t * D:(t + 1) * D],
                                             q_t[t * D:(t + 1) * D, c * C:(c + 1) * C])


def _attn_chain(s, v_t, acc_ref, m_ref, cols, first_visible):
    if first_visible is not None:
        key = lax.broadcasted_iota(jnp.int32, s.shape, 0)
        qry = lax.broadcasted_iota(jnp.int32, s.shape, 1) + first_visible
        s = jnp.where(key <= qry, s, jnp.finfo(F32).min)
    m_prev = m_ref[:, cols]
    m_new = jnp.maximum(m_prev, jnp.max(s, axis=0, keepdims=True))
    alpha = jnp.exp2(m_prev - m_new)
    p = jnp.exp2(s - m_new)
    m_ref[:, cols] = m_new
    pv = _dot(v_t, p.astype(BF16))
    yield
    acc_ref[:, cols] = alpha * acc_ref[:, cols] + pv


def _attn_kernel(q_ref, k_ref, v_ref, lq1_ref, lk1_ref, lq2_ref, lk2_ref, sw_ref, *refs):
    n_cast = (len(refs) - 7) // 2
    cast_in, o_ref, cast_out = refs[:n_cast], refs[n_cast], refs[n_cast + 1:2 * n_cast + 1]
    vt_ref, s_ref, acc1_ref, acc2_ref, m1_ref, m2_ref = refs[2 * n_cast + 1:]
    TK, TQ, C = ATT_KEYS, ATT_QUERIES, ATT_COLS
    DV = 2 * DA_HEAD_DIM
    ncol = TQ // C
    nq = q_ref.shape[1] // TQ
    every = tuple(range(ncol))
    plans = [[(c, min(TK, (c + 1) * C - b * TK), c * C - b * TK) for c in every if (c + 1) * C > b * TK]
             for b in range(TQ // TK)]

    for src_ref, dst_ref in zip(cast_in, cast_out):
        dst_ref[...] = src_ref[...].astype(BF16)

    for j in range(vt_ref.shape[0]):
        vt_ref[j, :DV, :] = v_ref[0, j * TK:(j + 1) * TK, :].T
        vt_ref[j, DV:, :] = jnp.ones((ATT_ONES, TK), BF16)

    lam = (jnp.exp(jnp.sum(lq1_ref[...] * lk1_ref[...], axis=-1, keepdims=True))
           - jnp.exp(jnp.sum(lq2_ref[...] * lk2_ref[...], axis=-1, keepdims=True))
           + LAMBDA_INIT)

    def queries_t(i):
        return q_ref[0, pl.ds(pl.multiple_of(i * TQ, TQ), TQ), :].T

    def query_block(i, carry):
        q_t = queries_t(i)
        acc1_ref[...] = jnp.zeros_like(acc1_ref)
        acc2_ref[...] = jnp.zeros_like(acc2_ref)
        m1_ref[...] = jnp.full_like(m1_ref, -jnp.inf)
        m2_ref[...] = jnp.full_like(m2_ref, -jnp.inf)

        def step(n, slot, plan, prefetch_cols):
            if prefetch_cols:
                _attn_scores(k_ref, q_t, n + 1, s_ref, 1 - slot, prefetch_cols)
            vt = vt_ref[n]
            chains = []
            for t, (acc_ref, m_ref) in enumerate(((acc1_ref, m1_ref), (acc2_ref, m2_ref))):
                for c, nk, offset in plan:
                    full = nk == TK and offset is not None and offset >= TK - 1
                    chains.append(_attn_chain(s_ref[slot, t * ncol + c, :nk, :], vt[:, :nk], acc_ref, m_ref,
                                              slice(c * C, (c + 1) * C), None if full else offset))
            _run_interleaved(chains)

        unmasked = [(c, TK, None) for c in every]

        def pair(t, carry):
            step(2 * t, 0, unmasked, every)
            step(2 * t + 1, 1, unmasked, every)
            return carry

        ndiag = len(plans)
        lax.fori_loop(0, (ndiag // 2) * i, pair, 0)
        for b, plan in enumerate(plans):
            nxt = tuple(c for c, _, _ in plans[b + 1]) if b + 1 < ndiag else ()
            step(ndiag * i + b, b % 2, plan, nxt)

        _attn_scores(k_ref, queries_t(jnp.minimum(i + 1, nq - 1)), 0, s_ref, 0, every)

        inv1 = 1.0 / acc1_ref[DV:DV + 1, :]
        inv2 = lam / acc2_ref[DV:DV + 1, :]
        o_t = acc1_ref[:DV, :] * inv1 - acc2_ref[:DV, :] * inv2
        o = o_t.T
        ms = jnp.mean(o * o, -1, keepdims=True)
        y = o * lax.rsqrt(ms + RMS_EPS) * sw_ref[...] * (1.0 - LAMBDA_INIT)
        o_ref[0, pl.ds(pl.multiple_of(i * TQ, TQ), TQ), :] = y.astype(BF16)
        return carry

    _attn_scores(k_ref, queries_t(0), 0, s_ref, 0, every)
    lax.fori_loop(0, nq, query_block, 0)


def _attn(q, k, v, lq1, lk1, lq2, lk2, subln_w, to_cast):
    bsz, s, _ = q.shape
    TK, TQ = ATT_KEYS, ATT_QUERIES
    hw = 2 * DA_HEAD_DIM
    blk = pl.BlockSpec((1, s, hw), lambda b, h: (b, 0, h))
    vec = lambda n: pl.BlockSpec((1, n), lambda b, h: (0, 0))
    cast_specs, cast_shapes = _cast_specs(to_cast, bsz * DA_HEADS, lambda b, h: b * DA_HEADS + h)
    return pl.pallas_call(
        _attn_kernel,
        grid=(bsz, DA_HEADS),
        in_specs=[blk, blk, blk, vec(DA_HEAD_DIM), vec(DA_HEAD_DIM), vec(DA_HEAD_DIM),
                  vec(DA_HEAD_DIM), vec(hw)] + cast_specs,
        out_specs=[blk] + cast_specs,
        out_shape=[jax.ShapeDtypeStruct((bsz, s, DA_WIDTH), BF16)] + cast_shapes,
        scratch_shapes=[pltpu.VMEM((s // TK, hw + ATT_ONES, TK), BF16),
                        pltpu.VMEM((2, 2 * (TQ // ATT_COLS), TK, ATT_COLS), F32),
                        pltpu.VMEM((hw + ATT_ONES, TQ), F32), pltpu.VMEM((hw + ATT_ONES, TQ), F32),
                        pltpu.VMEM((1, TQ), F32), pltpu.VMEM((1, TQ), F32)],
        compiler_params=pltpu.CompilerParams(
            dimension_semantics=("arbitrary", "arbitrary"),
            vmem_limit_bytes=VMEM_LIMIT),
        name="diff_attn",
    )(q, k, v, lq1.reshape(1, -1), lk1.reshape(1, -1), lq2.reshape(1, -1), lk2.reshape(1, -1),
      subln_w.reshape(1, -1), *to_cast)


def _outproj_kernel(n_tiles, yh_ref, ya_ref, w_ref, x_ref, mod_ref, g_ref, b_ref, h_ref, u_ref, y_ref):
    half = w_ref.shape[1] // 2

    def matmul(slot, zero):
        yh = yh_ref[0]
        ya = ya_ref[0]
        y_ref[slot, :, :half] = _dot(yh, w_ref[:HG_WIDTH, :half]) + _dot(ya, w_ref[HG_WIDTH:, :half])
        if zero is not None:
            ya = ya + jnp.tile(zero, (1, ya.shape[1] // zero.shape[1]))
        y_ref[slot, :, half:] = _dot(yh, w_ref[:HG_WIDTH, half:]) + _dot(ya, w_ref[HG_WIDTH:, half:])

    def finish(slot):
        m = mod_ref[0]
        r = DEEPNORM_ALPHA * x_ref[0] + m[2:3, :] * y_ref[slot]
        h = _ln_plain(r) * g_ref[...] + b_ref[...]
        h_ref[0] = h
        u = (_ln_plain(h) * (1.0 + m[4:5, :]) + m[3:4, :]).astype(BF16)
        u_ref[0] = u
        return _zero_after(u)

    _lagged_steps(matmul, finish, n_tiles)


def _outproj(y_hg, y_da, w_out, x, mod, ln_g, ln_b):
    bsz, s, d = x.shape
    tm = PROJ_ROWS
    n_tiles = bsz * (s // tm)
    cur, lag, lag_batch = _tile_maps(s // tm, n_tiles)
    const = lambda shape: pl.BlockSpec(shape, lambda t: (0, 0))
    return pl.pallas_call(
        functools.partial(_outproj_kernel, n_tiles),
        grid=(n_tiles + 1,),
        in_specs=[pl.BlockSpec((1, tm, HG_WIDTH), cur),
                  pl.BlockSpec((1, tm, DA_WIDTH), cur),
                  const(w_out.shape),
                  pl.BlockSpec((1, tm, d), lag),
                  pl.BlockSpec((1, 6, d), lag_batch),
                  const((1, d)), const((1, d))],
        out_specs=[pl.BlockSpec((1, tm, d), lag), pl.BlockSpec((1, tm, d), lag)],
        out_shape=[jax.ShapeDtypeStruct((bsz, s, d), F32), jax.ShapeDtypeStruct((bsz, s, d), BF16)],
        scratch_shapes=[pltpu.VMEM((2, tm, d), F32)],
        compiler_params=pltpu.CompilerParams(dimension_semantics=("arbitrary",),
                                             vmem_limit_bytes=VMEM_LIMIT),
        name="outproj_ln1",
    )(y_hg, y_da, w_out, x, mod, ln_g.reshape(1, d), ln_b.reshape(1, d))


FFN_ROWS = 512
FFN_COLS = 512


def _ffn_kernel(u_ref, wg_ref, wu_ref, wd_ref, h_ref, mod_ref, g_ref, b_ref, o_ref, acc_ref):
    j = pl.program_id(2)

    @pl.when(j == 0)
    def _():
        acc_ref[...] = jnp.zeros_like(acc_ref)

    u = u_ref[0]
    a = _dot(u, wg_ref[...])
    b = _dot(u, wu_ref[...])
    z = (a * jax.nn.sigmoid(a) * b).astype(BF16)
    acc_ref[...] += _dot(z, wd_ref[...])

    @pl.when(j == pl.num_programs(2) - 1)
    def _():
        m = mod_ref[0]
        r = DEEPNORM_ALPHA * h_ref[0] + m[5:6, :] * acc_ref[...]
        o_ref[0] = _ln_plain(r) * g_ref[...] + b_ref[...]


def _ffn(u, w_gate, w_up, w_down, h, mod, ln_g, ln_b):
    bsz, s, d = h.shape
    f = w_gate.shape[1]
    tm, tf = FFN_ROWS, FFN_COLS
    row = lambda b, i, j: (b, i, 0)
    const3 = lambda shape: pl.BlockSpec(shape, lambda b, i, j: (0, 0))
    return pl.pallas_call(
        _ffn_kernel,
        grid=(bsz, s // tm, f // tf),
        in_specs=[pl.BlockSpec((1, tm, d), row),
                  pl.BlockSpec((d, tf), lambda b, i, j: (0, j)),
                  pl.BlockSpec((d, tf), lambda b, i, j: (0, j)),
                  pl.BlockSpec((tf, d), lambda b, i, j: (j, 0)),
                  pl.BlockSpec((1, tm, d), row),
                  pl.BlockSpec((1, 6, d), lambda b, i, j: (b, 0, 0)),
                  const3((1, d)), const3((1, d))],
        out_specs=pl.BlockSpec((1, tm, d), row),
        out_shape=jax.ShapeDtypeStruct((bsz, s, d), F32),
        scratch_shapes=[pltpu.VMEM((tm, d), F32)],
        compiler_params=pltpu.CompilerParams(
            dimension_semantics=("arbitrary", "arbitrary", "arbitrary"),
            vmem_limit_bytes=VMEM_LIMIT),
        name="ffn_ln2",
    )(u, w_gate, w_up, w_down, h, mod, ln_g.reshape(1, d), ln_b.reshape(1, d))


def kernel(x, c, positions, w_ada, b_ada, w_in, lb_logits, hg_norm_w, lam_q1, lam_k1, lam_q2, lam_k2,
           subln_w, w_out, ln1_g, ln1_b, w_gate, w_up, w_down, ln2_g, ln2_b):
    assert w_ada.shape[0] == DEPTH == 1
    bsz, s, d = x.shape

    mod = _adaln(c, w_ada[0], b_ada[0]).reshape(bsz, 6, d)
    cos, sin, w_in_b = _rope_tables(positions, w_in[0])

    qs, lf, ks, vs, gs, dq, dk, dv = _inproj(x, mod, w_in_b, lb_logits, cos, sin)
    y_hg, w_out_b, w_gate_b, w_up_b = _hgrn(qs, lf, ks, vs, gs, hg_norm_w[0],
                                            [w_out[0], w_gate[0], w_up[0]])
    y_da, w_down_b = _attn(dq, dk, dv, lam_q1[0], lam_k1[0], lam_q2[0], lam_k2[0], subln_w[0],
                           [w_down[0]])
    h1, u2 = _outproj(y_hg, y_da, w_out_b, x, mod, ln1_g[0], ln1_b[0])
    return _ffn(u2, w_gate_b, w_up_b, w_down_b, h1, mod, ln2_g[0], ln2_b[0])
```

```python
import functools
import math

import numpy as np
import jax
import jax.numpy as jnp
from jax import lax
from jax.experimental import pallas as pl
from jax.experimental.pallas import tpu as pltpu

HG_WIDTH = 1024
DA_WIDTH = 1024
HG_HEADS = 8
HG_DK = 128
HG_CHUNK = 32
DA_HEADS = 4
DA_HEAD_DIM = 128
ROPE_THETA = 10000.0
DEPTH = 1
DEEPNORM_ALPHA = (2.0 * DEPTH) ** 0.25
LN_EPS = 1e-5
RMS_EPS = 1e-6
LAMBDA_INIT = 0.8 - 0.6 * math.exp(-0.3 * 0)
SEG = 1024

VMEM_LIMIT = 56 * 1024 * 1024

F32 = jnp.float32
BF16 = jnp.bfloat16


def _dot(a, b):
    return jnp.dot(a, b, preferred_element_type=F32)


def _dot_nt(a, b):
    return lax.dot_general(a, b, (((1,), (1,)), ((), ())), preferred_element_type=F32)


def _dot_tn(a, b):
    return lax.dot_general(a, b, (((0,), (0,)), ((), ())), preferred_element_type=F32)


def _ln_plain(x):
    mu = jnp.mean(x, -1, keepdims=True)
    xc = x - mu
    var = jnp.mean(xc * xc, -1, keepdims=True)
    return xc * lax.rsqrt(var + LN_EPS)


def _run_interleaved(chains):
    pending = list(chains)
    results = {}
    while pending:
        for chain in list(pending):
            try:
                next(chain)
            except StopIteration as done:
                results[id(chain)] = done.value
                pending.remove(chain)
    return [results[id(chain)] for chain in chains]


def _cast_specs(mats, n_steps, step_of):
    specs, shapes = [], []
    for m in mats:
        rows = m.shape[0] // n_steps
        assert rows * n_steps == m.shape[0] and rows % 16 == 0, (m.shape, n_steps)
        specs.append(pl.BlockSpec((rows, m.shape[1]), lambda *idx: (step_of(*idx), 0)))
        shapes.append(jax.ShapeDtypeStruct(m.shape, BF16))
    return specs, shapes


def _adaln_kernel(c_ref, w_ref, b_ref, o_ref):
    cond = c_ref[...]
    cond = cond * jax.nn.sigmoid(cond)
    o_ref[...] = _dot(cond, w_ref[...]) + b_ref[...]


def _adaln(c, w, b):
    bsz, d = c.shape
    n = w.shape[1]
    tn = 1024
    return pl.pallas_call(
        _adaln_kernel,
        grid=(n // tn,),
        in_specs=[pl.BlockSpec((bsz, d), lambda j: (0, 0)),
                  pl.BlockSpec((d, tn), lambda j: (0, j)),
                  pl.BlockSpec((1, tn), lambda j: (0, j))],
        out_specs=pl.BlockSpec((bsz, tn), lambda j: (0, j)),
        out_shape=jax.ShapeDtypeStruct((bsz, n), F32),
        compiler_params=pltpu.CompilerParams(dimension_semantics=("arbitrary",),
                                             vmem_limit_bytes=VMEM_LIMIT),
        name="adaln",
    )(c, w, b.reshape(1, n))


def _rope_kernel(pos_ref, inv_ref, sign_ref, w_ref, cos_ref, sin_ref, wb_ref):
    wb_ref[...] = w_ref[...].astype(BF16)

    hs = pos_ref.shape[1] // 2
    lo = lax.broadcasted_iota(jnp.int32, (hs, DA_HEAD_DIM), 1) < DA_HEAD_DIM // 2
    ang = jnp.where(lo, pos_ref[0, :hs, :], pos_ref[0, hs:, :]) * inv_ref[...]
    for fn, ref, scale in ((jnp.cos, cos_ref, None), (jnp.sin, sin_ref, sign_ref[...])):
        val = fn(ang)
        swapped = pltpu.roll(val, DA_HEAD_DIM // 2, 1)
        top = jnp.where(lo, val, swapped)
        bottom = jnp.where(lo, swapped, val)
        ref[0, :hs, :] = top if scale is None else top * scale
        ref[0, hs:, :] = bottom if scale is None else bottom * scale


def _rope_tables(positions, to_cast):
    bsz, s = positions.shape
    ts = 2048
    half = DA_HEAD_DIM // 2
    inv = 1.0 / (ROPE_THETA ** (jnp.arange(0, DA_HEAD_DIM, 2, dtype=F32) / DA_HEAD_DIM))
    inv2 = jnp.concatenate([inv, inv]).reshape(1, DA_HEAD_DIM)
    sign = jnp.concatenate([-jnp.ones((half,), F32), jnp.ones((half,), F32)]).reshape(1, DA_HEAD_DIM)
    pos = positions.astype(F32).reshape(bsz, s, 1)
    tab = jax.ShapeDtypeStruct((bsz, s, DA_HEAD_DIM), F32)
    cast_specs, cast_shapes = _cast_specs([to_cast], bsz * (s // ts), lambda b, i: b * (s // ts) + i)
    return pl.pallas_call(
        _rope_kernel,
        grid=(bsz, s // ts),
        in_specs=[pl.BlockSpec((1, ts, 1), lambda b, i: (b, i, 0)),
                  pl.BlockSpec((1, DA_HEAD_DIM), lambda b, i: (0, 0)),
                  pl.BlockSpec((1, DA_HEAD_DIM), lambda b, i: (0, 0))] + cast_specs,
        out_specs=[pl.BlockSpec((1, ts, DA_HEAD_DIM), lambda b, i: (b, i, 0)),
                   pl.BlockSpec((1, ts, DA_HEAD_DIM), lambda b, i: (b, i, 0))] + cast_specs,
        out_shape=[tab, tab] + cast_shapes,
        compiler_params=pltpu.CompilerParams(dimension_semantics=("arbitrary", "arbitrary"),
                                             vmem_limit_bytes=VMEM_LIMIT),
        name="rope_tables",
    )(pos, inv2, sign, to_cast)


PROJ_ROWS = 512
SEG_ROWS = 1024


def _zero_after(x):
    bits = pltpu.bitcast(x, jnp.int32)
    acc = bits[0:8, :]
    for r in range(8, bits.shape[0], 8):
        acc = acc | bits[r:r + 8, :]
    word = acc[:, 0:128]
    for c in range(128, acc.shape[1], 128):
        word = word | acc[:, c:c + 128]
    word = lax.shift_right_logical(lax.shift_right_logical(word, 16), 16)
    return pltpu.bitcast(word, F32)[0:1, :].astype(BF16)


def _lagged_steps(matmul, finish, n_tiles):
    t = pl.program_id(0)

    @pl.when(t == 0)
    def _():
        matmul(0, None)

    for parity in (0, 1):
        @pl.when(jnp.logical_and(jnp.logical_and(t > 0, t < n_tiles), t % 2 == parity))
        def _():
            matmul(parity, finish(1 - parity))

    @pl.when(t == n_tiles)
    def _():
        finish((n_tiles - 1) % 2)


def _tile_maps(tiles_per_batch, n_tiles):
    def split(k):
        return k // tiles_per_batch, k % tiles_per_batch
    cur = lambda t: (*split(jnp.minimum(t, n_tiles - 1)), 0)
    lag = lambda t: (*split(jnp.maximum(t - 1, 0)), 0)
    lag_batch = lambda t: (jnp.maximum(t - 1, 0) // tiles_per_batch, 0, 0)
    return cur, lag, lag_batch


def _rope_apply(acc, cos, sin, scale):
    outs = []
    for h in range(SEG // DA_HEAD_DIM):
        xh = acc[:, h * DA_HEAD_DIM:(h + 1) * DA_HEAD_DIM]
        rot = pltpu.roll(xh, DA_HEAD_DIM // 2, 1)
        y = xh * cos + rot * sin
        if scale is not None:
            y = y * scale
        outs.append(y.astype(BF16))
    return jnp.concatenate(outs, axis=-1)


def _sigmoid_tanh(x):
    return 0.5 * jnp.tanh(0.5 * x) + 0.5


def _finish_hg_q(y, side, outs):
    q = (y * _sigmoid_tanh(y) * (HG_DK ** -0.5)).astype(BF16)
    outs[0][0] = q
    return _zero_after(q)


def _finish_hg_f(y, side, outs):
    lbl = side[0][...]
    e = jnp.exp(lbl - jnp.max(lbl, axis=0, keepdims=True))
    lb = e[0:1, :] / jnp.sum(e, axis=0, keepdims=True)
    f = lb + (1.0 - lb) * jax.nn.sigmoid(y)
    log_f = jnp.log2(f)
    key = (1.0 - f).astype(BF16)
    outs[0][0] = log_f
    outs[1][0] = key
    return _zero_after(log_f) + _zero_after(key)


def _finish_cast(y, side, outs):
    outs[0][0] = y.astype(BF16)
    return None


def _finish_hg_g(y, side, outs):
    g = _sigmoid_tanh(y).astype(BF16)
    outs[0][0] = g
    return _zero_after(g)


def _finish_da_q(y, side, outs):
    q = _rope_apply(y, side[0][0], side[1][0], DA_HEAD_DIM ** -0.5 * math.log2(math.e))
    outs[0][0] = q
    return _zero_after(q)


def _finish_da_k(y, side, outs):
    k = _rope_apply(y, side[0][0], side[1][0], None)
    outs[0][0] = k
    return _zero_after(k)


def _segment_kernel(finish, n_side, n_tiles, u_ref, w_ref, *refs):
    side, outs, y_ref = refs[:n_side], refs[n_side:-1], refs[-1]

    half = SEG // 2

    def matmul(slot, zero):
        u = u_ref[0]
        y_ref[slot, :, :half] = _dot(u, w_ref[:, :half])
        if zero is not None:
            u = u + jnp.tile(zero, (1, u.shape[1] // zero.shape[1]))
        y_ref[slot, :, half:] = _dot(u, w_ref[:, half:])

    _lagged_steps(matmul, lambda slot: finish(y_ref[slot], side, outs), n_tiles)


def _segment(u, w_in, seg, finish, side, side_specs, out_dtypes, name):
    bsz, s, d = u.shape
    tm = SEG_ROWS
    n_tiles = bsz * (s // tm)
    cur, lag, _ = _tile_maps(s // tm, n_tiles)
    specs = [pl.BlockSpec((1, tm, d), cur), pl.BlockSpec((d, SEG), lambda t: (0, seg))]
    specs += [spec(lag) for spec in side_specs]
    return pl.pallas_call(
        functools.partial(_segment_kernel, finish, len(side), n_tiles),
        grid=(n_tiles + 1,),
        in_specs=specs,
        out_specs=[pl.BlockSpec((1, tm, SEG), lag) for _ in out_dtypes],
        out_shape=[jax.ShapeDtypeStruct((bsz, s, SEG), dt) for dt in out_dtypes],
        scratch_shapes=[pltpu.VMEM((2, tm, SEG), F32)],
        compiler_params=pltpu.CompilerParams(dimension_semantics=("arbitrary",),
                                             vmem_limit_bytes=VMEM_LIMIT),
        name=name,
    )(u, w_in, *side)


LN_CHUNK_ROWS = 512


def _first_segment_kernel(finish, n_tiles, x_ref, mod_ref, w_ref, u_ref, o_ref, y_ref):
    half = SEG // 2

    def matmul(slot, zero):
        m = mod_ref[0]
        for r0 in range(0, x_ref.shape[1], LN_CHUNK_ROWS):
            rows = slice(r0, r0 + LN_CHUNK_ROWS)
            u = (_ln_plain(x_ref[0, rows, :]) * (1.0 + m[1:2, :]) + m[0:1, :]).astype(BF16)
            u_ref[0, rows, :] = u
            y_ref[slot, rows, :half] = _dot(u, w_ref[:, :half])
            if zero is not None:
                u = u + jnp.tile(zero, (1, u.shape[1] // zero.shape[1]))
            y_ref[slot, rows, half:] = _dot(u, w_ref[:, half:])

    _lagged_steps(matmul, lambda slot: finish(y_ref[slot], (), (o_ref,)), n_tiles)


def _first_segment(x, mod, w_in, finish, name):
    bsz, s, d = x.shape
    tm = SEG_ROWS
    n_tiles = bsz * (s // tm)
    cur, lag, _ = _tile_maps(s // tm, n_tiles)
    cur_batch = lambda t: (jnp.minimum(t, n_tiles - 1) // (s // tm), 0, 0)
    return pl.pallas_call(
        functools.partial(_first_segment_kernel, finish, n_tiles),
        grid=(n_tiles + 1,),
        in_specs=[pl.BlockSpec((1, tm, d), cur), pl.BlockSpec((1, 6, d), cur_batch),
                  pl.BlockSpec((d, SEG), lambda t: (0, 0))],
        out_specs=[pl.BlockSpec((1, tm, d), cur), pl.BlockSpec((1, tm, SEG), lag)],
        out_shape=[jax.ShapeDtypeStruct((bsz, s, d), BF16), jax.ShapeDtypeStruct((bsz, s, SEG), BF16)],
        scratch_shapes=[pltpu.VMEM((2, tm, SEG), F32)],
        compiler_params=pltpu.CompilerParams(dimension_semantics=("arbitrary",),
                                             vmem_limit_bytes=VMEM_LIMIT),
        name=name,
    )(x, mod, w_in)


def _inproj(x, mod, w_in, lb_logits, cos, sin):
    tm = SEG_ROWS
    table = lambda lag: pl.BlockSpec((1, tm, DA_HEAD_DIM), lag)
    const = lambda lag: pl.BlockSpec(lb_logits.shape, lambda t: (0, 0))
    u, qs = _first_segment(x, mod, w_in, _finish_hg_q, "inproj_ln_hg_q")
    lf, ks = _segment(u, w_in, 1, _finish_hg_f, [lb_logits], [const], [F32, BF16], "inproj_hg_f")
    vs, = _segment(u, w_in, 2, _finish_cast, [], [], [BF16], "inproj_hg_i")
    gs, = _segment(u, w_in, 3, _finish_hg_g, [], [], [BF16], "inproj_hg_g")
    dq, = _segment(u, w_in, 4, _finish_da_q, [cos, sin], [table, table], [BF16], "inproj_da_q")
    dk, = _segment(u, w_in, 5, _finish_da_k, [cos, sin], [table, table], [BF16], "inproj_da_k")
    dv, = _segment(u, w_in, 6, _finish_cast, [], [], [BF16], "inproj_da_v")
    return qs, lf, ks, vs, gs, dq, dk, dv


HG_BLOCK = 256
HG_HEADS_PER_STEP = 8


def _hgrn_consts():
    L, C = HG_BLOCK, HG_CHUNK
    r = np.arange(L)[:, None]
    c = np.arange(L)[None, :]
    t_cum = (((r // C) == (c // C)) & (c <= r)).astype(np.float32)
    return jnp.asarray(t_cum, BF16)


def _bcast_chunk_rows(x, row_in_chunk):
    L, C = HG_BLOCK, HG_CHUNK
    parts = [jnp.broadcast_to(x[c * C + row_in_chunk:c * C + row_in_chunk + 1, :], (C, x.shape[1]))
             for c in range(L // C)]
    return jnp.concatenate(parts, axis=0)


def _hgrn_head(q, lf, k, v, g, nw, tmat, st):
    L, C = HG_BLOCK, HG_CHUNK

    hi = lf.astype(BF16)
    r1 = lf - hi.astype(F32)
    mid = r1.astype(BF16)
    lo = (r1 - mid.astype(F32)).astype(BF16)
    gg = _dot(tmat, jnp.concatenate([hi, mid, lo], axis=1))
    yield
    g_cum = gg[:, :HG_DK] + gg[:, HG_DK:2 * HG_DK] + gg[:, 2 * HG_DK:]
    g_mid = _bcast_chunk_rows(g_cum, C // 2 - 1)
    g_last = _bcast_chunk_rows(g_cum, C - 1)

    qa = (q * jnp.exp2(g_cum - g_mid)).astype(BF16)
    ka = (k * jnp.exp2(g_mid - g_cum)).astype(BF16)
    qd = (q * jnp.exp2(g_cum)).astype(BF16)
    kd = (k * jnp.exp2(g_last - g_cum)).astype(BF16)
    dl = jnp.exp2(g_last)

    a = _dot_nt(qa, ka)
    yield
    a = a.astype(BF16)
    a = jnp.where(tmat > 0, a, jnp.zeros_like(a))
    o_intra = _dot(a, v)
    yield

    zeros = jnp.zeros((C, HG_DK), BF16)
    upds = []
    for p in range(L // (2 * C)):
        r0 = 2 * p * C
        rhs = jnp.concatenate(
            [jnp.concatenate([kd[r0:r0 + C], zeros], axis=1),
             jnp.concatenate([zeros, kd[r0 + C:r0 + 2 * C]], axis=1)], axis=0)
        u2 = _dot_tn(v[r0:r0 + 2 * C], rhs)
        upds += [u2[:, :HG_DK], u2[:, HG_DK:]]
    yield

    starts = []
    for c in range(L // C):
        starts.append(st.T.astype(BF16))
        st = st * dl[c * C:c * C + 1, :] + upds[c]

    outs = [o_intra[c * C:(c + 1) * C] + _dot(qd[c * C:(c + 1) * C], starts[c])
            for c in range(L // C)]
    yield
    o = jnp.concatenate(outs, axis=0)
    ms = jnp.mean(o * o, -1, keepdims=True)
    y = o * lax.rsqrt(ms + RMS_EPS) * nw * g
    return y.astype(BF16), st


def _hgrn_kernel(q_ref, lf_ref, k_ref, v_ref, g_ref, nw_ref, tmat_ref, *refs):
    n_cast = (len(refs) - 2) // 2
    cast_in, o_ref, cast_out, st_ref = refs[:n_cast], refs[n_cast], refs[n_cast + 1:-1], refs[-1]

    @pl.when(pl.program_id(2) == 0)
    def _():
        st_ref[...] = jnp.zeros_like(st_ref)

    for src_ref, dst_ref in zip(cast_in, cast_out):
        dst_ref[...] = src_ref[...].astype(BF16)

    lanes = [slice(h * HG_DK, (h + 1) * HG_DK) for h in range(HG_HEADS_PER_STEP)]
    heads = [_hgrn_head(q_ref[0, :, ln].astype(F32), lf_ref[0, :, ln], k_ref[0, :, ln].astype(F32),
                        v_ref[0, :, ln], g_ref[0, :, ln].astype(F32), nw_ref[...], tmat_ref[...],
                        st_ref[h]) for h, ln in enumerate(lanes)]
    for h, (y, st) in enumerate(_run_interleaved(heads)):
        o_ref[0, :, lanes[h]] = y
        st_ref[h] = st


def _hgrn(qs, lf, ks, vs, gs, norm_w, to_cast):
    bsz, s, _ = qs.shape
    L = HG_BLOCK
    hp = HG_HEADS_PER_STEP
    tmat = _hgrn_consts()
    blk = pl.BlockSpec((1, L, hp * HG_DK), lambda b, h, n: (b, n, h))
    const = lambda shape: pl.BlockSpec(shape, lambda b, h, n: (0, 0))
    heads, blocks = HG_HEADS // hp, s // L
    cast_specs, cast_shapes = _cast_specs(to_cast, bsz * heads * blocks,
                                          lambda b, h, n: (b * heads + h) * blocks + n)
    return pl.pallas_call(
        _hgrn_kernel,
        grid=(bsz, heads, blocks),
        in_specs=[blk, blk, blk, blk, blk, const((1, HG_DK)), const((L, L))] + cast_specs,
        out_specs=[blk] + cast_specs,
        out_shape=[jax.ShapeDtypeStruct((bsz, s, HG_WIDTH), BF16)] + cast_shapes,
        scratch_shapes=[pltpu.VMEM((hp, HG_DK, HG_DK), F32)],
        compiler_params=pltpu.CompilerParams(
            dimension_semantics=("arbitrary", "arbitrary", "arbitrary"),
            vmem_limit_bytes=VMEM_LIMIT),
        name="hgrn",
    )(qs, lf, ks, vs, gs, norm_w.reshape(1, HG_DK), tmat, *to_cast)


ATT_KEYS = 512
ATT_QUERIES = 1024
ATT_COLS = 256
ATT_ONES = 16

def _attn_scores(k_ref, q_t, n, s_ref, slot, col_blocks):
    TK, C, D = ATT_KEYS, ATT_COLS, DA_HEAD_DIM
    ncol = ATT_QUERIES // C
    kb = k_ref[0, pl.ds(pl.multiple_of(n * TK, TK), TK), :]
    for t in range(2):
        for c in col_blocks:
            s_ref[slot, t * ncol + c] = _dot(kb[:, t * D:(t + 1) * D],
                                             q_t[t * D:(t + 1) * D, c * C:(c + 1) * C])


def _attn_chain(s, v_t, acc_ref, m_ref, cols, first_visible):
    if first_visible is not None:
        key = lax.broadcasted_iota(jnp.int32, s.shape, 0)
        qry = lax.broadcasted_iota(jnp.int32, s.shape, 1) + first_visible
        s = jnp.where(key <= qry, s, jnp.finfo(F32).min)
    m_prev = m_ref[:, cols]
    m_new = jnp.maximum(m_prev, jnp.max(s, axis=0, keepdims=True))
    alpha = jnp.exp2(m_prev - m_new)
    p = jnp.exp2(s - m_new)
    m_ref[:, cols] = m_new
    pv = _dot(v_t, p.astype(BF16))
    yield
    acc_ref[:, cols] = alpha * acc_ref[:, cols] + pv


def _attn_kernel(q_ref, k_ref, v_ref, lq1_ref, lk1_ref, lq2_ref, lk2_ref, sw_ref, *refs):
    n_cast = (len(refs) - 7) // 2
    cast_in, o_ref, cast_out = refs[:n_cast], refs[n_cast], refs[n_cast + 1:2 * n_cast + 1]
    vt_ref, s_ref, acc1_ref, acc2_ref, m1_ref, m2_ref = refs[2 * n_cast + 1:]
    TK, TQ, C = ATT_KEYS, ATT_QUERIES, ATT_COLS
    DV = 2 * DA_HEAD_DIM
    ncol = TQ // C
    nq = q_ref.shape[1] // TQ
    every = tuple(range(ncol))
    plans = [[(c, min(TK, (c + 1) * C - b * TK), c * C - b * TK) for c in every if (c + 1) * C > b * TK]
             for b in range(TQ // TK)]

    for src_ref, dst_ref in zip(cast_in, cast_out):
        dst_ref[...] = src_ref[...].astype(BF16)

    for j in range(vt_ref.shape[0]):
        vt_ref[j, :DV, :] = v_ref[0, j * TK:(j + 1) * TK, :].T
        vt_ref[j, DV:, :] = jnp.ones((ATT_ONES, TK), BF16)

    lam = (jnp.exp(jnp.sum(lq1_ref[...] * lk1_ref[...], axis=-1, keepdims=True))
           - jnp.exp(jnp.sum(lq2_ref[...] * lk2_ref[...], axis=-1, keepdims=True))
           + LAMBDA_INIT)

    def queries_t(i):
        return q_ref[0, pl.ds(pl.multiple_of(i * TQ, TQ), TQ), :].T

    def query_block(i, carry):
        q_t = queries_t(i)
        acc1_ref[...] = jnp.zeros_like(acc1_ref)
        acc2_ref[...] = jnp.zeros_like(acc2_ref)
        m1_ref[...] = jnp.full_like(m1_ref, -jnp.inf)
        m2_ref[...] = jnp.full_like(m2_ref, -jnp.inf)

        def step(n, slot, plan, prefetch_cols):
            if prefetch_cols:
                _attn_scores(k_ref, q_t, n + 1, s_ref, 1 - slot, prefetch_cols)
            vt = vt_ref[n]
            chains = []
            for t, (acc_ref, m_ref) in enumerate(((acc1_ref, m1_ref), (acc2_ref, m2_ref))):
                for c, nk, offset in plan:
                    full = nk == TK and offset is not None and offset >= TK - 1
                    chains.append(_attn_chain(s_ref[slot, t * ncol + c, :nk, :], vt[:, :nk], acc_ref, m_ref,
                                              slice(c * C, (c + 1) * C), None if full else offset))
            _run_interleaved(chains)

        unmasked = [(c, TK, None) for c in every]

        def pair(t, carry):
            step(2 * t, 0, unmasked, every)
            step(2 * t + 1, 1, unmasked, every)
            return carry

        ndiag = len(plans)
        lax.fori_loop(0, (ndiag // 2) * i, pair, 0)
        for b, plan in enumerate(plans):
            nxt = tuple(c for c, _, _ in plans[b + 1]) if b + 1 < ndiag else ()
            step(ndiag * i + b, b % 2, plan, nxt)

        _attn_scores(k_ref, queries_t(jnp.minimum(i + 1, nq - 1)), 0, s_ref, 0, every)

        inv1 = 1.0 / acc1_ref[DV:DV + 1, :]
        inv2 = lam / acc2_ref[DV:DV + 1, :]
        o_t = acc1_ref[:DV, :] * inv1 - acc2_ref[:DV, :] * inv2
        o = o_t.T
        ms = jnp.mean(o * o, -1, keepdims=True)
        y = o * lax.rsqrt(ms + RMS_EPS) * sw_ref[...] * (1.0 - LAMBDA_INIT)
        o_ref[0, pl.ds(pl.multiple_of(i * TQ, TQ), TQ), :] = y.astype(BF16)
        return carry

    _attn_scores(k_ref, queries_t(0), 0, s_ref, 0, every)
    lax.fori_loop(0, nq, query_block, 0)


def _attn(q, k, v, lq1, lk1, lq2, lk2, subln_w, to_cast):
    bsz, s, _ = q.shape
    TK, TQ = ATT_KEYS, ATT_QUERIES
    hw = 2 * DA_HEAD_DIM
    blk = pl.BlockSpec((1, s, hw), lambda b, h: (b, 0, h))
    vec = lambda n: pl.BlockSpec((1, n), lambda b, h: (0, 0))
    cast_specs, cast_shapes = _cast_specs(to_cast, bsz * DA_HEADS, lambda b, h: b * DA_HEADS + h)
    return pl.pallas_call(
        _attn_kernel,
        grid=(bsz, DA_HEADS),
        in_specs=[blk, blk, blk, vec(DA_HEAD_DIM), vec(DA_HEAD_DIM), vec(DA_HEAD_DIM),
                  vec(DA_HEAD_DIM), vec(hw)] + cast_specs,
        out_specs=[blk] + cast_specs,
        out_shape=[jax.ShapeDtypeStruct((bsz, s, DA_WIDTH), BF16)] + cast_shapes,
        scratch_shapes=[pltpu.VMEM((s // TK, hw + ATT_ONES, TK), BF16),
                        pltpu.VMEM((2, 2 * (TQ // ATT_COLS), TK, ATT_COLS), F32),
                        pltpu.VMEM((hw + ATT_ONES, TQ), F32), pltpu.VMEM((hw + ATT_ONES, TQ), F32),
                        pltpu.VMEM((1, TQ), F32), pltpu.VMEM((1, TQ), F32)],
        compiler_params=pltpu.CompilerParams(
            dimension_semantics=("arbitrary", "arbitrary"),
            vmem_limit_bytes=VMEM_LIMIT),
        name="diff_attn",
    )(q, k, v, lq1.reshape(1, -1), lk1.reshape(1, -1), lq2.reshape(1, -1), lk2.reshape(1, -1),
      subln_w.reshape(1, -1), *to_cast)


def _outproj_kernel(n_tiles, yh_ref, ya_ref, w_ref, x_ref, mod_ref, g_ref, b_ref, h_ref, u_ref, y_ref):
    half = w_ref.shape[1] // 2

    def matmul(slot, zero):
        yh = yh_ref[0]
        ya = ya_ref[0]
        y_ref[slot, :, :half] = _dot(yh, w_ref[:HG_WIDTH, :half]) + _dot(ya, w_ref[HG_WIDTH:, :half])
        if zero is not None:
            ya = ya + jnp.tile(zero, (1, ya.shape[1] // zero.shape[1]))
        y_ref[slot, :, half:] = _dot(yh, w_ref[:HG_WIDTH, half:]) + _dot(ya, w_ref[HG_WIDTH:, half:])

    def finish(slot):
        m = mod_ref[0]
        r = DEEPNORM_ALPHA * x_ref[0] + m[2:3, :] * y_ref[slot]
        h = _ln_plain(r) * g_ref[...] + b_ref[...]
        h_ref[0] = h
        u = (_ln_plain(h) * (1.0 + m[4:5, :]) + m[3:4, :]).astype(BF16)
        u_ref[0] = u
        return _zero_after(u)

    _lagged_steps(matmul, finish, n_tiles)


def _outproj(y_hg, y_da, w_out, x, mod, ln_g, ln_b):
    bsz, s, d = x.shape
    tm = PROJ_ROWS
    n_tiles = bsz * (s // tm)
    cur, lag, lag_batch = _tile_maps(s // tm, n_tiles)
    const = lambda shape: pl.BlockSpec(shape, lambda t: (0, 0))
    return pl.pallas_call(
        functools.partial(_outproj_kernel, n_tiles),
        grid=(n_tiles + 1,),
        in_specs=[pl.BlockSpec((1, tm, HG_WIDTH), cur),
                  pl.BlockSpec((1, tm, DA_WIDTH), cur),
                  const(w_out.shape),
                  pl.BlockSpec((1, tm, d), lag),
                  pl.BlockSpec((1, 6, d), lag_batch),
                  const((1, d)), const((1, d))],
        out_specs=[pl.BlockSpec((1, tm, d), lag), pl.BlockSpec((1, tm, d), lag)],
        out_shape=[jax.ShapeDtypeStruct((bsz, s, d), F32), jax.ShapeDtypeStruct((bsz, s, d), BF16)],
        scratch_shapes=[pltpu.VMEM((2, tm, d), F32)],
        compiler_params=pltpu.CompilerParams(dimension_semantics=("arbitrary",),
                                             vmem_limit_bytes=VMEM_LIMIT),
        name="outproj_ln1",
    )(y_hg, y_da, w_out, x, mod, ln_g.reshape(1, d), ln_b.reshape(1, d))


FFN_ROWS = 512
FFN_COLS = 512


def _ffn_kernel(u_ref, wg_ref, wu_ref, wd_ref, h_ref, mod_ref, g_ref, b_ref, o_ref, acc_ref):
    j = pl.program_id(2)

    @pl.when(j == 0)
    def _():
        acc_ref[...] = jnp.zeros_like(acc_ref)

    u = u_ref[0]
    a = _dot(u, wg_ref[...])
    b = _dot(u, wu_ref[...])
    z = (a * jax.nn.sigmoid(a) * b).astype(BF16)
    acc_ref[...] += _dot(z, wd_ref[...])

    @pl.when(j == pl.num_programs(2) - 1)
    def _():
        m = mod_ref[0]
        r = DEEPNORM_ALPHA * h_ref[0] + m[5:6, :] * acc_ref[...]
        o_ref[0] = _ln_plain(r) * g_ref[...] + b_ref[...]


def _ffn(u, w_gate, w_up, w_down, h, mod, ln_g, ln_b):
    bsz, s, d = h.shape
    f = w_gate.shape[1]
    tm, tf = FFN_ROWS, FFN_COLS
    row = lambda b, i, j: (b, i, 0)
    const3 = lambda shape: pl.BlockSpec(shape, lambda b, i, j: (0, 0))
    return pl.pallas_call(
        _ffn_kernel,
        grid=(bsz, s // tm, f // tf),
        in_specs=[pl.BlockSpec((1, tm, d), row),
                  pl.BlockSpec((d, tf), lambda b, i, j: (0, j)),
                  pl.BlockSpec((d, tf), lambda b, i, j: (0, j)),
                  pl.BlockSpec((tf, d), lambda b, i, j: (j, 0)),
                  pl.BlockSpec((1, tm, d), row),
                  pl.BlockSpec((1, 6, d), lambda b, i, j: (b, 0, 0)),
                  const3((1, d)), const3((1, d))],
        out_specs=pl.BlockSpec((1, tm, d), row),
        out_shape=jax.ShapeDtypeStruct((bsz, s, d), F32),
        scratch_shapes=[pltpu.VMEM((tm, d), F32)],
        compiler_params=pltpu.CompilerParams(
            dimension_semantics=("arbitrary", "arbitrary", "arbitrary"),
            vmem_limit_bytes=VMEM_LIMIT),
        name="ffn_ln2",
    )(u, w_gate, w_up, w_down, h, mod, ln_g.reshape(1, d), ln_b.reshape(1, d))


def kernel(x, c, positions, w_ada, b_ada, w_in, lb_logits, hg_norm_w, lam_q1, lam_k1, lam_q2, lam_k2,
           subln_w, w_out, ln1_g, ln1_b, w_gate, w_up, w_down, ln2_g, ln2_b):
    assert w_ada.shape[0] == DEPTH == 1
    bsz, s, d = x.shape

    mod = _adaln(c, w_ada[0], b_ada[0]).reshape(bsz, 6, d)
    cos, sin, w_in_b = _rope_tables(positions, w_in[0])

    qs, lf, ks, vs, gs, dq, dk, dv = _inproj(x, mod, w_in_b, lb_logits, cos, sin)
    y_hg, w_out_b, w_gate_b, w_up_b = _hgrn(qs, lf, ks, vs, gs, hg_norm_w[0],
                                            [w_out[0], w_gate[0], w_up[0]])
    y_da, w_down_b = _attn(dq, dk, dv, lam_q1[0], lam_k1[0], lam_q2[0], lam_k2[0], subln_w[0],
                           [w_down[0]])
    h1, u2 = _outproj(y_hg, y_da, w_out_b, x, mod, ln1_g[0], ln1_b[0])
    return _ffn(u2, w_gate_b, w_up_b, w_down_b, h1, mod, ln2_g[0], ln2_b[0])
```
